```python
import jax, jax.numpy as jnp
from jax import lax
import numpy as np

D_MODEL = 1024
BATCH = 2
SEQ = 8192
DEPTH = 2
DEC_BATCH = 32
DEC_SEQ = 2048
PAST_LEN = 128

GRID_W = 64
N_MEM = 256
D_MIX = 2 * D_MODEL
GROUP_W = D_MIX // 4
HEAD_DIM = 64
N_HEADS = GROUP_W // HEAD_DIM
N_KV_HEADS = 2
Q_PER_KV = N_HEADS // N_KV_HEADS
ROPE_PAIRS = HEAD_DIM // 4
ROPE_THETA = 10000.0
Q_BLOCK = 128
N_FOURIER_GROUPS = 4
FOURIER_W = GROUP_W // N_FOURIER_GROUPS
N_SGU_HEADS = 4
SGU_W = GROUP_W // N_SGU_HEADS
CHUNK = 128
N_MEM_HEADS = 4
MEM_HEAD_DIM = GROUP_W // N_MEM_HEADS
EPS = 1e-6
SPLIT_SIZES = (GROUP_W, N_KV_HEADS * HEAD_DIM, N_KV_HEADS * HEAD_DIM, GROUP_W,
               GROUP_W, GROUP_W, GROUP_W, GROUP_W, GROUP_W, GROUP_W, GROUP_W)
IN_W = sum(SPLIT_SIZES)
SPLIT_POINTS = tuple(int(v) for v in np.cumsum(SPLIT_SIZES)[:-1])

kernel_name = "hybrid_parallel_group_encoder"


def rms_norm(x, g):
    xf = x.astype(jnp.float32)
    y = xf * lax.rsqrt(jnp.mean(xf * xf, axis=-1, keepdims=True) + EPS)
    return (y * g.astype(jnp.float32)).astype(x.dtype)


def axial_rope_tables(S):
    rows = S // GRID_W
    row = jnp.broadcast_to(jnp.arange(rows, dtype=jnp.float32)[:, None], (rows, GRID_W)).reshape(S)
    col = jnp.broadcast_to(jnp.arange(GRID_W, dtype=jnp.float32)[None, :], (rows, GRID_W)).reshape(S)
    inv = ROPE_THETA ** (-jnp.arange(ROPE_PAIRS, dtype=jnp.float32) / ROPE_PAIRS)
    ang = jnp.stack([row[:, None] * inv, col[:, None] * inv], axis=1)
    return jnp.cos(ang), jnp.sin(ang)


def apply_axial_rope(x, cos, sin):
    B, S, H, _ = x.shape
    xr = x.astype(jnp.float32).reshape(B, S, H, 2, 2, ROPE_PAIRS)
    x1, x2 = xr[..., 0, :], xr[..., 1, :]
    c = cos[None, :, None]
    s = sin[None, :, None]
    out = jnp.stack([x1 * c - x2 * s, x2 * c + x1 * s], axis=-2)
    return out.reshape(B, S, H, HEAD_DIM).astype(x.dtype)


def self_attention(q, k, v):
    B, S = q.shape[0], q.shape[1]
    nblk = S // Q_BLOCK
    scale = HEAD_DIM ** -0.5
    qb = q.reshape(B, nblk, Q_BLOCK, N_KV_HEADS, Q_PER_KV, HEAD_DIM).transpose(1, 0, 2, 3, 4, 5)

    def one_block(qblk):
        s = jnp.einsum('bqkgd,bskd->bkgqs', qblk, k, preferred_element_type=jnp.float32) * scale
        p = jax.nn.softmax(s, axis=-1)
        return jnp.einsum('bkgqs,bskd->bqkgd', p.astype(v.dtype), v)

    o = lax.map(one_block, qb)
    return o.transpose(1, 0, 2, 3, 4, 5).reshape(B, S, N_HEADS * HEAD_DIM)


def fourier_mix(a, w_f):
    B, S, _ = a.shape
    ag = a.astype(jnp.float32).reshape(B, S, N_FOURIER_GROUPS, FOURIER_W)
    f = jnp.fft.fft2(ag, axes=(1, 3), norm='ortho').real
    y = jnp.einsum('bsgc,gcd->bsgd', f, w_f.astype(jnp.float32))
    return y.reshape(B, S, GROUP_W).astype(a.dtype)


def spatial_gating(u, vv, v_g, w_s, b_s):
    B, S, _ = u.shape
    vh = rms_norm(vv.reshape(B, S, N_SGU_HEADS, SGU_W), v_g)
    vc = vh.reshape(B, S // CHUNK, CHUNK, N_SGU_HEADS, SGU_W)
    sp = jnp.einsum('hpq,bnqhc->bnphc', w_s, vc) + b_s.T[None, None, :, :, None]
    return u * sp.reshape(B, S, GROUP_W)


def memory_attention(cq, mem, mem_g, w_mem_kv):
    B, S, _ = cq.shape
    kv = rms_norm(mem, mem_g) @ w_mem_kv
    mk, mv = jnp.split(kv, 2, axis=-1)
    M = mem.shape[1]
    mk = mk.reshape(B, M, N_MEM_HEADS, MEM_HEAD_DIM)
    mv = mv.reshape(B, M, N_MEM_HEADS, MEM_HEAD_DIM)
    qh = cq.reshape(B, S, N_MEM_HEADS, MEM_HEAD_DIM)
    s = jnp.einsum('bshd,bmhd->bhsm', qh, mk, preferred_element_type=jnp.float32) * (MEM_HEAD_DIM ** -0.5)
    p = jax.nn.softmax(s, axis=-1)
    o = jnp.einsum('bhsm,bmhd->bshd', p.astype(mv.dtype), mv)
    return o.reshape(B, S, GROUP_W)


def hybrid_layer(x, mem, cos, sin, pre_g, w_in, q_g, k_g, w_f, v_g, w_s, b_s, mem_g, w_mem_kv, w_out, post_g):
    B, S, _ = x.shape
    h = rms_norm(x, pre_g)
    z = h @ w_in
    (aq, ak, av, ag, fa, fg, su, sv, sg, mq, mg) = jnp.split(z, SPLIT_POINTS, axis=-1)
    q = rms_norm(aq.reshape(B, S, N_HEADS, HEAD_DIM), q_g)
    k = rms_norm(ak.reshape(B, S, N_KV_HEADS, HEAD_DIM), k_g)
    v = av.reshape(B, S, N_KV_HEADS, HEAD_DIM)
    q = apply_axial_rope(q, cos, sin)
    k = apply_axial_rope(k, cos, sin)
    o_att = self_attention(q, k, v) * jax.nn.silu(ag)
    o_four = fourier_mix(fa, w_f) * jax.nn.silu(fg)
    o_sgu = spatial_gating(su, sv, v_g, w_s, b_s) * jax.nn.silu(sg)
    o_mem = memory_attention(mq, mem, mem_g, w_mem_kv) * jax.nn.silu(mg)
    o = jnp.concatenate([o_att, o_four, o_sgu, o_mem], axis=-1) @ w_out
    return x + rms_norm(o, post_g)


def trunk(x, mem, pre_norm_g, w_in, q_norm_g, k_norm_g, w_fourier, sgu_norm_g, w_spatial,
          b_spatial, mem_norm_g, w_mem_kv, w_out, post_norm_g):
    cos, sin = axial_rope_tables(x.shape[1])
    for l in range(DEPTH):
        x = hybrid_layer(x, mem, cos, sin, pre_norm_g[l], w_in[l], q_norm_g[l], k_norm_g[l],
                         w_fourier[l], sgu_norm_g[l], w_spatial[l], b_spatial[l],
                         mem_norm_g[l], w_mem_kv[l], w_out[l], post_norm_g[l])
    return x


def setup_inputs(seed: int = 0) -> dict:
    key = jax.random.key(seed)
    ks = jax.random.split(key, 17)
    f32 = jnp.float32
    nrm = lambda k, shp, s: jax.random.normal(k, shp, f32) * s
    gain = lambda k, shp: 1.0 + 0.02 * jax.random.normal(k, shp, f32)
    return {
        "x_prompt": nrm(ks[0], (BATCH, SEQ, D_MODEL), 1.0),
        "x_sample": nrm(ks[1], (DEC_BATCH, DEC_SEQ, D_MODEL), 1.0),
        "mem_prompt": nrm(ks[2], (BATCH, N_MEM, D_MODEL), 1.0),
        "mem_sample": nrm(ks[3], (DEC_BATCH, N_MEM, D_MODEL), 1.0),
        "pre_norm_g": gain(ks[4], (DEPTH, D_MODEL)),
        "w_in": nrm(ks[5], (DEPTH, D_MODEL, IN_W), D_MODEL ** -0.5),
        "q_norm_g": gain(ks[6], (DEPTH, HEAD_DIM)),
        "k_norm_g": gain(ks[7], (DEPTH, HEAD_DIM)),
        "w_fourier": nrm(ks[8], (DEPTH, N_FOURIER_GROUPS, FOURIER_W, FOURIER_W), FOURIER_W ** -0.5),
        "sgu_norm_g": gain(ks[9], (DEPTH, N_SGU_HEADS, SGU_W)),
        "w_spatial": nrm(ks[10], (DEPTH, N_SGU_HEADS, CHUNK, CHUNK), CHUNK ** -0.5),
        "b_spatial": nrm(ks[11], (DEPTH, N_SGU_HEADS, CHUNK), 0.02),
        "mem_norm_g": gain(ks[12], (DEPTH, D_MODEL)),
        "w_mem_kv": nrm(ks[13], (DEPTH, D_MODEL, 2 * GROUP_W), D_MODEL ** -0.5),
        "w_out": nrm(ks[14], (DEPTH, D_MIX, D_MODEL), D_MIX ** -0.5),
        "post_norm_g": gain(ks[15], (DEPTH, D_MODEL)),
    }


def reference(x_prompt, x_sample, mem_prompt, mem_sample, pre_norm_g, w_in, q_norm_g, k_norm_g,
              w_fourier, sgu_norm_g, w_spatial, b_spatial, mem_norm_g, w_mem_kv, w_out, post_norm_g):
    y_prompt = trunk(x_prompt, mem_prompt, pre_norm_g, w_in, q_norm_g, k_norm_g, w_fourier,
                     sgu_norm_g, w_spatial, b_spatial, mem_norm_g, w_mem_kv, w_out, post_norm_g)
    y_sample = trunk(x_sample, mem_sample, pre_norm_g, w_in, q_norm_g, k_norm_g, w_fourier,
                     sgu_norm_g, w_spatial, b_spatial, mem_norm_g, w_mem_kv, w_out, post_norm_g)
    return (y_prompt, y_sample)
```

```python
import functools

import numpy as np
import jax
import jax.numpy as jnp
from jax import lax
from jax.experimental import pallas as pl
from jax.experimental.pallas import tpu as pltpu

F32 = jnp.float32
BF16 = jnp.bfloat16

D_MODEL = 1024
DEPTH = 2
GRID_W = 64
N_MEM = 256
GROUP_W = 512
HEAD_DIM = 64
N_HEADS = 8
N_KV_HEADS = 2
Q_PER_KV = N_HEADS // N_KV_HEADS
ROPE_PAIRS = HEAD_DIM // 4
ROPE_THETA = 10000.0
N_FOURIER_GROUPS = 4
FOURIER_W = 128
N_SGU_HEADS = 4
SGU_W = 128
CHUNK = 128
N_MEM_HEADS = 4
MEM_HEAD_DIM = 128
EPS = 1e-6
IN_W = 4864
C_AQ, C_AK, C_AV, C_AG, C_FA, C_FG, C_SU, C_SV, C_SG, C_MQ, C_MG = (
    0, 512, 640, 768, 1280, 1792, 2304, 2816, 3328, 3840, 4352)

LANES = 128
SUBLANES = 8
VMEM_LIMIT_BYTES = 56 * 1024 * 1024

TOKEN_TILE = 512
KV_CHUNK = 512
Q_TILE = 128
DFT_N2 = 128


def _cparams(*sem):
    return pltpu.CompilerParams(dimension_semantics=sem, vmem_limit_bytes=VMEM_LIMIT_BYTES)


def _const_spec(shape):
    nd = len(shape)
    return pl.BlockSpec(shape, lambda *_: (0,) * nd)


def _silu(g):
    return g / (1.0 + jnp.exp(-g))


def _rms(x, g):
    ms = jnp.mean(x * x, axis=-1, keepdims=True)
    return x * lax.rsqrt(ms + EPS) * g


def _memkv_kernel(mem_ref, g_ref, w_ref, kv_ref):
    h = _rms(mem_ref[...], g_ref[...]).astype(BF16)
    kv_ref[...] = jnp.dot(h, w_ref[...], preferred_element_type=F32).astype(BF16)


def _memkv(mem, g, w):
    B = mem.shape[0]
    return pl.pallas_call(
        _memkv_kernel,
        grid=(B,),
        in_specs=[pl.BlockSpec((None, N_MEM, D_MODEL), lambda b: (b, 0, 0)),
                  _const_spec((1, D_MODEL)),
                  _const_spec((D_MODEL, 2 * GROUP_W))],
        out_specs=pl.BlockSpec((None, N_MEM, 2 * GROUP_W), lambda b: (b, 0, 0)),
        out_shape=jax.ShapeDtypeStruct((B, N_MEM, 2 * GROUP_W), BF16),
        compiler_params=_cparams("parallel"),
        name="memkv",
    )(mem, g, w)


def _head_ssq(x, bd):
    sq = x * x
    hi = sq.astype(BF16)
    lo = (sq - hi.astype(F32)).astype(BF16)
    return (jnp.dot(hi, bd, preferred_element_type=F32)
            + jnp.dot(lo, bd, preferred_element_type=F32))


def _rope(x, cos, sina, sinb):
    return x * cos + pltpu.roll(x, 16, 1) * sina + pltpu.roll(x, LANES - 16, 1) * sinb


def _inproj_kernel(x_ref, pre_g_ref, w_ref, bd_ref, qg_ref, kg_ref, cos_ref, sina_ref, sinb_ref,
                   vg_ref, ws_ref, bs_ref, kv_ref,
                   q_ref, k_ref, vt_ref, ag_ref, fa_ref, fg_ref, osgu_ref, omem_ref):
    tm = x_ref.shape[0]
    h = _rms(x_ref[...], pre_g_ref[...]).astype(BF16)

    def proj(lo, width):
        return jnp.dot(h, w_ref[:, lo:lo + width], preferred_element_type=F32)

    cos, sina, sinb = cos_ref[...], sina_ref[...], sinb_ref[...]

    aq = proj(C_AQ, GROUP_W)
    ssq = _head_ssq(aq, bd_ref[...])
    qn = aq * lax.rsqrt(ssq * (1.0 / HEAD_DIM) + EPS) * qg_ref[...]
    for j in range(GROUP_W // LANES):
        sl = slice(j * LANES, (j + 1) * LANES)
        q_ref[:, sl] = (_rope(qn[:, sl], cos, sina, sinb) * (HEAD_DIM ** -0.5)).astype(BF16)

    ak = proj(C_AK, LANES)
    ssk = _head_ssq(ak, bd_ref[0:LANES, 0:LANES])
    kn = ak * lax.rsqrt(ssk * (1.0 / HEAD_DIM) + EPS) * kg_ref[...]
    k_ref[...] = _rope(kn, cos, sina, sinb).astype(BF16)
    av = proj(C_AV, LANES)
    vt_ref[...] = av.T.astype(BF16)

    ag_ref[...] = proj(C_AG, GROUP_W).astype(BF16)
    fa_ref[...] = proj(C_FA, GROUP_W)
    fg_ref[...] = proj(C_FG, GROUP_W)

    su = proj(C_SU, GROUP_W)
    sv = proj(C_SV, GROUP_W)
    sg = proj(C_SG, GROUP_W)
    gate = _silu(sg)
    for hd in range(N_SGU_HEADS):
        sl = slice(hd * SGU_W, (hd + 1) * SGU_W)
        vh = _rms(sv[:, sl], vg_ref[:, sl]).astype(BF16)
        w_s = ws_ref[hd]
        for c in range(tm // CHUNK):
            rows = slice(c * CHUNK, (c + 1) * CHUNK)
            sp = jnp.dot(w_s, vh[rows, :], preferred_element_type=F32) + bs_ref[:, sl]
            osgu_ref[rows, sl] = (su[rows, sl] * sp * gate[rows, sl]).astype(BF16)

    mq = proj(C_MQ, GROUP_W)
    mg = proj(C_MG, GROUP_W)
    mgate = _silu(mg)
    for hd in range(N_MEM_HEADS):
        sl = slice(hd * MEM_HEAD_DIM, (hd + 1) * MEM_HEAD_DIM)
        mk = kv_ref[:, hd * MEM_HEAD_DIM:(hd + 1) * MEM_HEAD_DIM]
        mv = kv_ref[:, GROUP_W + hd * MEM_HEAD_DIM:GROUP_W + (hd + 1) * MEM_HEAD_DIM]
        s = lax.dot_general(mq[:, sl].astype(BF16), mk, (((1,), (1,)), ((), ())),
                            preferred_element_type=F32) * (MEM_HEAD_DIM ** -0.5)
        e = jnp.exp(s - jnp.max(s, axis=-1, keepdims=True))
        o = jnp.dot(e.astype(BF16), mv, preferred_element_type=F32)
        o = o / jnp.sum(e, axis=-1, keepdims=True)
        omem_ref[:, sl] = (o * mgate[:, sl]).astype(BF16)


def _inproj(x, kv, lw, tabs):
    B, S, _ = x.shape
    tm = TOKEN_TILE
    nt = S // tm
    tok = lambda w: pl.BlockSpec((None, tm, w), lambda b, i: (b, i, 0))
    pos = pl.BlockSpec((tm, LANES), lambda b, i: (i, 0))
    in_specs = [
        tok(D_MODEL),
        _const_spec((1, D_MODEL)),
        _const_spec((D_MODEL, IN_W)),
        _const_spec((GROUP_W, GROUP_W)),
        _const_spec((1, GROUP_W)),
        _const_spec((1, LANES)),
        pos, pos, pos,
        _const_spec((1, GROUP_W)),
        _const_spec((N_SGU_HEADS, CHUNK, CHUNK)),
        _const_spec((CHUNK, GROUP_W)),
        pl.BlockSpec((None, N_MEM, 2 * GROUP_W), lambda b, i: (b, 0, 0)),
    ]
    out_specs = [
        tok(GROUP_W), tok(LANES),
        pl.BlockSpec((None, None, LANES, tm), lambda b, i: (b, i, 0, 0)),
        tok(GROUP_W), tok(GROUP_W), tok(GROUP_W), tok(GROUP_W), tok(GROUP_W),
    ]
    sds = jax.ShapeDtypeStruct
    out_shape = [
        sds((B, S, GROUP_W), BF16), sds((B, S, LANES), BF16),
        sds((B, nt, LANES, tm), BF16),
        sds((B, S, GROUP_W), BF16), sds((B, S, GROUP_W), F32), sds((B, S, GROUP_W), F32),
        sds((B, S, GROUP_W), BF16), sds((B, S, GROUP_W), BF16),
    ]
    return pl.pallas_call(
        _inproj_kernel,
        grid=(B, nt),
        in_specs=in_specs, out_specs=out_specs, out_shape=out_shape,
        compiler_params=_cparams("parallel", "parallel"),
        name="inproj",
    )(x, lw["pre_g"], lw["w_in"], tabs["bd"], lw["qg"], lw["kg"],
      tabs["cos"], tabs["sina"], tabs["sinb"], lw["vg"], lw["ws"], lw["bs"], kv)


def _attn_kernel(q_ref, k_ref, vt_ref, ag_ref, o_ref, qt_ref, m_ref, l_ref, acc_ref):
    tq = q_ref.shape[0]
    nchunks, _, tk = vt_ref.shape
    lane = lax.broadcasted_iota(jnp.int32, (tq, LANES), 1)
    low = lane < HEAD_DIM

    q = q_ref[...].astype(F32)
    parts = []
    for j in range(GROUP_W // LANES):
        slab = q[:, j * LANES:(j + 1) * LANES]
        rolled = pltpu.roll(slab, HEAD_DIM, 1)
        if j < Q_PER_KV // 2:
            parts += [jnp.where(low, slab, 0.0), jnp.where(low, rolled, 0.0)]
        else:
            parts += [jnp.where(low, 0.0, rolled), jnp.where(low, 0.0, slab)]
    qt_ref[...] = jnp.concatenate(parts, axis=0).T.astype(BF16)

    m_ref[...] = jnp.full(m_ref.shape, -jnp.inf, F32)
    l_ref[...] = jnp.zeros(l_ref.shape, F32)
    acc_ref[...] = jnp.zeros(acc_ref.shape, F32)

    def chunk(c, carry):
        kc = k_ref[pl.ds(pl.multiple_of(c * tk, tk), tk), :]
        st = jnp.dot(kc, qt_ref[...], preferred_element_type=F32)
        m_old = m_ref[...]
        m_new = jnp.maximum(m_old, jnp.max(st, axis=0, keepdims=True))
        alpha = jnp.exp(m_old - m_new)
        p = jnp.exp(st - m_new)
        l_ref[...] = alpha * l_ref[...] + jnp.sum(p, axis=0, keepdims=True)
        acc_ref[...] = alpha * acc_ref[...] + jnp.dot(vt_ref[c], p.astype(BF16),
                                                      preferred_element_type=F32)
        m_ref[...] = m_new
        return carry

    lax.fori_loop(0, nchunks, chunk, 0)

    ot = (acc_ref[...] / l_ref[...]).T
    gate = _silu(ag_ref[...].astype(F32))
    for j in range(GROUP_W // LANES):
        a = ot[(2 * j) * tq:(2 * j + 1) * tq, :]
        b = ot[(2 * j + 1) * tq:(2 * j + 2) * tq, :]
        if j < Q_PER_KV // 2:
            slab = jnp.where(low, a, pltpu.roll(b, HEAD_DIM, 1))
        else:
            slab = jnp.where(low, pltpu.roll(a, HEAD_DIM, 1), b)
        sl = slice(j * LANES, (j + 1) * LANES)
        o_ref[:, sl] = (slab * gate[:, sl]).astype(BF16)


def _attention(q, k, vt, ag):
    B, S, _ = q.shape
    tq = Q_TILE
    nchunks, tk = vt.shape[1], vt.shape[3]
    rows = N_HEADS * tq
    tok = pl.BlockSpec((None, tq, GROUP_W), lambda b, i: (b, i, 0))
    return pl.pallas_call(
        _attn_kernel,
        grid=(B, S // tq),
        in_specs=[tok,
                  pl.BlockSpec((None, S, LANES), lambda b, i: (b, 0, 0)),
                  pl.BlockSpec((None, nchunks, LANES, tk), lambda b, i: (b, 0, 0, 0)),
                  tok],
        out_specs=tok,
        out_shape=jax.ShapeDtypeStruct((B, S, GROUP_W), BF16),
        scratch_shapes=[pltpu.VMEM((LANES, rows), BF16),
                        pltpu.VMEM((1, rows), F32),
                        pltpu.VMEM((1, rows), F32),
                        pltpu.VMEM((LANES, rows), F32)],
        compiler_params=_cparams("parallel", "parallel"),
        name="attn",
    )(q, k, vt, ag)


def _dft_tables(S):
    n2 = DFT_N2
    n1 = S // n2
    c = np.arange(FOURIER_W)
    ang = 2.0 * np.pi * np.outer(c, c) / FOURIER_W
    chan = np.concatenate([np.cos(ang), -np.sin(ang)], axis=1) / np.sqrt(FOURIER_W)
    a1 = 2.0 * np.pi * np.outer(np.arange(n1), np.arange(n1)) / n1
    eye = np.eye(SUBLANES)
    mr = np.kron(np.cos(a1), eye) / np.sqrt(n1)
    mi = np.kron(-np.sin(a1), eye) / np.sqrt(n1)
    m1 = np.block([[mr, -mi], [mi, mr]])
    k1 = np.arange(n1)[None, :, None]
    s2 = (np.arange(n2 // SUBLANES)[:, None, None] * SUBLANES + np.arange(SUBLANES)[None, None, :])
    at = 2.0 * np.pi * (k1 * s2) / S
    twr = np.cos(at).reshape(n2 // SUBLANES, n1 * SUBLANES, 1)
    twi = (-np.sin(at)).reshape(n2 // SUBLANES, n1 * SUBLANES, 1)
    a2 = 2.0 * np.pi * np.outer(np.arange(n2), np.arange(n2)) / n2
    c2 = (np.cos(a2) / np.sqrt(n2))[:, None, None, :] * eye[None, :, :, None]
    s2m = (np.sin(a2) / np.sqrt(n2))[:, None, None, :] * eye[None, :, :, None]
    m2 = np.concatenate([c2.reshape(n2 * SUBLANES, n2 * SUBLANES),
                         s2m.reshape(n2 * SUBLANES, n2 * SUBLANES)], axis=1)
    return dict(chan=jnp.asarray(chan, BF16), m1=jnp.asarray(m1, BF16),
                twr=jnp.asarray(twr, F32), twi=jnp.asarray(twi, F32), m2=jnp.asarray(m2, BF16))


def _four1_kernel(fa_ref, chan_ref, m1_ref, twr_ref, twi_ref, br_ref, bi_ref):
    n1, sb, _ = fa_ref.shape
    rows = n1 * SUBLANES
    for i in range(sb // SUBLANES):
        rs = slice(i * SUBLANES, (i + 1) * SUBLANES)
        x = fa_ref[:, rs, :].reshape(rows, GROUP_W).astype(BF16)
        zr, zi = [], []
        for g in range(N_FOURIER_GROUPS):
            z = jnp.dot(x[:, g * FOURIER_W:(g + 1) * FOURIER_W], chan_ref[...],
                        preferred_element_type=F32)
            zr.append(z[:, :FOURIER_W])
            zi.append(z[:, FOURIER_W:])
        zcat = jnp.concatenate([jnp.concatenate(zr, axis=1), jnp.concatenate(zi, axis=1)],
                               axis=0).astype(BF16)
        a = jnp.dot(m1_ref[...], zcat, preferred_element_type=F32)
        ar, ai = a[:rows], a[rows:]
        twr, twi = twr_ref[i], twi_ref[i]
        br_ref[:, rs, :] = (ar * twr - ai * twi).reshape(n1, SUBLANES, GROUP_W)
        bi_ref[:, rs, :] = (ar * twi + ai * twr).reshape(n1, SUBLANES, GROUP_W)


def _four2_kernel(br_ref, bi_ref, m2_ref, wf_ref, fg_ref, o_ref):
    _, n2, _ = br_ref.shape
    rows = SUBLANES * n2
    bcat = jnp.concatenate([br_ref[...].reshape(rows, GROUP_W), bi_ref[...].reshape(rows, GROUP_W)],
                           axis=0).astype(BF16)
    f = jnp.dot(m2_ref[...], bcat, preferred_element_type=F32).astype(BF16)
    gate = _silu(fg_ref[...].reshape(rows, GROUP_W))
    ys = [jnp.dot(f[:, g * FOURIER_W:(g + 1) * FOURIER_W], wf_ref[g], preferred_element_type=F32)
          for g in range(N_FOURIER_GROUPS)]
    y = jnp.concatenate(ys, axis=1) * gate
    o_ref[...] = y.reshape(n2, SUBLANES, GROUP_W)


def _fourier(fa, fg, wf, ft):
    B, S, _ = fa.shape
    n2 = DFT_N2
    n1 = S // n2
    sb = 32 if n1 <= 16 else SUBLANES
    nsub = sb // SUBLANES
    fa4 = fa.reshape(B, n1, n2, GROUP_W)
    blk1 = pl.BlockSpec((None, n1, sb, GROUP_W), lambda b, j: (b, 0, j, 0))
    tw = pl.BlockSpec((nsub, n1 * SUBLANES, 1), lambda b, j: (j, 0, 0))
    br, bi = pl.pallas_call(
        _four1_kernel,
        grid=(B, n2 // sb),
        in_specs=[blk1, _const_spec((FOURIER_W, 2 * FOURIER_W)),
                  _const_spec((2 * n1 * SUBLANES, 2 * n1 * SUBLANES)), tw, tw],
        out_specs=[blk1, blk1],
        out_shape=[jax.ShapeDtypeStruct((B, n1, n2, GROUP_W), F32)] * 2,
        compiler_params=_cparams("parallel", "parallel"),
        name="four1",
    )(fa4, ft["chan"], ft["m1"], ft["twr"], ft["twi"])
    blk_in = pl.BlockSpec((None, SUBLANES, n2, GROUP_W), lambda b, j: (b, j, 0, 0))
    blk_out = pl.BlockSpec((None, n2, SUBLANES, GROUP_W), lambda b, j: (b, 0, j, 0))
    o = pl.pallas_call(
        _four2_kernel,
        grid=(B, n1 // SUBLANES),
        in_specs=[blk_in, blk_in, _const_spec((SUBLANES * n2, 2 * SUBLANES * n2)),
                  _const_spec((N_FOURIER_GROUPS, FOURIER_W, FOURIER_W)), blk_out],
        out_specs=blk_out,
        out_shape=jax.ShapeDtypeStruct((B, n2, n1, GROUP_W), F32),
        compiler_params=_cparams("parallel", "parallel"),
        name="four2",
    )(br, bi, ft["m2"], wf, fg.reshape(B, n2, n1, GROUP_W))
    return o.reshape(B, S, GROUP_W)


def _outproj_kernel(oa_ref, of_ref, os_ref, om_ref, x_ref, w_ref, g_ref, y_ref):
    o = jnp.concatenate([oa_ref[...], of_ref[...].astype(BF16), os_ref[...], om_ref[...]], axis=1)
    y = jnp.dot(o, w_ref[...], preferred_element_type=F32)
    y_ref[...] = x_ref[...] + _rms(y, g_ref[...])


def _outproj(oa, of, osg, om, x, w, g):
    B, S, _ = x.shape
    tm = TOKEN_TILE
    tok = lambda wd: pl.BlockSpec((None, tm, wd), lambda b, i: (b, i, 0))
    return pl.pallas_call(
        _outproj_kernel,
        grid=(B, S // tm),
        in_specs=[tok(GROUP_W), tok(GROUP_W), tok(GROUP_W), tok(GROUP_W), tok(D_MODEL),
                  _const_spec((4 * GROUP_W, D_MODEL)), _const_spec((1, D_MODEL))],
        out_specs=tok(D_MODEL),
        out_shape=jax.ShapeDtypeStruct((B, S, D_MODEL), F32),
        compiler_params=_cparams("parallel", "parallel"),
        name="outproj",
    )(oa, of, osg, om, x, w, g)


def _rope_tables(S):
    rows = S // GRID_W
    row = jnp.broadcast_to(jnp.arange(rows, dtype=F32)[:, None], (rows, GRID_W)).reshape(S)
    col = jnp.broadcast_to(jnp.arange(GRID_W, dtype=F32)[None, :], (rows, GRID_W)).reshape(S)
    inv = ROPE_THETA ** (-jnp.arange(ROPE_PAIRS, dtype=F32) / ROPE_PAIRS)
    ang = jnp.stack([row[:, None] * inv, col[:, None] * inv], axis=1)
    cos, sin = jnp.cos(ang), jnp.sin(ang)
    zero = jnp.zeros((S, ROPE_PAIRS), F32)
    two = lambda parts: jnp.tile(jnp.concatenate(parts, axis=-1), (1, LANES // HEAD_DIM))
    return dict(cos=two([cos[:, 0], cos[:, 0], cos[:, 1], cos[:, 1]]),
                sina=two([zero, sin[:, 0], zero, sin[:, 1]]),
                sinb=two([-sin[:, 0], zero, -sin[:, 1], zero]))


def _layer_weights(l, pre_norm_g, w_in, q_norm_g, k_norm_g, w_fourier, sgu_norm_g, w_spatial,
                   b_spatial, mem_norm_g, w_mem_kv, w_out, post_norm_g):
    return dict(
        pre_g=pre_norm_g[l][None, :],
        w_in=w_in[l].astype(BF16),
        qg=jnp.tile(q_norm_g[l], N_HEADS)[None, :],
        kg=jnp.tile(k_norm_g[l], N_KV_HEADS)[None, :],
        wf=w_fourier[l].astype(BF16),
        vg=sgu_norm_g[l].reshape(1, GROUP_W),
        ws=w_spatial[l].astype(BF16),
        bs=jnp.repeat(b_spatial[l].T, SGU_W, axis=1),
        mem_g=mem_norm_g[l][None, :],
        w_mem_kv=w_mem_kv[l].astype(BF16),
        w_out=w_out[l].astype(BF16),
        post_g=post_norm_g[l][None, :],
    )


def _trunk(x, mem, layers):
    S = x.shape[1]
    tabs = _rope_tables(S)
    ids = np.arange(GROUP_W) // HEAD_DIM
    tabs["bd"] = jnp.asarray(ids[:, None] == ids[None, :], BF16)
    ft = _dft_tables(S)
    for lw in layers:
        kv = _memkv(mem, lw["mem_g"], lw["w_mem_kv"])
        q, k, vt, ag, fa, fg, osgu, omem = _inproj(x, kv, lw, tabs)
        oatt = _attention(q, k, vt, ag)
        ofour = _fourier(fa, fg, lw["wf"], ft)
        x = _outproj(oatt, ofour, osgu, omem, x, lw["w_out"], lw["post_g"])
    return x


def kernel(x_prompt, x_sample, mem_prompt, mem_sample, pre_norm_g, w_in, q_norm_g, k_norm_g,
           w_fourier, sgu_norm_g, w_spatial, b_spatial, mem_norm_g, w_mem_kv, w_out, post_norm_g):
    layers = [_layer_weights(l, pre_norm_g, w_in, q_norm_g, k_norm_g, w_fourier, sgu_norm_g,
                             w_spatial, b_spatial, mem_norm_g, w_mem_kv, w_out, post_norm_g)
              for l in range(DEPTH)]
    return (_trunk(x_prompt, mem_prompt, layers), _trunk(x_sample, mem_sample, layers))
```

```python
import functools

import numpy as np
import jax
import jax.numpy as jnp
from jax import lax
from jax.experimental import pallas as pl
from jax.experimental.pallas import tpu as pltpu

F32 = jnp.float32
BF16 = jnp.bfloat16

D_MODEL = 1024
DEPTH = 2
GRID_W = 64
N_MEM = 256
GROUP_W = 512
HEAD_DIM = 64
N_HEADS = 8
N_KV_HEADS = 2
Q_PER_KV = N_HEADS // N_KV_HEADS
ROPE_PAIRS = HEAD_DIM // 4
ROPE_THETA = 10000.0
N_FOURIER_GROUPS = 4
FOURIER_W = 128
N_SGU_HEADS = 4
SGU_W = 128
CHUNK = 128
N_MEM_HEADS = 4
MEM_HEAD_DIM = 128
EPS = 1e-6
IN_W = 4864
C_AQ, C_AK, C_AV, C_AG, C_FA, C_FG, C_SU, C_SV, C_SG, C_MQ, C_MG = (
    0, 512, 640, 768, 1280, 1792, 2304, 2816, 3328, 3840, 4352)

LANES = 128
SUBLANES = 8
VMEM_LIMIT_BYTES = 56 * 1024 * 1024

TOKEN_TILE = 512
KV_CHUNK = 512
Q_TILE = 128
DFT_N2 = 128
ATTN_UNROLL = 4
BF16_SUBLANES = 16
VT_ROWS = LANES + BF16_SUBLANES
Q_SCALE = float(HEAD_DIM ** -0.5 * np.log2(np.e))


def _cparams(*sem):
    return pltpu.CompilerParams(dimension_semantics=sem, vmem_limit_bytes=VMEM_LIMIT_BYTES)


def _const_spec(shape):
    nd = len(shape)
    return pl.BlockSpec(shape, lambda *_: (0,) * nd)


def _silu(g):
    return g / (1.0 + jnp.exp(-g))


def _rms(x, g):
    ms = jnp.mean(x * x, axis=-1, keepdims=True)
    return x * lax.rsqrt(ms + EPS) * g


def _memkv_kernel(mem_ref, g_ref, w_ref, kv_ref):
    h = _rms(mem_ref[...], g_ref[...]).astype(BF16)
    kv_ref[...] = jnp.dot(h, w_ref[...], preferred_element_type=F32).astype(BF16)


def _memkv(mem, g, w):
    B = mem.shape[0]
    return pl.pallas_call(
        _memkv_kernel,
        grid=(B,),
        in_specs=[pl.BlockSpec((None, N_MEM, D_MODEL), lambda b: (b, 0, 0)),
                  _const_spec((1, D_MODEL)),
                  _const_spec((D_MODEL, 2 * GROUP_W))],
        out_specs=pl.BlockSpec((None, N_MEM, 2 * GROUP_W), lambda b: (b, 0, 0)),
        out_shape=jax.ShapeDtypeStruct((B, N_MEM, 2 * GROUP_W), BF16),
        compiler_params=_cparams("parallel"),
        name="memkv",
    )(mem, g, w)


def _head_ssq(x, bd):
    return jnp.dot((x * x).astype(BF16), bd, preferred_element_type=F32)


def _rope(x, cos, sina, sinb):
    return x * cos + pltpu.roll(x, 16, 1) * sina + pltpu.roll(x, LANES - 16, 1) * sinb


def _inproj_kernel(x_ref, pre_g_ref, w_ref, bd_ref, qg_ref, kg_ref, cos_ref, sina_ref, sinb_ref,
                   vg_ref, ws_ref, bs_ref, kv_ref,
                   q_ref, k_ref, vt_ref, ag_ref, fa_ref, fg_ref, osgu_ref, omem_ref):
    tm = x_ref.shape[0]
    h = _rms(x_ref[...], pre_g_ref[...]).astype(BF16)

    def proj(lo, width):
        return jnp.dot(h, w_ref[:, lo:lo + width], preferred_element_type=F32)

    cos, sina, sinb = cos_ref[...], sina_ref[...], sinb_ref[...]

    aq = proj(C_AQ, GROUP_W)
    ssq = _head_ssq(aq, bd_ref[...])
    qn = aq * lax.rsqrt(ssq * (1.0 / HEAD_DIM) + EPS) * qg_ref[...]
    for j in range(GROUP_W // LANES):
        sl = slice(j * LANES, (j + 1) * LANES)
        q_ref[:, sl] = (_rope(qn[:, sl], cos, sina, sinb) * Q_SCALE).astype(BF16)

    ak = proj(C_AK, LANES)
    ssk = _head_ssq(ak, bd_ref[0:LANES, 0:LANES])
    kn = ak * lax.rsqrt(ssk * (1.0 / HEAD_DIM) + EPS) * kg_ref[...]
    k_ref[...] = _rope(kn, cos, sina, sinb).astype(BF16)
    av = proj(C_AV, LANES)
    vt_ref[0:LANES, :] = av.T.astype(BF16)
    vt_ref[LANES:VT_ROWS, :] = jnp.ones((VT_ROWS - LANES, tm), BF16)

    ag_ref[...] = proj(C_AG, GROUP_W).astype(BF16)
    fa_ref[...] = proj(C_FA, GROUP_W)
    fg_ref[...] = proj(C_FG, GROUP_W)

    su = proj(C_SU, GROUP_W)
    sv = proj(C_SV, GROUP_W)
    sg = proj(C_SG, GROUP_W)
    gate = _silu(sg)
    nck = tm // CHUNK
    for hd in range(N_SGU_HEADS):
        sl = slice(hd * SGU_W, (hd + 1) * SGU_W)
        vh = _rms(sv[:, sl], vg_ref[:, sl]).astype(BF16)
        vcat = jnp.concatenate([vh[c * CHUNK:(c + 1) * CHUNK, :] for c in range(nck)], axis=1)
        spc = jnp.dot(ws_ref[hd], vcat, preferred_element_type=F32)
        for c in range(nck):
            rows = slice(c * CHUNK, (c + 1) * CHUNK)
            sp = spc[:, c * SGU_W:(c + 1) * SGU_W] + bs_ref[:, sl]
            osgu_ref[rows, sl] = (su[rows, sl] * sp * gate[rows, sl]).astype(BF16)

    mq = proj(C_MQ, GROUP_W)
    mg = proj(C_MG, GROUP_W)
    mgate = _silu(mg)
    for hd in range(N_MEM_HEADS):
        sl = slice(hd * MEM_HEAD_DIM, (hd + 1) * MEM_HEAD_DIM)
        mk = kv_ref[:, hd * MEM_HEAD_DIM:(hd + 1) * MEM_HEAD_DIM]
        mv = kv_ref[:, GROUP_W + hd * MEM_HEAD_DIM:GROUP_W + (hd + 1) * MEM_HEAD_DIM]
        s = lax.dot_general(mq[:, sl].astype(BF16), mk, (((1,), (1,)), ((), ())),
                            preferred_element_type=F32) * (MEM_HEAD_DIM ** -0.5)
        e = jnp.exp(s - jnp.max(s, axis=-1, keepdims=True))
        o = jnp.dot(e.astype(BF16), mv, preferred_element_type=F32)
        o = o / jnp.sum(e, axis=-1, keepdims=True)
        omem_ref[:, sl] = (o * mgate[:, sl]).astype(BF16)


def _inproj(x, kv, lw, tabs):
    B, S, _ = x.shape
    tm = TOKEN_TILE
    nt = S // tm
    tok = lambda w: pl.BlockSpec((None, tm, w), lambda b, i: (b, i, 0))
    pos = pl.BlockSpec((tm, LANES), lambda b, i: (i, 0))
    in_specs = [
        tok(D_MODEL),
        _const_spec((1, D_MODEL)),
        _const_spec((D_MODEL, IN_W)),
        _const_spec((GROUP_W, GROUP_W)),
        _const_spec((1, GROUP_W)),
        _const_spec((1, LANES)),
        pos, pos, pos,
        _const_spec((1, GROUP_W)),
        _const_spec((N_SGU_HEADS, CHUNK, CHUNK)),
        _const_spec((CHUNK, GROUP_W)),
        pl.BlockSpec((None, N_MEM, 2 * GROUP_W), lambda b, i: (b, 0, 0)),
    ]
    out_specs = [
        tok(GROUP_W), tok(LANES),
        pl.BlockSpec((None, None, VT_ROWS, tm), lambda b, i: (b, i, 0, 0)),
        tok(GROUP_W), tok(GROUP_W), tok(GROUP_W), tok(GROUP_W), tok(GROUP_W),
    ]
    sds = jax.ShapeDtypeStruct
    out_shape = [
        sds((B, S, GROUP_W), BF16), sds((B, S, LANES), BF16),
        sds((B, nt, VT_ROWS, tm), BF16),
        sds((B, S, GROUP_W), BF16), sds((B, S, GROUP_W), F32), sds((B, S, GROUP_W), F32),
        sds((B, S, GROUP_W), BF16), sds((B, S, GROUP_W), BF16),
    ]
    return pl.pallas_call(
        _inproj_kernel,
        grid=(B, nt),
        in_specs=in_specs, out_specs=out_specs, out_shape=out_shape,
        compiler_params=_cparams("parallel", "parallel"),
        name="inproj",
    )(x, lw["pre_g"], lw["w_in"], tabs["bd"], lw["qg"], lw["kg"],
      tabs["cos"], tabs["sina"], tabs["sinb"], lw["vg"], lw["ws"], lw["bs"], kv)


def _attn_kernel(q_ref, k_ref, vt_ref, ag_ref, o_ref, qt_ref, st_ref, acc_ref):
    tq = q_ref.shape[0]
    nchunks, _, tk = vt_ref.shape
    lane = lax.broadcasted_iota(jnp.int32, (tq, LANES), 1)
    low = lane < HEAD_DIM

    q = q_ref[...].astype(F32)
    parts = []
    for j in range(GROUP_W // LANES):
        slab = q[:, j * LANES:(j + 1) * LANES]
        rolled = pltpu.roll(slab, HEAD_DIM, 1)
        if j < Q_PER_KV // 2:
            parts += [jnp.where(low, slab, 0.0), jnp.where(low, rolled, 0.0)]
        else:
            parts += [jnp.where(low, 0.0, rolled), jnp.where(low, 0.0, slab)]
    qt_ref[...] = jnp.concatenate(parts, axis=0).T.astype(BF16)

    acc_ref[...] = jnp.zeros(acc_ref.shape, F32)

    def scores(c, slot):
        start = c * tk if isinstance(c, int) else pl.multiple_of(c * tk, tk)
        kc = k_ref[pl.ds(start, tk), :]
        st_ref[slot] = jnp.dot(kc, qt_ref[...], preferred_element_type=F32)

    def softmax_pv(c, slot, m_old):
        st = st_ref[slot]
        m_new = jnp.maximum(m_old, jnp.max(st, axis=0, keepdims=True))
        alpha = jnp.exp2(m_old - m_new)
        p = jnp.exp2(st - m_new).astype(BF16)
        acc_ref[...] = alpha * acc_ref[...] + jnp.dot(vt_ref[c], p, preferred_element_type=F32)
        return m_new

    def step(c, slot, m):
        scores(c + 1, 1 - slot)
        return softmax_pv(c, slot, m)

    m = jnp.full((1, acc_ref.shape[1]), -jnp.inf, F32)
    scores(0, 0)
    steady = nchunks - 1
    groups = steady // ATTN_UNROLL
    if groups < 2:
        groups = 0
    peeled = steady - groups * ATTN_UNROLL
    for c in range(peeled):
        m = step(c, c % 2, m)
    if groups:
        def group(g, m):
            for u in range(ATTN_UNROLL):
                m = step(peeled + g * ATTN_UNROLL + u, (peeled + u) % 2, m)
            return m
        m = lax.fori_loop(0, groups, group, m)
    softmax_pv(nchunks - 1, (nchunks - 1) % 2, m)

    acc = acc_ref[...]
    ot = (acc[:LANES] / acc[LANES:LANES + 1]).T
    gate = _silu(ag_ref[...].astype(F32))
    for j in range(GROUP_W // LANES):
        a = ot[(2 * j) * tq:(2 * j + 1) * tq, :]
        b = ot[(2 * j + 1) * tq:(2 * j + 2) * tq, :]
        if j < Q_PER_KV // 2:
            slab = jnp.where(low, a, pltpu.roll(b, HEAD_DIM, 1))
        else:
            slab = jnp.where(low, pltpu.roll(a, HEAD_DIM, 1), b)
        sl = slice(j * LANES, (j + 1) * LANES)
        o_ref[:, sl] = (slab * gate[:, sl]).astype(BF16)


def _attention(q, k, vt, ag):
    B, S, _ = q.shape
    tq = Q_TILE
    nchunks, tk = vt.shape[1], vt.shape[3]
    rows = N_HEADS * tq
    tok = pl.BlockSpec((None, tq, GROUP_W), lambda b, i: (b, i, 0))
    return pl.pallas_call(
        _attn_kernel,
        grid=(B, S // tq),
        in_specs=[tok,
                  pl.BlockSpec((None, S, LANES), lambda b, i: (b, 0, 0)),
                  pl.BlockSpec((None, nchunks, VT_ROWS, tk), lambda b, i: (b, 0, 0, 0)),
                  tok],
        out_specs=tok,
        out_shape=jax.ShapeDtypeStruct((B, S, GROUP_W), BF16),
        scratch_shapes=[pltpu.VMEM((LANES, rows), BF16),
                        pltpu.VMEM((2, tk, rows), F32),
                        pltpu.VMEM((VT_ROWS, rows), F32)],
        compiler_params=_cparams("parallel", "parallel"),
        name="attn",
    )(q, k, vt, ag)


def _dft_tables(S):
    n2 = DFT_N2
    n1 = S // n2
    c = np.arange(FOURIER_W)
    ang = 2.0 * np.pi * np.outer(c, c) / FOURIER_W
    chan = np.concatenate([np.cos(ang), -np.sin(ang)], axis=1) / np.sqrt(FOURIER_W)
    a1 = 2.0 * np.pi * np.outer(np.arange(n1), np.arange(n1)) / n1
    eye = np.eye(SUBLANES)
    mr = np.kron(np.cos(a1), eye) / np.sqrt(n1)
    mi = np.kron(-np.sin(a1), eye) / np.sqrt(n1)
    m1 = np.block([[mr, -mi], [mi, mr]])
    k1 = np.arange(n1)[None, :, None]
    s2 = (np.arange(n2 // SUBLANES)[:, None, None] * SUBLANES + np.arange(SUBLANES)[None, None, :])
    at = 2.0 * np.pi * (k1 * s2) / S
    twr = np.cos(at).reshape(n2 // SUBLANES, n1 * SUBLANES, 1)
    twi = (-np.sin(at)).reshape(n2 // SUBLANES, n1 * SUBLANES, 1)
    a2 = 2.0 * np.pi * np.outer(np.arange(n2), np.arange(n2)) / n2
    c2 = (np.cos(a2) / np.sqrt(n2))[:, None, None, :] * eye[None, :, :, None]
    s2m = (np.sin(a2) / np.sqrt(n2))[:, None, None, :] * eye[None, :, :, None]
    m2 = np.concatenate([c2.reshape(n2 * SUBLANES, n2 * SUBLANES),
                         s2m.reshape(n2 * SUBLANES, n2 * SUBLANES)], axis=1)
    return dict(chan=jnp.asarray(chan, BF16), m1=jnp.asarray(m1, BF16),
                twr=jnp.asarray(twr, F32), twi=jnp.asarray(twi, F32), m2=jnp.asarray(m2, BF16))


def _four1_kernel(fa_ref, chan_ref, m1_ref, twr_ref, twi_ref, br_ref, bi_ref):
    n1, sb, _ = fa_ref.shape
    rows = n1 * SUBLANES
    for i in range(sb // SUBLANES):
        rs = slice(i * SUBLANES, (i + 1) * SUBLANES)
        x = fa_ref[:, rs, :].reshape(rows, GROUP_W).astype(BF16)
        zr, zi = [], []
        for g in range(N_FOURIER_GROUPS):
            z = jnp.dot(x[:, g * FOURIER_W:(g + 1) * FOURIER_W], chan_ref[...],
                        preferred_element_type=F32)
            zr.append(z[:, :FOURIER_W])
            zi.append(z[:, FOURIER_W:])
        zcat = jnp.concatenate([jnp.concatenate(zr, axis=1), jnp.concatenate(zi, axis=1)],
                               axis=0).astype(BF16)
        a = jnp.dot(m1_ref[...], zcat, preferred_element_type=F32)
        ar, ai = a[:rows], a[rows:]
        twr, twi = twr_ref[i], twi_ref[i]
        br_ref[:, rs, :] = (ar * twr - ai * twi).reshape(n1, SUBLANES, GROUP_W)
        bi_ref[:, rs, :] = (ar * twi + ai * twr).reshape(n1, SUBLANES, GROUP_W)


def _four2_kernel(br_ref, bi_ref, m2_ref, wf_ref, fg_ref, o_ref):
    _, n2, _ = br_ref.shape
    rows = SUBLANES * n2
    bcat = jnp.concatenate([br_ref[...].reshape(rows, GROUP_W), bi_ref[...].reshape(rows, GROUP_W)],
                           axis=0).astype(BF16)
    f = jnp.dot(m2_ref[...], bcat, preferred_element_type=F32).astype(BF16)
    gate = _silu(fg_ref[...].reshape(rows, GROUP_W))
    ys = [jnp.dot(f[:, g * FOURIER_W:(g + 1) * FOURIER_W], wf_ref[g], preferred_element_type=F32)
          for g in range(N_FOURIER_GROUPS)]
    y = jnp.concatenate(ys, axis=1) * gate
    o_ref[...] = y.reshape(n2, SUBLANES, GROUP_W)


def _fourier(fa, fg, wf, ft):
    B, S, _ = fa.shape
    n2 = DFT_N2
    n1 = S // n2
    sb = 32 if n1 <= 16 else SUBLANES
    nsub = sb // SUBLANES
    fa4 = fa.reshape(B, n1, n2, GROUP_W)
    blk1 = pl.BlockSpec((None, n1, sb, GROUP_W), lambda b, j: (b, 0, j, 0))
    tw = pl.BlockSpec((nsub, n1 * SUBLANES, 1), lambda b, j: (j, 0, 0))
    br, bi = pl.pallas_call(
        _four1_kernel,
        grid=(B, n2 // sb),
        in_specs=[blk1, _const_spec((FOURIER_W, 2 * FOURIER_W)),
                  _const_spec((2 * n1 * SUBLANES, 2 * n1 * SUBLANES)), tw, tw],
        out_specs=[blk1, blk1],
        out_shape=[jax.ShapeDtypeStruct((B, n1, n2, GROUP_W), F32)] * 2,
        compiler_params=_cparams("parallel", "parallel"),
        name="four1",
    )(fa4, ft["chan"], ft["m1"], ft["twr"], ft["twi"])
    blk_in = pl.BlockSpec((None, SUBLANES, n2, GROUP_W), lambda b, j: (b, j, 0, 0))
    blk_out = pl.BlockSpec((None, n2, SUBLANES, GROUP_W), lambda b, j: (b, 0, j, 0))
    o = pl.pallas_call(
        _four2_kernel,
        grid=(B, n1 // SUBLANES),
        in_specs=[blk_in, blk_in, _const_spec((SUBLANES * n2, 2 * SUBLANES * n2)),
                  _const_spec((N_FOURIER_GROUPS, FOURIER_W, FOURIER_W)), blk_out],
        out_specs=blk_out,
        out_shape=jax.ShapeDtypeStruct((B, n2, n1, GROUP_W), F32),
        compiler_params=_cparams("parallel", "parallel"),
        name="four2",
    )(br, bi, ft["m2"], wf, fg.reshape(B, n2, n1, GROUP_W))
    return o.reshape(B, S, GROUP_W)


def _outproj_kernel(oa_ref, of_ref, os_ref, om_ref, x_ref, w_ref, g_ref, y_ref):
    o = jnp.concatenate([oa_ref[...], of_ref[...].astype(BF16), os_ref[...], om_ref[...]], axis=1)
    y = jnp.dot(o, w_ref[...], preferred_element_type=F32)
    y_ref[...] = x_ref[...] + _rms(y, g_ref[...])


def _outproj(oa, of, osg, om, x, w, g):
    B, S, _ = x.shape
    tm = TOKEN_TILE
    tok = lambda wd: pl.BlockSpec((None, tm, wd), lambda b, i: (b, i, 0))
    return pl.pallas_call(
        _outproj_kernel,
        grid=(B, S // tm),
        in_specs=[tok(GROUP_W), tok(GROUP_W), tok(GROUP_W), tok(GROUP_W), tok(D_MODEL),
                  _const_spec((4 * GROUP_W, D_MODEL)), _const_spec((1, D_MODEL))],
        out_specs=tok(D_MODEL),
        out_shape=jax.ShapeDtypeStruct((B, S, D_MODEL), F32),
        compiler_params=_cparams("parallel", "parallel"),
        name="outproj",
    )(oa, of, osg, om, x, w, g)


def _rope_tables(S):
    rows = S // GRID_W
    row = jnp.broadcast_to(jnp.arange(rows, dtype=F32)[:, None], (rows, GRID_W)).reshape(S)
    col = jnp.broadcast_to(jnp.arange(GRID_W, dtype=F32)[None, :], (rows, GRID_W)).reshape(S)
    inv = ROPE_THETA ** (-jnp.arange(ROPE_PAIRS, dtype=F32) / ROPE_PAIRS)
    ang = jnp.stack([row[:, None] * inv, col[:, None] * inv], axis=1)
    cos, sin = jnp.cos(ang), jnp.sin(ang)
    zero = jnp.zeros((S, ROPE_PAIRS), F32)
    two = lambda parts: jnp.tile(jnp.concatenate(parts, axis=-1), (1, LANES // HEAD_DIM))
    return dict(cos=two([cos[:, 0], cos[:, 0], cos[:, 1], cos[:, 1]]),
                sina=two([zero, sin[:, 0], zero, sin[:, 1]]),
                sinb=two([-sin[:, 0], zero, -sin[:, 1], zero]))


def _layer_weights(l, pre_norm_g, w_in, q_norm_g, k_norm_g, w_fourier, sgu_norm_g, w_spatial,
                   b_spatial, mem_norm_g, w_mem_kv, w_out, post_norm_g):
    return dict(
        pre_g=pre_norm_g[l][None, :],
        w_in=w_in[l].astype(BF16),
        qg=jnp.tile(q_norm_g[l], N_HEADS)[None, :],
        kg=jnp.tile(k_norm_g[l], N_KV_HEADS)[None, :],
        wf=w_fourier[l].astype(BF16),
        vg=sgu_norm_g[l].reshape(1, GROUP_W),
        ws=w_spatial[l].astype(BF16),
        bs=jnp.repeat(b_spatial[l].T, SGU_W, axis=1),
        mem_g=mem_norm_g[l][None, :],
        w_mem_kv=w_mem_kv[l].astype(BF16),
        w_out=w_out[l].astype(BF16),
        post_g=post_norm_g[l][None, :],
    )


def _trunk(x, mem, layers):
    S = x.shape[1]
    tabs = _rope_tables(S)
    ids = np.arange(GROUP_W) // HEAD_DIM
    tabs["bd"] = jnp.asarray(ids[:, None] == ids[None, :], BF16)
    ft = _dft_tables(S)
    for lw in layers:
        kv = _memkv(mem, lw["mem_g"], lw["w_mem_kv"])
        q, k, vt, ag, fa, fg, osgu, omem = _inproj(x, kv, lw, tabs)
        oatt = _attention(q, k, vt, ag)
        ofour = _fourier(fa, fg, lw["wf"], ft)
        x = _outproj(oatt, ofour, osgu, omem, x, lw["w_out"], lw["post_g"])
    return x


def kernel(x_prompt, x_sample, mem_prompt, mem_sample, pre_norm_g, w_in, q_norm_g, k_norm_g,
           w_fourier, sgu_norm_g, w_spatial, b_spatial, mem_norm_g, w_mem_kv, w_out, post_norm_g):
    layers = [_layer_weights(l, pre_norm_g, w_in, q_norm_g, k_norm_g, w_fourier, sgu_norm_g,
                             w_spatial, b_spatial, mem_norm_g, w_mem_kv, w_out, post_norm_g)
              for l in range(DEPTH)]
    return (_trunk(x_prompt, mem_prompt, layers), _trunk(x_sample, mem_sample, layers))
```

```python
import functools

import numpy as np
import jax
import jax.numpy as jnp
from jax import lax
from jax.experimental import pallas as pl
from jax.experimental.pallas import tpu as pltpu

F32 = jnp.float32
BF16 = jnp.bfloat16

D_MODEL = 1024
DEPTH = 2
GRID_W = 64
N_MEM = 256
GROUP_W = 512
HEAD_DIM = 64
N_HEADS = 8
N_KV_HEADS = 2
Q_PER_KV = N_HEADS // N_KV_HEADS
ROPE_PAIRS = HEAD_DIM // 4
ROPE_THETA = 10000.0
N_FOURIER_GROUPS = 4
FOURIER_W = 128
N_SGU_HEADS = 4
SGU_W = 128
CHUNK = 128
N_MEM_HEADS = 4
MEM_HEAD_DIM = 128
EPS = 1e-6
IN_W = 4864
C_AQ, C_AK, C_AV, C_AG, C_FA, C_FG, C_SU, C_SV, C_SG, C_MQ, C_MG = (
    0, 512, 640, 768, 1280, 1792, 2304, 2816, 3328, 3840, 4352)

LANES = 128
SUBLANES = 8
VMEM_LIMIT_BYTES = 56 * 1024 * 1024

TOKEN_TILE = 512
KV_CHUNK = 512
Q_TILE = 128
Q_ROWS = N_HEADS * Q_TILE
DFT_N2 = 128
ATTN_UNROLL = 6
ATTN_STATIC_STEPS = 8
BF16_SUBLANES = 16
VT_ROWS = HEAD_DIM + BF16_SUBLANES
Q_COLS_PER_KV = Q_PER_KV * Q_TILE
Q_SCALE = float(HEAD_DIM ** -0.5 * np.log2(np.e))


def _cparams(*sem):
    return pltpu.CompilerParams(dimension_semantics=sem, vmem_limit_bytes=VMEM_LIMIT_BYTES)


def _const_spec(shape):
    nd = len(shape)
    return pl.BlockSpec(shape, lambda *_: (0,) * nd)


def _silu(g):
    return g / (1.0 + jnp.exp(-g))


def _rms(x, g):
    ms = jnp.mean(x * x, axis=-1, keepdims=True)
    return x * lax.rsqrt(ms + EPS) * g


def _memkv_kernel(mem_ref, g_ref, w_ref, kv_ref):
    h = _rms(mem_ref[...], g_ref[...]).astype(BF16)
    kv_ref[...] = jnp.dot(h, w_ref[...], preferred_element_type=F32).astype(BF16)


def _memkv(mem, g, w):
    B = mem.shape[0]
    return pl.pallas_call(
        _memkv_kernel,
        grid=(B,),
        in_specs=[pl.BlockSpec((None, N_MEM, D_MODEL), lambda b: (b, 0, 0)),
                  _const_spec((1, D_MODEL)),
                  _const_spec((D_MODEL, 2 * GROUP_W))],
        out_specs=pl.BlockSpec((None, N_MEM, 2 * GROUP_W), lambda b: (b, 0, 0)),
        out_shape=jax.ShapeDtypeStruct((B, N_MEM, 2 * GROUP_W), BF16),
        compiler_params=_cparams("parallel"),
        name="memkv",
    )(mem, g, w)


def _head_ssq(x, bd):
    return jnp.dot((x * x).astype(BF16), bd, preferred_element_type=F32)


def _rope(x, cos, sina, sinb):
    return x * cos + pltpu.roll(x, 16, 1) * sina + pltpu.roll(x, LANES - 16, 1) * sinb


def _inproj_kernel(x_ref, pre_g_ref, w_ref, bd_ref, qg_ref, kg_ref, cos_ref, sina_ref, sinb_ref,
                   vg_ref, ws_ref, bs_ref, kv_ref,
                   qt_ref, k_ref, vt_ref, ag_ref, fa_ref, fg_ref, osgu_ref, omem_ref):
    tm = x_ref.shape[0]
    h = _rms(x_ref[...], pre_g_ref[...]).astype(BF16)

    def proj(lo, width):
        return jnp.dot(h, w_ref[:, lo:lo + width], preferred_element_type=F32)

    cos, sina, sinb = cos_ref[...], sina_ref[...], sinb_ref[...]

    aq = proj(C_AQ, GROUP_W)
    ssq = _head_ssq(aq, bd_ref[...])
    qn = aq * lax.rsqrt(ssq * (1.0 / HEAD_DIM) + EPS) * qg_ref[...]
    slabs = [_rope(qn[:, j * LANES:(j + 1) * LANES], cos, sina, sinb) * Q_SCALE
             for j in range(GROUP_W // LANES)]
    low = lax.broadcasted_iota(jnp.int32, (Q_TILE, LANES), 1) < HEAD_DIM
    for t in range(tm // Q_TILE):
        parts = []
        for j, slab in enumerate(slabs):
            sj = slab[t * Q_TILE:(t + 1) * Q_TILE, :]
            rolled = pltpu.roll(sj, HEAD_DIM, 1)
            if j < Q_PER_KV // 2:
                parts += [jnp.where(low, sj, 0.0), jnp.where(low, rolled, 0.0)]
            else:
                parts += [jnp.where(low, 0.0, rolled), jnp.where(low, 0.0, sj)]
        qt_ref[t] = jnp.concatenate(parts, axis=0).T.astype(BF16)

    ak = proj(C_AK, LANES)
    ssk = _head_ssq(ak, bd_ref[0:LANES, 0:LANES])
    kn = ak * lax.rsqrt(ssk * (1.0 / HEAD_DIM) + EPS) * kg_ref[...]
    k_ref[...] = _rope(kn, cos, sina, sinb).astype(BF16)
    av = proj(C_AV, LANES)
    avt = av.T.astype(BF16)
    for g in range(N_KV_HEADS):
        vt_ref[g * VT_ROWS:g * VT_ROWS + HEAD_DIM, :] = avt[g * HEAD_DIM:(g + 1) * HEAD_DIM, :]
        vt_ref[g * VT_ROWS + HEAD_DIM:(g + 1) * VT_ROWS, :] = jnp.ones((BF16_SUBLANES, tm), BF16)

    ag_ref[...] = proj(C_AG, GROUP_W).astype(BF16)
    fa_ref[...] = proj(C_FA, GROUP_W)
    fg_ref[...] = proj(C_FG, GROUP_W)

    su = proj(C_SU, GROUP_W)
    sv = proj(C_SV, GROUP_W)
    sg = proj(C_SG, GROUP_W)
    gate = _silu(sg)
    nck = tm // CHUNK
    for hd in range(N_SGU_HEADS):
        sl = slice(hd * SGU_W, (hd + 1) * SGU_W)
        vh = _rms(sv[:, sl], vg_ref[:, sl]).astype(BF16)
        vcat = jnp.concatenate([vh[c * CHUNK:(c + 1) * CHUNK, :] for c in range(nck)], axis=1)
        spc = jnp.dot(ws_ref[hd], vcat, preferred_element_type=F32)
        for c in range(nck):
            rows = slice(c * CHUNK, (c + 1) * CHUNK)
            sp = spc[:, c * SGU_W:(c + 1) * SGU_W] + bs_ref[:, sl]
            osgu_ref[rows, sl] = (su[rows, sl] * sp * gate[rows, sl]).astype(BF16)

    mq = proj(C_MQ, GROUP_W)
    mg = proj(C_MG, GROUP_W)
    mgate = _silu(mg)
    for hd in range(N_MEM_HEADS):
        sl = slice(hd * MEM_HEAD_DIM, (hd + 1) * MEM_HEAD_DIM)
        mk = kv_ref[:, hd * MEM_HEAD_DIM:(hd + 1) * MEM_HEAD_DIM]
        mv = kv_ref[:, GROUP_W + hd * MEM_HEAD_DIM:GROUP_W + (hd + 1) * MEM_HEAD_DIM]
        s = lax.dot_general(mq[:, sl].astype(BF16), mk, (((1,), (1,)), ((), ())),
                            preferred_element_type=F32) * (MEM_HEAD_DIM ** -0.5)
        e = jnp.exp(s - jnp.max(s, axis=-1, keepdims=True))
        o = jnp.dot(e.astype(BF16), mv, preferred_element_type=F32)
        o = o / jnp.sum(e, axis=-1, keepdims=True)
        omem_ref[:, sl] = (o * mgate[:, sl]).astype(BF16)


def _inproj(x, kv, lw, tabs):
    B, S, _ = x.shape
    tm = TOKEN_TILE
    nt = S // tm
    tok = lambda w: pl.BlockSpec((None, tm, w), lambda b, i: (b, i, 0))
    pos = pl.BlockSpec((tm, LANES), lambda b, i: (i, 0))
    in_specs = [
        tok(D_MODEL),
        _const_spec((1, D_MODEL)),
        _const_spec((D_MODEL, IN_W)),
        _const_spec((GROUP_W, GROUP_W)),
        _const_spec((1, GROUP_W)),
        _const_spec((1, LANES)),
        pos, pos, pos,
        _const_spec((1, GROUP_W)),
        _const_spec((N_SGU_HEADS, CHUNK, CHUNK)),
        _const_spec((CHUNK, GROUP_W)),
        pl.BlockSpec((None, N_MEM, 2 * GROUP_W), lambda b, i: (b, 0, 0)),
    ]
    out_specs = [
        pl.BlockSpec((None, tm // Q_TILE, LANES, Q_ROWS), lambda b, i: (b, i, 0, 0)), tok(LANES),
        pl.BlockSpec((None, None, N_KV_HEADS * VT_ROWS, tm), lambda b, i: (b, i, 0, 0)),
        tok(GROUP_W), tok(GROUP_W), tok(GROUP_W), tok(GROUP_W), tok(GROUP_W),
    ]
    sds = jax.ShapeDtypeStruct
    out_shape = [
        sds((B, S // Q_TILE, LANES, Q_ROWS), BF16), sds((B, S, LANES), BF16),
        sds((B, nt, N_KV_HEADS * VT_ROWS, tm), BF16),
        sds((B, S, GROUP_W), BF16), sds((B, S, GROUP_W), F32), sds((B, S, GROUP_W), F32),
        sds((B, S, GROUP_W), BF16), sds((B, S, GROUP_W), BF16),
    ]
    return pl.pallas_call(
        _inproj_kernel,
        grid=(B, nt),
        in_specs=in_specs, out_specs=out_specs, out_shape=out_shape,
        compiler_params=_cparams("parallel", "parallel"),
        name="inproj",
    )(x, lw["pre_g"], lw["w_in"], tabs["bd"], lw["qg"], lw["kg"],
      tabs["cos"], tabs["sina"], tabs["sinb"], lw["vg"], lw["ws"], lw["bs"], kv)


def _attn_kernel(qt_ref, k_ref, vt_ref, ot_ref, st_ref, acc_ref):
    ntiles = qt_ref.shape[0]
    nchunks, _, tk = vt_ref.shape
    neg_inf = jnp.full((1, qt_ref.shape[2]), -jnp.inf, F32)

    def scores(t, c, slot):
        start = c * tk if isinstance(c, int) else pl.multiple_of(c * tk, tk)
        st_ref[slot] = jnp.dot(k_ref[pl.ds(start, tk), :], qt_ref[t], preferred_element_type=F32)

    def softmax_pv(t, c, slot, m_old, first=False):
        st = st_ref[slot]
        m_new = jnp.max(st, axis=0, keepdims=True)
        if not first:
            m_new = jnp.maximum(m_old, m_new)
        p = jnp.exp2(st - m_new).astype(BF16)
        alpha = None if first else jnp.exp2(m_old - m_new)
        for g in range(N_KV_HEADS):
            cols = slice(g * Q_COLS_PER_KV, (g + 1) * Q_COLS_PER_KV)
            pv = jnp.dot(vt_ref[c, g * VT_ROWS:(g + 1) * VT_ROWS, :], p[:, cols],
                         preferred_element_type=F32)
            acc_ref[t, g] = pv if first else alpha[:, cols] * acc_ref[t, g] + pv
        return m_new

    def finish(t):
        for g in range(N_KV_HEADS):
            acc = acc_ref[t, g]
            ot_ref[t, :, g * Q_COLS_PER_KV:(g + 1) * Q_COLS_PER_KV] = (
                acc[:HEAD_DIM] / acc[HEAD_DIM:HEAD_DIM + 1]).astype(BF16)

    if ntiles * nchunks <= ATTN_STATIC_STEPS:
        steps = [(t, c) for t in range(ntiles) for c in range(nchunks)]
        scores(0, 0, 0)
        m = neg_inf
        for i, (t, c) in enumerate(steps):
            if i + 1 < len(steps):
                scores(*steps[i + 1], (i + 1) % 2)
            m = softmax_pv(t, c, i % 2, m, first=(c == 0))
            if c == nchunks - 1:
                finish(t)
    else:
        assert ntiles == 1
        steady = nchunks - 1
        groups = (steady - 1) // ATTN_UNROLL
        peeled = steady - groups * ATTN_UNROLL

        def step(c, slot, m, first=False):
            scores(0, c + 1, 1 - slot)
            return softmax_pv(0, c, slot, m, first)

        scores(0, 0, 0)
        m = neg_inf
        for c in range(peeled):
            m = step(c, c % 2, m, first=(c == 0))

        def group(g, m):
            for u in range(ATTN_UNROLL):
                m = step(peeled + g * ATTN_UNROLL + u, (peeled + u) % 2, m)
            return m
        m = lax.fori_loop(0, groups, group, m)
        softmax_pv(0, nchunks - 1, (nchunks - 1) % 2, m)
        finish(0)


def _attention(qt, k, vt):
    B, nq = qt.shape[0], qt.shape[1]
    S = k.shape[1]
    tk = vt.shape[3]
    ntiles = max(1, ATTN_STATIC_STEPS // (S // tk))
    qspec = pl.BlockSpec((None, ntiles, LANES, Q_ROWS), lambda b, i: (b, i, 0, 0))
    return pl.pallas_call(
        _attn_kernel,
        grid=(B, nq // ntiles),
        in_specs=[qspec,
                  pl.BlockSpec((None, S, LANES), lambda b, i: (b, 0, 0)),
                  pl.BlockSpec((None,) + vt.shape[1:], lambda b, i: (b, 0, 0, 0))],
        out_specs=pl.BlockSpec((None, ntiles, HEAD_DIM, Q_ROWS), lambda b, i: (b, i, 0, 0)),
        out_shape=jax.ShapeDtypeStruct((B, nq, HEAD_DIM, Q_ROWS), BF16),
        scratch_shapes=[pltpu.VMEM((2, tk, Q_ROWS), F32),
                        pltpu.VMEM((ntiles, N_KV_HEADS, VT_ROWS, Q_COLS_PER_KV), F32)],
        compiler_params=_cparams("parallel", "parallel"),
        name="attn",
    )(qt, k, vt)


def _dft_tables(S):
    n2 = DFT_N2
    n1 = S // n2
    c = np.arange(FOURIER_W)
    ang = 2.0 * np.pi * np.outer(c, c) / FOURIER_W
    chan = np.concatenate([np.cos(ang), -np.sin(ang)], axis=1) / np.sqrt(FOURIER_W)
    a1 = 2.0 * np.pi * np.outer(np.arange(n1), np.arange(n1)) / n1
    eye = np.eye(SUBLANES)
    mr = np.kron(np.cos(a1), eye) / np.sqrt(n1)
    mi = np.kron(-np.sin(a1), eye) / np.sqrt(n1)
    m1 = np.block([[mr, -mi], [mi, mr]])
    k1 = np.arange(n1)[None, :, None]
    s2 = (np.arange(n2 // SUBLANES)[:, None, None] * SUBLANES + np.arange(SUBLANES)[None, None, :])
    at = 2.0 * np.pi * (k1 * s2) / S
    twr = np.cos(at).reshape(n2 // SUBLANES, n1 * SUBLANES, 1)
    twi = (-np.sin(at)).reshape(n2 // SUBLANES, n1 * SUBLANES, 1)
    a2 = 2.0 * np.pi * np.outer(np.arange(n2), np.arange(n2)) / n2
    c2 = (np.cos(a2) / np.sqrt(n2))[:, None, None, :] * eye[None, :, :, None]
    s2m = (np.sin(a2) / np.sqrt(n2))[:, None, None, :] * eye[None, :, :, None]
    m2 = np.concatenate([c2.reshape(n2 * SUBLANES, n2 * SUBLANES),
                         s2m.reshape(n2 * SUBLANES, n2 * SUBLANES)], axis=1)
    return dict(chan=jnp.asarray(chan, BF16), m1=jnp.asarray(m1, BF16),
                twr=jnp.asarray(twr, F32), twi=jnp.asarray(twi, F32), m2=jnp.asarray(m2, BF16))


def _four1_kernel(fa_ref, chan_ref, m1_ref, twr_ref, twi_ref, br_ref, bi_ref):
    n1, sb, _ = fa_ref.shape
    rows = n1 * SUBLANES
    for i in range(sb // SUBLANES):
        rs = slice(i * SUBLANES, (i + 1) * SUBLANES)
        x = fa_ref[:, rs, :].reshape(rows, GROUP_W).astype(BF16)
        zr, zi = [], []
        for g in range(N_FOURIER_GROUPS):
            z = jnp.dot(x[:, g * FOURIER_W:(g + 1) * FOURIER_W], chan_ref[...],
                        preferred_element_type=F32)
            zr.append(z[:, :FOURIER_W])
            zi.append(z[:, FOURIER_W:])
        zcat = jnp.concatenate([jnp.concatenate(zr, axis=1), jnp.concatenate(zi, axis=1)],
                               axis=0).astype(BF16)
        a = jnp.dot(m1_ref[...], zcat, preferred_element_type=F32)
        ar, ai = a[:rows], a[rows:]
        twr, twi = twr_ref[i], twi_ref[i]
        br_ref[:, rs, :] = (ar * twr - ai * twi).reshape(n1, SUBLANES, GROUP_W)
        bi_ref[:, rs, :] = (ar * twi + ai * twr).reshape(n1, SUBLANES, GROUP_W)


def _four2_kernel(br_ref, bi_ref, m2_ref, wf_ref, fg_ref, o_ref):
    _, n2, _ = br_ref.shape
    rows = SUBLANES * n2
    bcat = jnp.concatenate([br_ref[...].reshape(rows, GROUP_W), bi_ref[...].reshape(rows, GROUP_W)],
                           axis=0).astype(BF16)
    f = jnp.dot(m2_ref[...], bcat, preferred_element_type=F32).astype(BF16)
    gate = _silu(fg_ref[...].reshape(rows, GROUP_W))
    ys = [jnp.dot(f[:, g * FOURIER_W:(g + 1) * FOURIER_W], wf_ref[g], preferred_element_type=F32)
          for g in range(N_FOURIER_GROUPS)]
    y = jnp.concatenate(ys, axis=1) * gate
    o_ref[...] = y.reshape(n2, SUBLANES, GROUP_W)


def _fourier(fa, fg, wf, ft):
    B, S, _ = fa.shape
    n2 = DFT_N2
    n1 = S // n2
    sb = 32 if n1 <= 16 else SUBLANES
    nsub = sb // SUBLANES
    fa4 = fa.reshape(B, n1, n2, GROUP_W)
    blk1 = pl.BlockSpec((None, n1, sb, GROUP_W), lambda b, j: (b, 0, j, 0))
    tw = pl.BlockSpec((nsub, n1 * SUBLANES, 1), lambda b, j: (j, 0, 0))
    br, bi = pl.pallas_call(
        _four1_kernel,
        grid=(B, n2 // sb),
        in_specs=[blk1, _const_spec((FOURIER_W, 2 * FOURIER_W)),
                  _const_spec((2 * n1 * SUBLANES, 2 * n1 * SUBLANES)), tw, tw],
        out_specs=[blk1, blk1],
        out_shape=[jax.ShapeDtypeStruct((B, n1, n2, GROUP_W), F32)] * 2,
        compiler_params=_cparams("parallel", "parallel"),
        name="four1",
    )(fa4, ft["chan"], ft["m1"], ft["twr"], ft["twi"])
    blk_in = pl.BlockSpec((None, SUBLANES, n2, GROUP_W), lambda b, j: (b, j, 0, 0))
    blk_out = pl.BlockSpec((None, n2, SUBLANES, GROUP_W), lambda b, j: (b, 0, j, 0))
    o = pl.pallas_call(
        _four2_kernel,
        grid=(B, n1 // SUBLANES),
        in_specs=[blk_in, blk_in, _const_spec((SUBLANES * n2, 2 * SUBLANES * n2)),
                  _const_spec((N_FOURIER_GROUPS, FOURIER_W, FOURIER_W)), blk_out],
        out_specs=blk_out,
        out_shape=jax.ShapeDtypeStruct((B, n2, n1, GROUP_W), F32),
        compiler_params=_cparams("parallel", "parallel"),
        name="four2",
    )(br, bi, ft["m2"], wf, fg.reshape(B, n2, n1, GROUP_W))
    return o.reshape(B, S, GROUP_W)


def _outproj_kernel(ot_ref, ag_ref, of_ref, os_ref, om_ref, x_ref, w_ref, g_ref, y_ref):
    gate = _silu(ag_ref[...].astype(F32))
    tiles = []
    for t in range(ot_ref.shape[0]):
        ot = ot_ref[t].astype(F32)
        slabs = []
        for j in range(GROUP_W // LANES):
            pair = jnp.concatenate([ot[:, (2 * j) * Q_TILE:(2 * j + 1) * Q_TILE],
                                    ot[:, (2 * j + 1) * Q_TILE:(2 * j + 2) * Q_TILE]], axis=0)
            slabs.append(pair.T)
        tiles.append(jnp.concatenate(slabs, axis=1))
    oa = (jnp.concatenate(tiles, axis=0) * gate).astype(BF16)
    o = jnp.concatenate([oa, of_ref[...].astype(BF16), os_ref[...], om_ref[...]], axis=1)
    y = jnp.dot(o, w_ref[...], preferred_element_type=F32)
    y_ref[...] = x_ref[...] + _rms(y, g_ref[...])


def _outproj(ot, ag, of, osg, om, x, w, g):
    B, S, _ = x.shape
    tm = TOKEN_TILE
    tok = lambda wd: pl.BlockSpec((None, tm, wd), lambda b, i: (b, i, 0))
    return pl.pallas_call(
        _outproj_kernel,
        grid=(B, S // tm),
        in_specs=[pl.BlockSpec((None, tm // Q_TILE, HEAD_DIM, Q_ROWS), lambda b, i: (b, i, 0, 0)),
                  tok(GROUP_W), tok(GROUP_W), tok(GROUP_W), tok(GROUP_W), tok(D_MODEL),
                  _const_spec((4 * GROUP_W, D_MODEL)), _const_spec((1, D_MODEL))],
        out_specs=tok(D_MODEL),
        out_shape=jax.ShapeDtypeStruct((B, S, D_MODEL), F32),
        compiler_params=_cparams("parallel", "parallel"),
        name="outproj",
    )(ot, ag, of, osg, om, x, w, g)


def _rope_tables(S):
    rows = S // GRID_W
    row = jnp.broadcast_to(jnp.arange(rows, dtype=F32)[:, None], (rows, GRID_W)).reshape(S)
    col = jnp.broadcast_to(jnp.arange(GRID_W, dtype=F32)[None, :], (rows, GRID_W)).reshape(S)
    inv = ROPE_THETA ** (-jnp.arange(ROPE_PAIRS, dtype=F32) / ROPE_PAIRS)
    ang = jnp.stack([row[:, None] * inv, col[:, None] * inv], axis=1)
    cos, sin = jnp.cos(ang), jnp.sin(ang)
    zero = jnp.zeros((S, ROPE_PAIRS), F32)
    two = lambda parts: jnp.tile(jnp.concatenate(parts, axis=-1), (1, LANES // HEAD_DIM))
    return dict(cos=two([cos[:, 0], cos[:, 0], cos[:, 1], cos[:, 1]]),
                sina=two([zero, sin[:, 0], zero, sin[:, 1]]),
                sinb=two([-sin[:, 0], zero, -sin[:, 1], zero]))


def _layer_weights(l, pre_norm_g, w_in, q_norm_g, k_norm_g, w_fourier, sgu_norm_g, w_spatial,
                   b_spatial, mem_norm_g, w_mem_kv, w_out, post_norm_g):
    return dict(
        pre_g=pre_norm_g[l][None, :],
        w_in=w_in[l].astype(BF16),
        qg=jnp.tile(q_norm_g[l], N_HEADS)[None, :],
        kg=jnp.tile(k_norm_g[l], N_KV_HEADS)[None, :],
        wf=w_fourier[l].astype(BF16),
        vg=sgu_norm_g[l].reshape(1, GROUP_W),
        ws=w_spatial[l].astype(BF16),
        bs=jnp.repeat(b_spatial[l].T, SGU_W, axis=1),
        mem_g=mem_norm_g[l][None, :],
        w_mem_kv=w_mem_kv[l].astype(BF16),
        w_out=w_out[l].astype(BF16),
        post_g=post_norm_g[l][None, :],
    )


def _trunk(x, mem, layers):
    S = x.shape[1]
    tabs = _rope_tables(S)
    ids = np.arange(GROUP_W) // HEAD_DIM
    tabs["bd"] = jnp.asarray(ids[:, None] == ids[None, :], BF16)
    ft = _dft_tables(S)
    for lw in layers:
        kv = _memkv(mem, lw["mem_g"], lw["w_mem_kv"])
        qt, k, vt, ag, fa, fg, osgu, omem = _inproj(x, kv, lw, tabs)
        ot = _attention(qt, k, vt)
        ofour = _fourier(fa, fg, lw["wf"], ft)
        x = _outproj(ot, ag, ofour, osgu, omem, x, lw["w_out"], lw["post_g"])
    return x


def kernel(x_prompt, x_sample, mem_prompt, mem_sample, pre_norm_g, w_in, q_norm_g, k_norm_g,
           w_fourier, sgu_norm_g, w_spatial, b_spatial, mem_norm_g, w_mem_kv, w_out, post_norm_g):
    layers = [_layer_weights(l, pre_norm_g, w_in, q_norm_g, k_norm_g, w_fourier, sgu_norm_g,
                             w_spatial, b_spatial, mem_norm_g, w_mem_kv, w_out, post_norm_g)
              for l in range(DEPTH)]
    return (_trunk(x_prompt, mem_prompt, layers), _trunk(x_sample, mem_sample, layers))
```

```python
import functools

import numpy as np
import jax
import jax.numpy as jnp
from jax import lax
from jax.experimental import pallas as pl
from jax.experimental.pallas import tpu as pltpu

F32 = jnp.float32
BF16 = jnp.bfloat16

D_MODEL = 1024
DEPTH = 2
GRID_W = 64
N_MEM = 256
GROUP_W = 512
HEAD_DIM = 64
N_HEADS = 8
N_KV_HEADS = 2
Q_PER_KV = N_HEADS // N_KV_HEADS
ROPE_PAIRS = HEAD_DIM // 4
ROPE_THETA = 10000.0
N_FOURIER_GROUPS = 4
FOURIER_W = 128
N_SGU_HEADS = 4
SGU_W = 128
CHUNK = 128
N_MEM_HEADS = 4
MEM_HEAD_DIM = 128
EPS = 1e-6
IN_W = 4864
C_AQ, C_AK, C_AV, C_AG, C_FA, C_FG, C_SU, C_SV, C_SG, C_MQ, C_MG = (
    0, 512, 640, 768, 1280, 1792, 2304, 2816, 3328, 3840, 4352)

LANES = 128
SUBLANES = 8
VMEM_LIMIT_BYTES = 56 * 1024 * 1024

TOKEN_TILE = 512
KV_CHUNK = 512
Q_TILE = 128
Q_ROWS = N_HEADS * Q_TILE
DFT_N2 = 128
ATTN_UNROLL = 6
OUT_SUB_TILES = 2
ATTN_TILES_PER_STEP = 8
BF16_SUBLANES = 16
VT_ROWS = HEAD_DIM + BF16_SUBLANES
Q_COLS_PER_KV = Q_PER_KV * Q_TILE
Q_SCALE = float(HEAD_DIM ** -0.5 * np.log2(np.e))


def _cparams(*sem):
    return pltpu.CompilerParams(dimension_semantics=sem, vmem_limit_bytes=VMEM_LIMIT_BYTES)


def _const_spec(shape):
    nd = len(shape)
    return pl.BlockSpec(shape, lambda *_: (0,) * nd)


def _silu(g):
    return g / (1.0 + jnp.exp(-g))


def _rms(x, g):
    ms = jnp.mean(x * x, axis=-1, keepdims=True)
    return x * lax.rsqrt(ms + EPS) * g


def _memkv_kernel(mem_ref, g_ref, w_ref, kv_ref):
    h = _rms(mem_ref[...], g_ref[...]).astype(BF16)
    kv_ref[...] = jnp.dot(h, w_ref[...], preferred_element_type=F32).astype(BF16)


def _memkv(mem, g, w):
    B = mem.shape[0]
    return pl.pallas_call(
        _memkv_kernel,
        grid=(B,),
        in_specs=[pl.BlockSpec((None, N_MEM, D_MODEL), lambda b: (b, 0, 0)),
                  _const_spec((1, D_MODEL)),
                  _const_spec((D_MODEL, 2 * GROUP_W))],
        out_specs=pl.BlockSpec((None, N_MEM, 2 * GROUP_W), lambda b: (b, 0, 0)),
        out_shape=jax.ShapeDtypeStruct((B, N_MEM, 2 * GROUP_W), BF16),
        compiler_params=_cparams("parallel"),
        name="memkv",
    )(mem, g, w)


def _head_ssq(x, bd):
    return jnp.dot((x * x).astype(BF16), bd, preferred_element_type=F32)


def _rope(x, cos, sina, sinb):
    return x * cos + pltpu.roll(x, 16, 1) * sina + pltpu.roll(x, LANES - 16, 1) * sinb


def _inproj_kernel(x_ref, pre_g_ref, w_ref, bd_ref, qg_ref, kg_ref, cos_ref, sina_ref, sinb_ref,
                   vg_ref, ws_ref, bs_ref, kv_ref,
                   qt_ref, k_ref, vt_ref, ag_ref, fa_ref, fg_ref, osgu_ref, omem_ref):
    tm = x_ref.shape[0]
    h = _rms(x_ref[...], pre_g_ref[...]).astype(BF16)

    def proj(lo, width):
        return jnp.dot(h, w_ref[:, lo:lo + width], preferred_element_type=F32)

    cos, sina, sinb = cos_ref[...], sina_ref[...], sinb_ref[...]

    aq = proj(C_AQ, GROUP_W)
    ssq = _head_ssq(aq, bd_ref[...])
    qn = aq * lax.rsqrt(ssq * (1.0 / HEAD_DIM) + EPS) * qg_ref[...]
    slabs = [_rope(qn[:, j * LANES:(j + 1) * LANES], cos, sina, sinb) * Q_SCALE
             for j in range(GROUP_W // LANES)]
    low = lax.broadcasted_iota(jnp.int32, (Q_TILE, LANES), 1) < HEAD_DIM
    for t in range(tm // Q_TILE):
        parts = []
        for j, slab in enumerate(slabs):
            sj = slab[t * Q_TILE:(t + 1) * Q_TILE, :]
            rolled = pltpu.roll(sj, HEAD_DIM, 1)
            if j < Q_PER_KV // 2:
                parts += [jnp.where(low, sj, 0.0), jnp.where(low, rolled, 0.0)]
            else:
                parts += [jnp.where(low, 0.0, rolled), jnp.where(low, 0.0, sj)]
        qt_ref[t] = jnp.concatenate(parts, axis=0).T.astype(BF16)

    ak = proj(C_AK, LANES)
    ssk = _head_ssq(ak, bd_ref[0:LANES, 0:LANES])
    kn = ak * lax.rsqrt(ssk * (1.0 / HEAD_DIM) + EPS) * kg_ref[...]
    k_ref[...] = _rope(kn, cos, sina, sinb).astype(BF16)
    av = proj(C_AV, LANES)
    avt = av.T.astype(BF16)
    for g in range(N_KV_HEADS):
        vt_ref[g * VT_ROWS:g * VT_ROWS + HEAD_DIM, :] = avt[g * HEAD_DIM:(g + 1) * HEAD_DIM, :]
        vt_ref[g * VT_ROWS + HEAD_DIM:(g + 1) * VT_ROWS, :] = jnp.ones((BF16_SUBLANES, tm), BF16)

    ag_ref[...] = proj(C_AG, GROUP_W).astype(BF16)
    fa_ref[...] = proj(C_FA, GROUP_W)
    fg_ref[...] = proj(C_FG, GROUP_W)

    su = proj(C_SU, GROUP_W)
    sv = proj(C_SV, GROUP_W)
    sg = proj(C_SG, GROUP_W)
    gate = _silu(sg)
    nck = tm // CHUNK
    for hd in range(N_SGU_HEADS):
        sl = slice(hd * SGU_W, (hd + 1) * SGU_W)
        vh = _rms(sv[:, sl], vg_ref[:, sl]).astype(BF16)
        vcat = jnp.concatenate([vh[c * CHUNK:(c + 1) * CHUNK, :] for c in range(nck)], axis=1)
        spc = jnp.dot(ws_ref[hd], vcat, preferred_element_type=F32)
        for c in range(nck):
            rows = slice(c * CHUNK, (c + 1) * CHUNK)
            sp = spc[:, c * SGU_W:(c + 1) * SGU_W] + bs_ref[:, sl]
            osgu_ref[rows, sl] = (su[rows, sl] * sp * gate[rows, sl]).astype(BF16)

    mq = proj(C_MQ, GROUP_W)
    mg = proj(C_MG, GROUP_W)
    mgate = _silu(mg)
    for hd in range(N_MEM_HEADS):
        sl = slice(hd * MEM_HEAD_DIM, (hd + 1) * MEM_HEAD_DIM)
        mk = kv_ref[:, hd * MEM_HEAD_DIM:(hd + 1) * MEM_HEAD_DIM]
        mv = kv_ref[:, GROUP_W + hd * MEM_HEAD_DIM:GROUP_W + (hd + 1) * MEM_HEAD_DIM]
        s = lax.dot_general(mq[:, sl].astype(BF16), mk, (((1,), (1,)), ((), ())),
                            preferred_element_type=F32) * (MEM_HEAD_DIM ** -0.5)
        e = jnp.exp(s - jnp.max(s, axis=-1, keepdims=True))
        o = jnp.dot(e.astype(BF16), mv, preferred_element_type=F32)
        o = o / jnp.sum(e, axis=-1, keepdims=True)
        omem_ref[:, sl] = (o * mgate[:, sl]).astype(BF16)


def _inproj(x, kv, lw, tabs):
    B, S, _ = x.shape
    tm = TOKEN_TILE
    nt = S // tm
    tok = lambda w: pl.BlockSpec((None, tm, w), lambda b, i: (b, i, 0))
    pos = pl.BlockSpec((tm, LANES), lambda b, i: (i, 0))
    in_specs = [
        tok(D_MODEL),
        _const_spec((1, D_MODEL)),
        _const_spec((D_MODEL, IN_W)),
        _const_spec((GROUP_W, GROUP_W)),
        _const_spec((1, GROUP_W)),
        _const_spec((1, LANES)),
        pos, pos, pos,
        _const_spec((1, GROUP_W)),
        _const_spec((N_SGU_HEADS, CHUNK, CHUNK)),
        _const_spec((CHUNK, GROUP_W)),
        pl.BlockSpec((None, N_MEM, 2 * GROUP_W), lambda b, i: (b, 0, 0)),
    ]
    out_specs = [
        pl.BlockSpec((None, tm // Q_TILE, LANES, Q_ROWS), lambda b, i: (b, i, 0, 0)), tok(LANES),
        pl.BlockSpec((None, None, N_KV_HEADS * VT_ROWS, tm), lambda b, i: (b, i, 0, 0)),
        tok(GROUP_W), tok(GROUP_W), tok(GROUP_W), tok(GROUP_W), tok(GROUP_W),
    ]
    sds = jax.ShapeDtypeStruct
    out_shape = [
        sds((B, S // Q_TILE, LANES, Q_ROWS), BF16), sds((B, S, LANES), BF16),
        sds((B, nt, N_KV_HEADS * VT_ROWS, tm), BF16),
        sds((B, S, GROUP_W), BF16), sds((B, S, GROUP_W), F32), sds((B, S, GROUP_W), F32),
        sds((B, S, GROUP_W), BF16), sds((B, S, GROUP_W), BF16),
    ]
    return pl.pallas_call(
        _inproj_kernel,
        grid=(B, nt),
        in_specs=in_specs, out_specs=out_specs, out_shape=out_shape,
        compiler_params=_cparams("parallel", "parallel"),
        name="inproj",
    )(x, lw["pre_g"], lw["w_in"], tabs["bd"], lw["qg"], lw["kg"],
      tabs["cos"], tabs["sina"], tabs["sinb"], lw["vg"], lw["ws"], lw["bs"], kv)


def _attn_kernel(qt_ref, k_ref, vt_ref, ot_ref, st_ref, mx_ref, acc_ref):
    ntiles = qt_ref.shape[0]
    nchunks, _, tk = vt_ref.shape
    neg_inf = jnp.full((1, qt_ref.shape[2]), -jnp.inf, F32)

    def scores(t, c, slot):
        start = c * tk if isinstance(c, int) else pl.multiple_of(c * tk, tk)
        st = jnp.dot(k_ref[pl.ds(start, tk), :], qt_ref[t], preferred_element_type=F32)
        st_ref[slot] = st
        mx_ref[slot] = jnp.max(st, axis=0, keepdims=True)

    def softmax_pv(t, c, slot, m_old, first=False):
        st = st_ref[slot]
        m_new = mx_ref[slot]
        if not first:
            m_new = jnp.maximum(m_old, m_new)
        p = jnp.exp2(st - m_new).astype(BF16)
        alpha = None if first else jnp.exp2(m_old - m_new)
        for g in range(N_KV_HEADS):
            cols = slice(g * Q_COLS_PER_KV, (g + 1) * Q_COLS_PER_KV)
            pv = jnp.dot(vt_ref[c, g * VT_ROWS:(g + 1) * VT_ROWS, :], p[:, cols],
                         preferred_element_type=F32)
            acc_ref[t, g] = pv if first else alpha[:, cols] * acc_ref[t, g] + pv
        return m_new

    def finish(t):
        for g in range(N_KV_HEADS):
            acc = acc_ref[t, g]
            ot_ref[t, :, g * Q_COLS_PER_KV:(g + 1) * Q_COLS_PER_KV] = (
                acc[:HEAD_DIM] / acc[HEAD_DIM:HEAD_DIM + 1]).astype(BF16)

    if nchunks <= ATTN_UNROLL:
        assert nchunks % 2 == 0

        def tile_steps(t, nxt):
            m = None
            for c in range(nchunks):
                if c + 1 < nchunks:
                    scores(t, c + 1, (c + 1) % 2)
                elif nxt is not None:
                    scores(nxt, 0, 0)
                m = softmax_pv(t, c, c % 2, m, first=(c == 0))
            finish(t)

        scores(0, 0, 0)
        if ntiles > 1:
            def tile(t, carry):
                tile_steps(t, t + 1)
                return carry
            lax.fori_loop(0, ntiles - 1, tile, 0)
        tile_steps(ntiles - 1, None)
    else:
        assert ntiles == 1
        steady = nchunks - 1
        groups = (steady - 1) // ATTN_UNROLL
        peeled = steady - groups * ATTN_UNROLL

        def step(c, slot, m, first=False):
            scores(0, c + 1, 1 - slot)
            return softmax_pv(0, c, slot, m, first)

        scores(0, 0, 0)
        m = neg_inf
        for c in range(peeled):
            m = step(c, c % 2, m, first=(c == 0))

        def group(g, m):
            for u in range(ATTN_UNROLL):
                m = step(peeled + g * ATTN_UNROLL + u, (peeled + u) % 2, m)
            return m
        m = lax.fori_loop(0, groups, group, m)
        softmax_pv(0, nchunks - 1, (nchunks - 1) % 2, m)
        finish(0)


def _attention(qt, k, vt):
    B, nq = qt.shape[0], qt.shape[1]
    S = k.shape[1]
    tk = vt.shape[3]
    ntiles = min(nq, ATTN_TILES_PER_STEP) if S // tk <= ATTN_UNROLL else 1
    qspec = pl.BlockSpec((None, ntiles, LANES, Q_ROWS), lambda b, i: (b, i, 0, 0))
    return pl.pallas_call(
        _attn_kernel,
        grid=(B, nq // ntiles),
        in_specs=[qspec,
                  pl.BlockSpec((None, S, LANES), lambda b, i: (b, 0, 0)),
                  pl.BlockSpec((None,) + vt.shape[1:], lambda b, i: (b, 0, 0, 0))],
        out_specs=pl.BlockSpec((None, ntiles, HEAD_DIM, Q_ROWS), lambda b, i: (b, i, 0, 0)),
        out_shape=jax.ShapeDtypeStruct((B, nq, HEAD_DIM, Q_ROWS), BF16),
        scratch_shapes=[pltpu.VMEM((2, tk, Q_ROWS), F32),
                        pltpu.VMEM((2, 1, Q_ROWS), F32),
                        pltpu.VMEM((ntiles, N_KV_HEADS, VT_ROWS, Q_COLS_PER_KV), F32)],
        compiler_params=_cparams("parallel", "parallel"),
        name="attn",
    )(qt, k, vt)


def _dft_tables(S):
    n2 = DFT_N2
    n1 = S // n2
    c = np.arange(FOURIER_W)
    ang = 2.0 * np.pi * np.outer(c, c) / FOURIER_W
    chan = np.concatenate([np.cos(ang), -np.sin(ang)], axis=1) / np.sqrt(FOURIER_W)
    a1 = 2.0 * np.pi * np.outer(np.arange(n1), np.arange(n1)) / n1
    eye = np.eye(SUBLANES)
    mr = np.kron(np.cos(a1), eye) / np.sqrt(n1)
    mi = np.kron(-np.sin(a1), eye) / np.sqrt(n1)
    m1 = np.block([[mr, -mi], [mi, mr]])
    k1 = np.arange(n1)[None, :, None]
    s2 = (np.arange(n2 // SUBLANES)[:, None, None] * SUBLANES + np.arange(SUBLANES)[None, None, :])
    at = 2.0 * np.pi * (k1 * s2) / S
    twr = np.cos(at).reshape(n2 // SUBLANES, n1 * SUBLANES, 1)
    twi = (-np.sin(at)).reshape(n2 // SUBLANES, n1 * SUBLANES, 1)
    a2 = 2.0 * np.pi * np.outer(np.arange(n2), np.arange(n2)) / n2
    c2 = (np.cos(a2) / np.sqrt(n2))[:, None, None, :] * eye[None, :, :, None]
    s2m = (np.sin(a2) / np.sqrt(n2))[:, None, None, :] * eye[None, :, :, None]
    m2 = np.concatenate([c2.reshape(n2 * SUBLANES, n2 * SUBLANES),
                         s2m.reshape(n2 * SUBLANES, n2 * SUBLANES)], axis=1)
    return dict(chan=jnp.asarray(chan, BF16), m1=jnp.asarray(m1, BF16),
                twr=jnp.asarray(twr, F32), twi=jnp.asarray(twi, F32), m2=jnp.asarray(m2, BF16))


def _four1_kernel(fa_ref, chan_ref, m1_ref, twr_ref, twi_ref, br_ref, bi_ref):
    n1, sb, _ = fa_ref.shape
    rows = n1 * SUBLANES
    for i in range(sb // SUBLANES):
        rs = slice(i * SUBLANES, (i + 1) * SUBLANES)
        x = fa_ref[:, rs, :].reshape(rows, GROUP_W).astype(BF16)
        zr, zi = [], []
        for g in range(N_FOURIER_GROUPS):
            z = jnp.dot(x[:, g * FOURIER_W:(g + 1) * FOURIER_W], chan_ref[...],
                        preferred_element_type=F32)
            zr.append(z[:, :FOURIER_W])
            zi.append(z[:, FOURIER_W:])
        zcat = jnp.concatenate([jnp.concatenate(zr, axis=1), jnp.concatenate(zi, axis=1)],
                               axis=0).astype(BF16)
        a = jnp.dot(m1_ref[...], zcat, preferred_element_type=F32)
        ar, ai = a[:rows], a[rows:]
        twr, twi = twr_ref[i], twi_ref[i]
        br_ref[:, rs, :] = (ar * twr - ai * twi).reshape(n1, SUBLANES, GROUP_W)
        bi_ref[:, rs, :] = (ar * twi + ai * twr).reshape(n1, SUBLANES, GROUP_W)


def _four2_kernel(br_ref, bi_ref, m2_ref, wf_ref, fg_ref, o_ref):
    _, n2, _ = br_ref.shape
    rows = SUBLANES * n2
    bcat = jnp.concatenate([br_ref[...].reshape(rows, GROUP_W), bi_ref[...].reshape(rows, GROUP_W)],
                           axis=0).astype(BF16)
    f = jnp.dot(m2_ref[...], bcat, preferred_element_type=F32).astype(BF16)
    gate = _silu(fg_ref[...].reshape(rows, GROUP_W))
    ys = [jnp.dot(f[:, g * FOURIER_W:(g + 1) * FOURIER_W], wf_ref[g], preferred_element_type=F32)
          for g in range(N_FOURIER_GROUPS)]
    y = jnp.concatenate(ys, axis=1) * gate
    o_ref[...] = y.reshape(n2, SUBLANES, GROUP_W)


def _fourier(fa, fg, wf, ft):
    B, S, _ = fa.shape
    n2 = DFT_N2
    n1 = S // n2
    sb = 32 if n1 <= 16 else SUBLANES
    nsub = sb // SUBLANES
    fa4 = fa.reshape(B, n1, n2, GROUP_W)
    blk1 = pl.BlockSpec((None, n1, sb, GROUP_W), lambda b, j: (b, 0, j, 0))
    tw = pl.BlockSpec((nsub, n1 * SUBLANES, 1), lambda b, j: (j, 0, 0))
    br, bi = pl.pallas_call(
        _four1_kernel,
        grid=(B, n2 // sb),
        in_specs=[blk1, _const_spec((FOURIER_W, 2 * FOURIER_W)),
                  _const_spec((2 * n1 * SUBLANES, 2 * n1 * SUBLANES)), tw, tw],
        out_specs=[blk1, blk1],
        out_shape=[jax.ShapeDtypeStruct((B, n1, n2, GROUP_W), F32)] * 2,
        compiler_params=_cparams("parallel", "parallel"),
        name="four1",
    )(fa4, ft["chan"], ft["m1"], ft["twr"], ft["twi"])
    blk_in = pl.BlockSpec((None, SUBLANES, n2, GROUP_W), lambda b, j: (b, j, 0, 0))
    blk_out = pl.BlockSpec((None, n2, SUBLANES, GROUP_W), lambda b, j: (b, 0, j, 0))
    o = pl.pallas_call(
        _four2_kernel,
        grid=(B, n1 // SUBLANES),
        in_specs=[blk_in, blk_in, _const_spec((SUBLANES * n2, 2 * SUBLANES * n2)),
                  _const_spec((N_FOURIER_GROUPS, FOURIER_W, FOURIER_W)), blk_out],
        out_specs=blk_out,
        out_shape=jax.ShapeDtypeStruct((B, n2, n1, GROUP_W), F32),
        compiler_params=_cparams("parallel", "parallel"),
        name="four2",
    )(br, bi, ft["m2"], wf, fg.reshape(B, n2, n1, GROUP_W))
    return o.reshape(B, S, GROUP_W)


def _outproj_kernel(ot_ref, ag_ref, of_ref, os_ref, om_ref, x_ref, w_ref, g_ref, y_ref):
    for s in range(ot_ref.shape[0] // OUT_SUB_TILES):
        rows = slice(s * OUT_SUB_TILES * Q_TILE, (s + 1) * OUT_SUB_TILES * Q_TILE)
        tiles = []
        for t in range(s * OUT_SUB_TILES, (s + 1) * OUT_SUB_TILES):
            ot = ot_ref[t].astype(F32)
            slabs = []
            for j in range(GROUP_W // LANES):
                pair = jnp.concatenate([ot[:, (2 * j) * Q_TILE:(2 * j + 1) * Q_TILE],
                                        ot[:, (2 * j + 1) * Q_TILE:(2 * j + 2) * Q_TILE]], axis=0)
                slabs.append(pair.T)
            tiles.append(jnp.concatenate(slabs, axis=1))
        oa = (jnp.concatenate(tiles, axis=0) * _silu(ag_ref[rows, :].astype(F32))).astype(BF16)
        o = jnp.concatenate([oa, of_ref[rows, :].astype(BF16), os_ref[rows, :], om_ref[rows, :]],
                            axis=1)
        y = jnp.dot(o, w_ref[...], preferred_element_type=F32)
        y_ref[rows, :] = x_ref[rows, :] + _rms(y, g_ref[...])


def _outproj(ot, ag, of, osg, om, x, w, g):
    B, S, _ = x.shape
    tm = TOKEN_TILE
    tok = lambda wd: pl.BlockSpec((None, tm, wd), lambda b, i: (b, i, 0))
    return pl.pallas_call(
        _outproj_kernel,
        grid=(B, S // tm),
        in_specs=[pl.BlockSpec((None, tm // Q_TILE, HEAD_DIM, Q_ROWS), lambda b, i: (b, i, 0, 0)),
                  tok(GROUP_W), tok(GROUP_W), tok(GROUP_W), tok(GROUP_W), tok(D_MODEL),
                  _const_spec((4 * GROUP_W, D_MODEL)), _const_spec((1, D_MODEL))],
        out_specs=tok(D_MODEL),
        out_shape=jax.ShapeDtypeStruct((B, S, D_MODEL), F32),
        compiler_params=_cparams("parallel", "parallel"),
        name="outproj",
    )(ot, ag, of, osg, om, x, w, g)


def _rope_tables(S):
    rows = S // GRID_W
    row = jnp.broadcast_to(jnp.arange(rows, dtype=F32)[:, None], (rows, GRID_W)).reshape(S)
    col = jnp.broadcast_to(jnp.arange(GRID_W, dtype=F32)[None, :], (rows, GRID_W)).reshape(S)
    inv = ROPE_THETA ** (-jnp.arange(ROPE_PAIRS, dtype=F32) / ROPE_PAIRS)
    ang = jnp.stack([row[:, None] * inv, col[:, None] * inv], axis=1)
    cos, sin = jnp.cos(ang), jnp.sin(ang)
    zero = jnp.zeros((S, ROPE_PAIRS), F32)
    two = lambda parts: jnp.tile(jnp.concatenate(parts, axis=-1), (1, LANES // HEAD_DIM))
    return dict(cos=two([cos[:, 0], cos[:, 0], cos[:, 1], cos[:, 1]]),
                sina=two([zero, sin[:, 0], zero, sin[:, 1]]),
                sinb=two([-sin[:, 0], zero, -sin[:, 1], zero]))


def _layer_weights(l, pre_norm_g, w_in, q_norm_g, k_norm_g, w_fourier, sgu_norm_g, w_spatial,
                   b_spatial, mem_norm_g, w_mem_kv, w_out, post_norm_g):
    return dict(
        pre_g=pre_norm_g[l][None, :],
        w_in=w_in[l].astype(BF16),
        qg=jnp.tile(q_norm_g[l], N_HEADS)[None, :],
        kg=jnp.tile(k_norm_g[l], N_KV_HEADS)[None, :],
        wf=w_fourier[l].astype(BF16),
        vg=sgu_norm_g[l].reshape(1, GROUP_W),
        ws=w_spatial[l].astype(BF16),
        bs=jnp.repeat(b_spatial[l].T, SGU_W, axis=1),
        mem_g=mem_norm_g[l][None, :],
        w_mem_kv=w_mem_kv[l].astype(BF16),
        w_out=w_out[l].astype(BF16),
        post_g=post_norm_g[l][None, :],
    )


def _trunk(x, mem, layers):
    S = x.shape[1]
    tabs = _rope_tables(S)
    ids = np.arange(GROUP_W) // HEAD_DIM
    tabs["bd"] = jnp.asarray(ids[:, None] == ids[None, :], BF16)
    ft = _dft_tables(S)
    for lw in layers:
        kv = _memkv(mem, lw["mem_g"], lw["w_mem_kv"])
        qt, k, vt, ag, fa, fg, osgu, omem = _inproj(x, kv, lw, tabs)
        ot = _attention(qt, k, vt)
        ofour = _fourier(fa, fg, lw["wf"], ft)
        x = _outproj(ot, ag, ofour, osgu, omem, x, lw["w_out"], lw["post_g"])
    return x


def kernel(x_prompt, x_sample, mem_prompt, mem_sample, pre_norm_g, w_in, q_norm_g, k_norm_g,
           w_fourier, sgu_norm_g, w_spatial, b_spatial, mem_norm_g, w_mem_kv, w_out, post_norm_g):
    layers = [_layer_weights(l, pre_norm_g, w_in, q_norm_g, k_norm_g, w_fourier, sgu_norm_g,
                             w_spatial, b_spatial, mem_norm_g, w_mem_kv, w_out, post_norm_g)
              for l in range(DEPTH)]
    return (_trunk(x_prompt, mem_prompt, layers), _trunk(x_sample, mem_sample, layers))
```

```python
import functools

import numpy as np
import jax
import jax.numpy as jnp
from jax import lax
from jax.experimental import pallas as pl
from jax.experimental.pallas import tpu as pltpu

F32 = jnp.float32
BF16 = jnp.bfloat16

D_MODEL = 1024
DEPTH = 2
GRID_W = 64
N_MEM = 256
GROUP_W = 512
HEAD_DIM = 64
N_HEADS = 8
N_KV_HEADS = 2
Q_PER_KV = N_HEADS // N_KV_HEADS
ROPE_PAIRS = HEAD_DIM // 4
ROPE_THETA = 10000.0
N_FOURIER_GROUPS = 4
FOURIER_W = 128
N_SGU_HEADS = 4
SGU_W = 128
CHUNK = 128
N_MEM_HEADS = 4
MEM_HEAD_DIM = 128
EPS = 1e-6
IN_W = 4864
C_AQ, C_AK, C_AV, C_AG, C_FA, C_FG, C_SU, C_SV, C_SG, C_MQ, C_MG = (
    0, 512, 640, 768, 1280, 1792, 2304, 2816, 3328, 3840, 4352)

LANES = 128
SUBLANES = 8
VMEM_LIMIT_BYTES = 56 * 1024 * 1024

TOKEN_TILE = 512
KV_CHUNK = 512
Q_TILE = 128
Q_ROWS = N_HEADS * Q_TILE
DFT_N2 = 128
FOURIER_FUSED_MAX_SEQ = 2048
FOURIER_UNROLL = 4
ATTN_UNROLL = 6
OUT_SUB_TILES = 2
ATTN_TILES_PER_STEP = 8
BF16_SUBLANES = 16
VT_ROWS = HEAD_DIM + BF16_SUBLANES
Q_COLS_PER_KV = Q_PER_KV * Q_TILE
Q_SCALE = float(HEAD_DIM ** -0.5 * np.log2(np.e))


def _cparams(*sem):
    return pltpu.CompilerParams(dimension_semantics=sem, vmem_limit_bytes=VMEM_LIMIT_BYTES)


def _const_spec(shape):
    nd = len(shape)
    return pl.BlockSpec(shape, lambda *_: (0,) * nd)


def _silu(g):
    return g / (1.0 + jnp.exp(-g))


def _rms(x, g):
    ms = jnp.mean(x * x, axis=-1, keepdims=True)
    return x * lax.rsqrt(ms + EPS) * g


def _memkv_kernel(mem_ref, g_ref, w_ref, kv_ref):
    h = _rms(mem_ref[...], g_ref[...]).astype(BF16)
    kv_ref[...] = jnp.dot(h, w_ref[...], preferred_element_type=F32).astype(BF16)


def _memkv(mem, g, w):
    B = mem.shape[0]
    return pl.pallas_call(
        _memkv_kernel,
        grid=(B,),
        in_specs=[pl.BlockSpec((None, N_MEM, D_MODEL), lambda b: (b, 0, 0)),
                  _const_spec((1, D_MODEL)),
                  _const_spec((D_MODEL, 2 * GROUP_W))],
        out_specs=pl.BlockSpec((None, N_MEM, 2 * GROUP_W), lambda b: (b, 0, 0)),
        out_shape=jax.ShapeDtypeStruct((B, N_MEM, 2 * GROUP_W), BF16),
        compiler_params=_cparams("parallel"),
        name="memkv",
    )(mem, g, w)


def _head_ssq(x, bd):
    return jnp.dot((x * x).astype(BF16), bd, preferred_element_type=F32)


def _rope(x, cos, sina, sinb):
    return x * cos + pltpu.roll(x, 16, 1) * sina + pltpu.roll(x, LANES - 16, 1) * sinb


def _inproj_kernel(x_ref, pre_g_ref, w_ref, bd_ref, qg_ref, kg_ref, cos_ref, sina_ref, sinb_ref,
                   vg_ref, ws_ref, bs_ref, kv_ref,
                   qt_ref, k_ref, vt_ref, ag_ref, fa_ref, fg_ref, osgu_ref, omem_ref):
    tm = x_ref.shape[0]
    h = _rms(x_ref[...], pre_g_ref[...]).astype(BF16)

    def proj(lo, width):
        return jnp.dot(h, w_ref[:, lo:lo + width], preferred_element_type=F32)

    cos, sina, sinb = cos_ref[...], sina_ref[...], sinb_ref[...]

    aq = proj(C_AQ, GROUP_W)
    ssq = _head_ssq(aq, bd_ref[...])
    qn = aq * lax.rsqrt(ssq * (1.0 / HEAD_DIM) + EPS) * qg_ref[...]
    slabs = [_rope(qn[:, j * LANES:(j + 1) * LANES], cos, sina, sinb) * Q_SCALE
             for j in range(GROUP_W // LANES)]
    low = lax.broadcasted_iota(jnp.int32, (Q_TILE, LANES), 1) < HEAD_DIM
    for t in range(tm // Q_TILE):
        parts = []
        for j, slab in enumerate(slabs):
            sj = slab[t * Q_TILE:(t + 1) * Q_TILE, :]
            rolled = pltpu.roll(sj, HEAD_DIM, 1)
            if j < Q_PER_KV // 2:
                parts += [jnp.where(low, sj, 0.0), jnp.where(low, rolled, 0.0)]
            else:
                parts += [jnp.where(low, 0.0, rolled), jnp.where(low, 0.0, sj)]
        qt_ref[t] = jnp.concatenate(parts, axis=0).T.astype(BF16)

    ak = proj(C_AK, LANES)
    ssk = _head_ssq(ak, bd_ref[0:LANES, 0:LANES])
    kn = ak * lax.rsqrt(ssk * (1.0 / HEAD_DIM) + EPS) * kg_ref[...]
    k_ref[...] = _rope(kn, cos, sina, sinb).astype(BF16)
    av = proj(C_AV, LANES)
    avt = av.T.astype(BF16)
    for g in range(N_KV_HEADS):
        vt_ref[g * VT_ROWS:g * VT_ROWS + HEAD_DIM, :] = avt[g * HEAD_DIM:(g + 1) * HEAD_DIM, :]
        vt_ref[g * VT_ROWS + HEAD_DIM:(g + 1) * VT_ROWS, :] = jnp.ones((BF16_SUBLANES, tm), BF16)

    ag_ref[...] = proj(C_AG, GROUP_W).astype(BF16)
    fa_ref[...] = proj(C_FA, GROUP_W).astype(fa_ref.dtype)
    fg_ref[...] = proj(C_FG, GROUP_W).astype(fg_ref.dtype)

    su = proj(C_SU, GROUP_W)
    sv = proj(C_SV, GROUP_W)
    sg = proj(C_SG, GROUP_W)
    gate = _silu(sg)
    nck = tm // CHUNK
    for hd in range(N_SGU_HEADS):
        sl = slice(hd * SGU_W, (hd + 1) * SGU_W)
        vh = _rms(sv[:, sl], vg_ref[:, sl]).astype(BF16)
        vcat = jnp.concatenate([vh[c * CHUNK:(c + 1) * CHUNK, :] for c in range(nck)], axis=1)
        spc = jnp.dot(ws_ref[hd], vcat, preferred_element_type=F32)
        for c in range(nck):
            rows = slice(c * CHUNK, (c + 1) * CHUNK)
            sp = spc[:, c * SGU_W:(c + 1) * SGU_W] + bs_ref[:, sl]
            osgu_ref[rows, sl] = (su[rows, sl] * sp * gate[rows, sl]).astype(BF16)

    mq = proj(C_MQ, GROUP_W)
    mg = proj(C_MG, GROUP_W)
    mgate = _silu(mg)
    for hd in range(N_MEM_HEADS):
        sl = slice(hd * MEM_HEAD_DIM, (hd + 1) * MEM_HEAD_DIM)
        mk = kv_ref[:, hd * MEM_HEAD_DIM:(hd + 1) * MEM_HEAD_DIM]
        mv = kv_ref[:, GROUP_W + hd * MEM_HEAD_DIM:GROUP_W + (hd + 1) * MEM_HEAD_DIM]
        s = lax.dot_general(mq[:, sl].astype(BF16), mk, (((1,), (1,)), ((), ())),
                            preferred_element_type=F32) * (MEM_HEAD_DIM ** -0.5)
        e = jnp.exp(s - jnp.max(s, axis=-1, keepdims=True))
        o = jnp.dot(e.astype(BF16), mv, preferred_element_type=F32)
        o = o / jnp.sum(e, axis=-1, keepdims=True)
        omem_ref[:, sl] = (o * mgate[:, sl]).astype(BF16)


def _inproj(x, kv, lw, tabs):
    B, S, _ = x.shape
    tm = TOKEN_TILE
    nt = S // tm
    tok = lambda w: pl.BlockSpec((None, tm, w), lambda b, i: (b, i, 0))
    pos = pl.BlockSpec((tm, LANES), lambda b, i: (i, 0))
    in_specs = [
        tok(D_MODEL),
        _const_spec((1, D_MODEL)),
        _const_spec((D_MODEL, IN_W)),
        _const_spec((GROUP_W, GROUP_W)),
        _const_spec((1, GROUP_W)),
        _const_spec((1, LANES)),
        pos, pos, pos,
        _const_spec((1, GROUP_W)),
        _const_spec((N_SGU_HEADS, CHUNK, CHUNK)),
        _const_spec((CHUNK, GROUP_W)),
        pl.BlockSpec((None, N_MEM, 2 * GROUP_W), lambda b, i: (b, 0, 0)),
    ]
    out_specs = [
        pl.BlockSpec((None, tm // Q_TILE, LANES, Q_ROWS), lambda b, i: (b, i, 0, 0)), tok(LANES),
        pl.BlockSpec((None, None, N_KV_HEADS * VT_ROWS, tm), lambda b, i: (b, i, 0, 0)),
        tok(GROUP_W), tok(GROUP_W), tok(GROUP_W), tok(GROUP_W), tok(GROUP_W),
    ]
    sds = jax.ShapeDtypeStruct
    four_dt = BF16 if S <= FOURIER_FUSED_MAX_SEQ else F32
    out_shape = [
        sds((B, S // Q_TILE, LANES, Q_ROWS), BF16), sds((B, S, LANES), BF16),
        sds((B, nt, N_KV_HEADS * VT_ROWS, tm), BF16),
        sds((B, S, GROUP_W), BF16), sds((B, S, GROUP_W), four_dt), sds((B, S, GROUP_W), four_dt),
        sds((B, S, GROUP_W), BF16), sds((B, S, GROUP_W), BF16),
    ]
    return pl.pallas_call(
        _inproj_kernel,
        grid=(B, nt),
        in_specs=in_specs, out_specs=out_specs, out_shape=out_shape,
        compiler_params=_cparams("parallel", "parallel"),
        name="inproj",
    )(x, lw["pre_g"], lw["w_in"], tabs["bd"], lw["qg"], lw["kg"],
      tabs["cos"], tabs["sina"], tabs["sinb"], lw["vg"], lw["ws"], lw["bs"], kv)


def _attn_kernel(qt_ref, k_ref, vt_ref, ot_ref, st_ref, mx_ref, acc_ref):
    ntiles = qt_ref.shape[0]
    nchunks, _, tk = vt_ref.shape
    neg_inf = jnp.full((1, qt_ref.shape[2]), -jnp.inf, F32)

    def scores(t, c, slot):
        start = c * tk if isinstance(c, int) else pl.multiple_of(c * tk, tk)
        st = jnp.dot(k_ref[pl.ds(start, tk), :], qt_ref[t], preferred_element_type=F32)
        st_ref[slot] = st
        mx_ref[slot] = jnp.max(st, axis=0, keepdims=True)

    def softmax_pv(t, c, slot, m_old, first=False):
        st = st_ref[slot]
        m_new = mx_ref[slot]
        if not first:
            m_new = jnp.maximum(m_old, m_new)
        p = jnp.exp2(st - m_new).astype(BF16)
        alpha = None if first else jnp.exp2(m_old - m_new)
        for g in range(N_KV_HEADS):
            cols = slice(g * Q_COLS_PER_KV, (g + 1) * Q_COLS_PER_KV)
            pv = jnp.dot(vt_ref[c, g * VT_ROWS:(g + 1) * VT_ROWS, :], p[:, cols],
                         preferred_element_type=F32)
            acc_ref[t, g] = pv if first else alpha[:, cols] * acc_ref[t, g] + pv
        return m_new

    def finish(t):
        for g in range(N_KV_HEADS):
            acc = acc_ref[t, g]
            ot_ref[t, :, g * Q_COLS_PER_KV:(g + 1) * Q_COLS_PER_KV] = (
                acc[:HEAD_DIM] / acc[HEAD_DIM:HEAD_DIM + 1]).astype(BF16)

    if nchunks <= ATTN_UNROLL:
        assert nchunks % 2 == 0

        def tile_steps(t, nxt):
            m = None
            for c in range(nchunks):
                if c + 1 < nchunks:
                    scores(t, c + 1, (c + 1) % 2)
                elif nxt is not None:
                    scores(nxt, 0, 0)
                m = softmax_pv(t, c, c % 2, m, first=(c == 0))
            finish(t)

        scores(0, 0, 0)
        if ntiles > 1:
            def tile(t, carry):
                tile_steps(t, t + 1)
                return carry
            lax.fori_loop(0, ntiles - 1, tile, 0)
        tile_steps(ntiles - 1, None)
    else:
        assert ntiles == 1
        steady = nchunks - 1
        groups = (steady - 1) // ATTN_UNROLL
        peeled = steady - groups * ATTN_UNROLL

        def step(c, slot, m, first=False):
            scores(0, c + 1, 1 - slot)
            return softmax_pv(0, c, slot, m, first)

        scores(0, 0, 0)
        m = neg_inf
        for c in range(peeled):
            m = step(c, c % 2, m, first=(c == 0))

        def group(g, m):
            for u in range(ATTN_UNROLL):
                m = step(peeled + g * ATTN_UNROLL + u, (peeled + u) % 2, m)
            return m
        m = lax.fori_loop(0, groups, group, m)
        softmax_pv(0, nchunks - 1, (nchunks - 1) % 2, m)
        finish(0)


def _attention(qt, k, vt):
    B, nq = qt.shape[0], qt.shape[1]
    S = k.shape[1]
    tk = vt.shape[3]
    ntiles = min(nq, ATTN_TILES_PER_STEP) if S // tk <= ATTN_UNROLL else 1
    qspec = pl.BlockSpec((None, ntiles, LANES, Q_ROWS), lambda b, i: (b, i, 0, 0))
    return pl.pallas_call(
        _attn_kernel,
        grid=(B, nq // ntiles),
        in_specs=[qspec,
                  pl.BlockSpec((None, S, LANES), lambda b, i: (b, 0, 0)),
                  pl.BlockSpec((None,) + vt.shape[1:], lambda b, i: (b, 0, 0, 0))],
        out_specs=pl.BlockSpec((None, ntiles, HEAD_DIM, Q_ROWS), lambda b, i: (b, i, 0, 0)),
        out_shape=jax.ShapeDtypeStruct((B, nq, HEAD_DIM, Q_ROWS), BF16),
        scratch_shapes=[pltpu.VMEM((2, tk, Q_ROWS), F32),
                        pltpu.VMEM((2, 1, Q_ROWS), F32),
                        pltpu.VMEM((ntiles, N_KV_HEADS, VT_ROWS, Q_COLS_PER_KV), F32)],
        compiler_params=_cparams("parallel", "parallel"),
        name="attn",
    )(qt, k, vt)


def _dft_tables(S):
    n2 = DFT_N2
    n1 = S // n2
    c = np.arange(FOURIER_W)
    ang = 2.0 * np.pi * np.outer(c, c) / FOURIER_W
    chan = np.concatenate([np.cos(ang), -np.sin(ang)], axis=1) / np.sqrt(FOURIER_W)
    a1 = 2.0 * np.pi * np.outer(np.arange(n1), np.arange(n1)) / n1
    eye = np.eye(SUBLANES)
    mr = np.kron(np.cos(a1), eye) / np.sqrt(n1)
    mi = np.kron(-np.sin(a1), eye) / np.sqrt(n1)
    m1 = np.block([[mr, -mi], [mi, mr]])
    k1 = np.arange(n1)[None, :, None]
    s2 = (np.arange(n2 // SUBLANES)[:, None, None] * SUBLANES + np.arange(SUBLANES)[None, None, :])
    at = 2.0 * np.pi * (k1 * s2) / S
    twr = np.cos(at).reshape(n2 // SUBLANES, n1 * SUBLANES, 1)
    twi = (-np.sin(at)).reshape(n2 // SUBLANES, n1 * SUBLANES, 1)
    a2 = 2.0 * np.pi * np.outer(np.arange(n2), np.arange(n2)) / n2
    c2 = (np.cos(a2) / np.sqrt(n2))[:, None, None, :] * eye[None, :, :, None]
    s2m = (np.sin(a2) / np.sqrt(n2))[:, None, None, :] * eye[None, :, :, None]
    m2 = np.concatenate([c2.reshape(n2 * SUBLANES, n2 * SUBLANES),
                         s2m.reshape(n2 * SUBLANES, n2 * SUBLANES)], axis=1)
    c2s2 = np.concatenate([np.cos(a2), np.sin(a2)], axis=1) / np.sqrt(n2)
    return dict(chan=jnp.asarray(chan, BF16), m1=jnp.asarray(m1, BF16),
                twr=jnp.asarray(twr, F32), twi=jnp.asarray(twi, F32), m2=jnp.asarray(m2, BF16),
                c2=jnp.asarray(c2s2, BF16))


def _four1_kernel(fa_ref, chan_ref, m1_ref, twr_ref, twi_ref, br_ref, bi_ref):
    n1, sb, _ = fa_ref.shape
    rows = n1 * SUBLANES
    for i in range(sb // SUBLANES):
        rs = slice(i * SUBLANES, (i + 1) * SUBLANES)
        x = fa_ref[:, rs, :].reshape(rows, GROUP_W).astype(BF16)
        zr, zi = [], []
        for g in range(N_FOURIER_GROUPS):
            z = jnp.dot(x[:, g * FOURIER_W:(g + 1) * FOURIER_W], chan_ref[...],
                        preferred_element_type=F32)
            zr.append(z[:, :FOURIER_W])
            zi.append(z[:, FOURIER_W:])
        zcat = jnp.concatenate([jnp.concatenate(zr, axis=1), jnp.concatenate(zi, axis=1)],
                               axis=0).astype(BF16)
        a = jnp.dot(m1_ref[...], zcat, preferred_element_type=F32)
        ar, ai = a[:rows], a[rows:]
        twr, twi = twr_ref[i], twi_ref[i]
        br_ref[:, rs, :] = (ar * twr - ai * twi).reshape(n1, SUBLANES, GROUP_W)
        bi_ref[:, rs, :] = (ar * twi + ai * twr).reshape(n1, SUBLANES, GROUP_W)


def _four2_kernel(br_ref, bi_ref, m2_ref, wf_ref, fg_ref, o_ref):
    _, n2, _ = br_ref.shape
    rows = SUBLANES * n2
    bcat = jnp.concatenate([br_ref[...].reshape(rows, GROUP_W), bi_ref[...].reshape(rows, GROUP_W)],
                           axis=0).astype(BF16)
    f = jnp.dot(m2_ref[...], bcat, preferred_element_type=F32).astype(BF16)
    gate = _silu(fg_ref[...].reshape(rows, GROUP_W))
    ys = [jnp.dot(f[:, g * FOURIER_W:(g + 1) * FOURIER_W], wf_ref[g], preferred_element_type=F32)
          for g in range(N_FOURIER_GROUPS)]
    y = jnp.concatenate(ys, axis=1) * gate
    o_ref[...] = y.reshape(n2, SUBLANES, GROUP_W)


def _four_fused_kernel(fa_ref, fg_ref, chan_ref, m1_ref, twr_ref, twi_ref, c2_ref, wf_ref, o_ref,
                       zr_ref, zi_ref, br_ref, bi_ref, f_ref):
    S = fa_ref.shape[0]
    n2 = DFT_N2
    n1 = S // n2
    lanes = lambda g: slice(g * FOURIER_W, (g + 1) * FOURIER_W)

    x = fa_ref[...].astype(BF16)
    for g in range(N_FOURIER_GROUPS):
        z = jnp.dot(x[:, lanes(g)], chan_ref[...], preferred_element_type=F32)
        zr_ref[:, lanes(g)] = z[:, :FOURIER_W]
        zi_ref[:, lanes(g)] = z[:, FOURIER_W:]

    def stage1(j, carry):
        base = pl.multiple_of(j * SUBLANES, SUBLANES)
        tiles = ([zr_ref[pl.ds(s1 * n2 + base, SUBLANES), :] for s1 in range(n1)]
                 + [zi_ref[pl.ds(s1 * n2 + base, SUBLANES), :] for s1 in range(n1)])
        a = jnp.dot(m1_ref[...], jnp.concatenate(tiles, axis=0).astype(BF16),
                    preferred_element_type=F32)
        ar, ai = a[:n1 * SUBLANES], a[n1 * SUBLANES:]
        twr, twi = twr_ref[j], twi_ref[j]
        br = ar * twr - ai * twi
        bi = ar * twi + ai * twr
        for k1 in range(n1):
            rows = slice(k1 * SUBLANES, (k1 + 1) * SUBLANES)
            br_ref[pl.ds(k1 * n2 + base, SUBLANES), :] = br[rows]
            bi_ref[pl.ds(k1 * n2 + base, SUBLANES), :] = bi[rows]
        return carry

    lax.fori_loop(0, n2 // SUBLANES, stage1, 0, unroll=FOURIER_UNROLL)

    def stage2(k1, carry):
        start = pl.multiple_of(k1 * n2, n2)
        bcat = jnp.concatenate([br_ref[pl.ds(start, n2), :], bi_ref[pl.ds(start, n2), :]],
                               axis=0).astype(BF16)
        f = jnp.dot(c2_ref[...], bcat, preferred_element_type=F32)
        for g in range(N_FOURIER_GROUPS):
            f_ref[g, pl.ds(k1, n2, stride=n1), :] = f[:, lanes(g)]
        return carry

    lax.fori_loop(0, n1, stage2, 0, unroll=FOURIER_UNROLL)

    gate = _silu(fg_ref[...].astype(F32))
    for g in range(N_FOURIER_GROUPS):
        y = jnp.dot(f_ref[g].astype(BF16), wf_ref[g], preferred_element_type=F32)
        o_ref[:, lanes(g)] = (y * gate[:, lanes(g)]).astype(o_ref.dtype)


def _fourier_fused(fa, fg, wf, ft):
    B, S, _ = fa.shape
    n1 = S // DFT_N2
    tok = pl.BlockSpec((None, S, GROUP_W), lambda b: (b, 0, 0))
    return pl.pallas_call(
        _four_fused_kernel,
        grid=(B,),
        in_specs=[tok, tok, _const_spec((FOURIER_W, 2 * FOURIER_W)),
                  _const_spec((2 * n1 * SUBLANES, 2 * n1 * SUBLANES)),
                  _const_spec(ft["twr"].shape), _const_spec(ft["twi"].shape),
                  _const_spec((DFT_N2, 2 * DFT_N2)),
                  _const_spec((N_FOURIER_GROUPS, FOURIER_W, FOURIER_W))],
        out_specs=tok,
        out_shape=jax.ShapeDtypeStruct((B, S, GROUP_W), BF16),
        scratch_shapes=[pltpu.VMEM((S, GROUP_W), F32)] * 4
                       + [pltpu.VMEM((N_FOURIER_GROUPS, S, FOURIER_W), F32)],
        compiler_params=_cparams("parallel"),
        name="four",
    )(fa, fg, ft["chan"], ft["m1"], ft["twr"], ft["twi"], ft["c2"], wf)


def _fourier(fa, fg, wf, ft):
    B, S, _ = fa.shape
    if S <= FOURIER_FUSED_MAX_SEQ:
        return _fourier_fused(fa, fg, wf, ft)
    n2 = DFT_N2
    n1 = S // n2
    sb = 32 if n1 <= 16 else SUBLANES
    nsub = sb // SUBLANES
    fa4 = fa.reshape(B, n1, n2, GROUP_W)
    blk1 = pl.BlockSpec((None, n1, sb, GROUP_W), lambda b, j: (b, 0, j, 0))
    tw = pl.BlockSpec((nsub, n1 * SUBLANES, 1), lambda b, j: (j, 0, 0))
    br, bi = pl.pallas_call(
        _four1_kernel,
        grid=(B, n2 // sb),
        in_specs=[blk1, _const_spec((FOURIER_W, 2 * FOURIER_W)),
                  _const_spec((2 * n1 * SUBLANES, 2 * n1 * SUBLANES)), tw, tw],
        out_specs=[blk1, blk1],
        out_shape=[jax.ShapeDtypeStruct((B, n1, n2, GROUP_W), F32)] * 2,
        compiler_params=_cparams("parallel", "parallel"),
        name="four1",
    )(fa4, ft["chan"], ft["m1"], ft["twr"], ft["twi"])
    blk_in = pl.BlockSpec((None, SUBLANES, n2, GROUP_W), lambda b, j: (b, j, 0, 0))
    blk_out = pl.BlockSpec((None, n2, SUBLANES, GROUP_W), lambda b, j: (b, 0, j, 0))
    o = pl.pallas_call(
        _four2_kernel,
        grid=(B, n1 // SUBLANES),
        in_specs=[blk_in, blk_in, _const_spec((SUBLANES * n2, 2 * SUBLANES * n2)),
                  _const_spec((N_FOURIER_GROUPS, FOURIER_W, FOURIER_W)), blk_out],
        out_specs=blk_out,
        out_shape=jax.ShapeDtypeStruct((B, n2, n1, GROUP_W), F32),
        compiler_params=_cparams("parallel", "parallel"),
        name="four2",
    )(br, bi, ft["m2"], wf, fg.reshape(B, n2, n1, GROUP_W))
    return o.reshape(B, S, GROUP_W)


def _outproj_kernel(ot_ref, ag_ref, of_ref, os_ref, om_ref, x_ref, w_ref, g_ref, y_ref):
    for s in range(ot_ref.shape[0] // OUT_SUB_TILES):
        rows = slice(s * OUT_SUB_TILES * Q_TILE, (s + 1) * OUT_SUB_TILES * Q_TILE)
        tiles = []
        for t in range(s * OUT_SUB_TILES, (s + 1) * OUT_SUB_TILES):
            ot = ot_ref[t].astype(F32)
            slabs = []
            for j in range(GROUP_W // LANES):
                pair = jnp.concatenate([ot[:, (2 * j) * Q_TILE:(2 * j + 1) * Q_TILE],
                                        ot[:, (2 * j + 1) * Q_TILE:(2 * j + 2) * Q_TILE]], axis=0)
                slabs.append(pair.T)
            tiles.append(jnp.concatenate(slabs, axis=1))
        oa = (jnp.concatenate(tiles, axis=0) * _silu(ag_ref[rows, :].astype(F32))).astype(BF16)
        o = jnp.concatenate([oa, of_ref[rows, :].astype(BF16), os_ref[rows, :], om_ref[rows, :]],
                            axis=1)
        y = jnp.dot(o, w_ref[...], preferred_element_type=F32)
        y_ref[rows, :] = x_ref[rows, :] + _rms(y, g_ref[...])


def _outproj(ot, ag, of, osg, om, x, w, g):
    B, S, _ = x.shape
    tm = TOKEN_TILE
    tok = lambda wd: pl.BlockSpec((None, tm, wd), lambda b, i: (b, i, 0))
    return pl.pallas_call(
        _outproj_kernel,
        grid=(B, S // tm),
        in_specs=[pl.BlockSpec((None, tm // Q_TILE, HEAD_DIM, Q_ROWS), lambda b, i: (b, i, 0, 0)),
                  tok(GROUP_W), tok(GROUP_W), tok(GROUP_W), tok(GROUP_W), tok(D_MODEL),
                  _const_spec((4 * GROUP_W, D_MODEL)), _const_spec((1, D_MODEL))],
        out_specs=tok(D_MODEL),
        out_shape=jax.ShapeDtypeStruct((B, S, D_MODEL), F32),
        compiler_params=_cparams("parallel", "parallel"),
        name="outproj",
    )(ot, ag, of, osg, om, x, w, g)


def _rope_tables(S):
    rows = S // GRID_W
    row = jnp.broadcast_to(jnp.arange(rows, dtype=F32)[:, None], (rows, GRID_W)).reshape(S)
    col = jnp.broadcast_to(jnp.arange(GRID_W, dtype=F32)[None, :], (rows, GRID_W)).reshape(S)
    inv = ROPE_THETA ** (-jnp.arange(ROPE_PAIRS, dtype=F32) / ROPE_PAIRS)
    ang = jnp.stack([row[:, None] * inv, col[:, None] * inv], axis=1)
    cos, sin = jnp.cos(ang), jnp.sin(ang)
    zero = jnp.zeros((S, ROPE_PAIRS), F32)
    two = lambda parts: jnp.tile(jnp.concatenate(parts, axis=-1), (1, LANES // HEAD_DIM))
    return dict(cos=two([cos[:, 0], cos[:, 0], cos[:, 1], cos[:, 1]]),
                sina=two([zero, sin[:, 0], zero, sin[:, 1]]),
                sinb=two([-sin[:, 0], zero, -sin[:, 1], zero]))


def _layer_weights(l, pre_norm_g, w_in, q_norm_g, k_norm_g, w_fourier, sgu_norm_g, w_spatial,
                   b_spatial, mem_norm_g, w_mem_kv, w_out, post_norm_g):
    return dict(
        pre_g=pre_norm_g[l][None, :],
        w_in=w_in[l].astype(BF16),
        qg=jnp.tile(q_norm_g[l], N_HEADS)[None, :],
        kg=jnp.tile(k_norm_g[l], N_KV_HEADS)[None, :],
        wf=w_fourier[l].astype(BF16),
        vg=sgu_norm_g[l].reshape(1, GROUP_W),
        ws=w_spatial[l].astype(BF16),
        bs=jnp.repeat(b_spatial[l].T, SGU_W, axis=1),
        mem_g=mem_norm_g[l][None, :],
        w_mem_kv=w_mem_kv[l].astype(BF16),
        w_out=w_out[l].astype(BF16),
        post_g=post_norm_g[l][None, :],
    )


def _trunk(x, mem, layers):
    S = x.shape[1]
    tabs = _rope_tables(S)
    ids = np.arange(GROUP_W) // HEAD_DIM
    tabs["bd"] = jnp.asarray(ids[:, None] == ids[None, :], BF16)
    ft = _dft_tables(S)
    for lw in layers:
        kv = _memkv(mem, lw["mem_g"], lw["w_mem_kv"])
        qt, k, vt, ag, fa, fg, osgu, omem = _inproj(x, kv, lw, tabs)
        ot = _attention(qt, k, vt)
        ofour = _fourier(fa, fg, lw["wf"], ft)
        x = _outproj(ot, ag, ofour, osgu, omem, x, lw["w_out"], lw["post_g"])
    return x


def kernel(x_prompt, x_sample, mem_prompt, mem_sample, pre_norm_g, w_in, q_norm_g, k_norm_g,
           w_fourier, sgu_norm_g, w_spatial, b_spatial, mem_norm_g, w_mem_kv, w_out, post_norm_g):
    layers = [_layer_weights(l, pre_norm_g, w_in, q_norm_g, k_norm_g, w_fourier, sgu_norm_g,
                             w_spatial, b_spatial, mem_norm_g, w_mem_kv, w_out, post_norm_g)
              for l in range(DEPTH)]
    return (_trunk(x_prompt, mem_prompt, layers), _trunk(x_sample, mem_sample, layers))
```

```python
import functools

import numpy as np
import jax
import jax.numpy as jnp
from jax import lax
from jax.experimental import pallas as pl
from jax.experimental.pallas import tpu as pltpu

F32 = jnp.float32
BF16 = jnp.bfloat16

D_MODEL = 1024
DEPTH = 2
GRID_W = 64
N_MEM = 256
GROUP_W = 512
HEAD_DIM = 64
N_HEADS = 8
N_KV_HEADS = 2
Q_PER_KV = N_HEADS // N_KV_HEADS
ROPE_PAIRS = HEAD_DIM // 4
ROPE_THETA = 10000.0
N_FOURIER_GROUPS = 4
FOURIER_W = 128
N_SGU_HEADS = 4
SGU_W = 128
CHUNK = 128
N_MEM_HEADS = 4
MEM_HEAD_DIM = 128
EPS = 1e-6
IN_W = 4864
C_AQ, C_AK, C_AV, C_AG, C_FA, C_FG, C_SU, C_SV, C_SG, C_MQ, C_MG = (
    0, 512, 640, 768, 1280, 1792, 2304, 2816, 3328, 3840, 4352)

LANES = 128
SUBLANES = 8
VMEM_LIMIT_BYTES = 56 * 1024 * 1024

TOKEN_TILE = 512
KV_CHUNK = 512
Q_TILE = 128
Q_ROWS = N_HEADS * Q_TILE
DFT_N2 = 128
FOURIER_FUSED_MAX_SEQ = 2048
FOURIER_UNROLL = 4
ATTN_UNROLL = 6
OUT_SUB_TILES = 2
ATTN_TILES_PER_STEP = 8
BF16_SUBLANES = 16
VT_ROWS = HEAD_DIM + BF16_SUBLANES
Q_COLS_PER_KV = Q_PER_KV * Q_TILE
Q_SCALE = float(HEAD_DIM ** -0.5 * np.log2(np.e))


def _cparams(*sem):
    return pltpu.CompilerParams(dimension_semantics=sem, vmem_limit_bytes=VMEM_LIMIT_BYTES)


def _const_spec(shape):
    nd = len(shape)
    return pl.BlockSpec(shape, lambda *_: (0,) * nd)


def _silu(g):
    return g / (1.0 + jnp.exp(-g))


def _rms(x, g):
    ms = jnp.mean(x * x, axis=-1, keepdims=True)
    return x * lax.rsqrt(ms + EPS) * g


def _memkv_kernel(mem_ref, g_ref, w_ref, kv_ref):
    h = _rms(mem_ref[...], g_ref[...]).astype(BF16)
    kv_ref[...] = jnp.dot(h, w_ref[...], preferred_element_type=F32).astype(BF16)


def _memkv(mem, g, w):
    B = mem.shape[0]
    return pl.pallas_call(
        _memkv_kernel,
        grid=(B,),
        in_specs=[pl.BlockSpec((None, N_MEM, D_MODEL), lambda b: (b, 0, 0)),
                  _const_spec((1, D_MODEL)),
                  _const_spec((D_MODEL, 2 * GROUP_W))],
        out_specs=pl.BlockSpec((None, N_MEM, 2 * GROUP_W), lambda b: (b, 0, 0)),
        out_shape=jax.ShapeDtypeStruct((B, N_MEM, 2 * GROUP_W), BF16),
        compiler_params=_cparams("parallel"),
        name="memkv",
    )(mem, g, w)


def _head_ssq(x, bd):
    return jnp.dot((x * x).astype(BF16), bd, preferred_element_type=F32)


def _rope(x, cos, sina, sinb):
    return x * cos + pltpu.roll(x, 16, 1) * sina + pltpu.roll(x, LANES - 16, 1) * sinb


def _inproj_kernel(x_ref, pre_g_ref, w_ref, bd_ref, qg_ref, kg_ref, cos_ref, sina_ref, sinb_ref,
                   vg_ref, ws_ref, bs_ref, kv_ref,
                   qt_ref, k_ref, vt_ref, ag_ref, fa_ref, fg_ref, osgu_ref, omem_ref):
    tm = x_ref.shape[0]
    h = _rms(x_ref[...], pre_g_ref[...]).astype(BF16)

    def proj(lo, width):
        return jnp.dot(h, w_ref[:, lo:lo + width], preferred_element_type=F32)

    cos, sina, sinb = cos_ref[...], sina_ref[...], sinb_ref[...]
    nck = tm // CHUNK
    lanes = lambda i: slice(i * LANES, (i + 1) * LANES)

    aq = proj(C_AQ, GROUP_W)
    ak = proj(C_AK, LANES)
    av = proj(C_AV, LANES)
    mq = proj(C_MQ, GROUP_W)
    sv = proj(C_SV, GROUP_W)
    ssq = _head_ssq(aq, bd_ref[...])
    ssk = _head_ssq(ak, bd_ref[0:LANES, 0:LANES])

    mq16 = mq.astype(BF16)
    scores = [lax.dot_general(mq16[:, lanes(hd)], kv_ref[:, lanes(hd)], (((1,), (1,)), ((), ())),
                              preferred_element_type=F32) * (MEM_HEAD_DIM ** -0.5)
              for hd in range(N_MEM_HEADS)]

    su = proj(C_SU, GROUP_W)
    sg = proj(C_SG, GROUP_W)
    mg = proj(C_MG, GROUP_W)

    vhs = [_rms(sv[:, lanes(hd)], vg_ref[:, lanes(hd)]).astype(BF16) for hd in range(N_SGU_HEADS)]
    spcs = [jnp.dot(ws_ref[hd],
                    jnp.concatenate([vhs[hd][c * CHUNK:(c + 1) * CHUNK, :] for c in range(nck)],
                                    axis=1), preferred_element_type=F32)
            for hd in range(N_SGU_HEADS)]

    ag_ref[...] = proj(C_AG, GROUP_W).astype(BF16)
    fa_ref[...] = proj(C_FA, GROUP_W).astype(fa_ref.dtype)
    fg_ref[...] = proj(C_FG, GROUP_W).astype(fg_ref.dtype)

    es = [jnp.exp(s - jnp.max(s, axis=-1, keepdims=True)) for s in scores]
    mgate = _silu(mg)
    for hd in range(N_MEM_HEADS):
        mv = kv_ref[:, GROUP_W + hd * MEM_HEAD_DIM:GROUP_W + (hd + 1) * MEM_HEAD_DIM]
        o = jnp.dot(es[hd].astype(BF16), mv, preferred_element_type=F32)
        o = o / jnp.sum(es[hd], axis=-1, keepdims=True)
        omem_ref[:, lanes(hd)] = (o * mgate[:, lanes(hd)]).astype(BF16)

    qn = aq * lax.rsqrt(ssq * (1.0 / HEAD_DIM) + EPS) * qg_ref[...]
    slabs = [_rope(qn[:, lanes(j)], cos, sina, sinb) * Q_SCALE for j in range(GROUP_W // LANES)]
    low = lax.broadcasted_iota(jnp.int32, (Q_TILE, LANES), 1) < HEAD_DIM
    for t in range(tm // Q_TILE):
        parts = []
        for j, slab in enumerate(slabs):
            sj = slab[t * Q_TILE:(t + 1) * Q_TILE, :]
            rolled = pltpu.roll(sj, HEAD_DIM, 1)
            if j < Q_PER_KV // 2:
                parts += [jnp.where(low, sj, 0.0), jnp.where(low, rolled, 0.0)]
            else:
                parts += [jnp.where(low, 0.0, rolled), jnp.where(low, 0.0, sj)]
        qt_ref[t] = jnp.concatenate(parts, axis=0).T.astype(BF16)

    kn = ak * lax.rsqrt(ssk * (1.0 / HEAD_DIM) + EPS) * kg_ref[...]
    k_ref[...] = _rope(kn, cos, sina, sinb).astype(BF16)
    avt = av.T.astype(BF16)
    for g in range(N_KV_HEADS):
        vt_ref[g * VT_ROWS:g * VT_ROWS + HEAD_DIM, :] = avt[g * HEAD_DIM:(g + 1) * HEAD_DIM, :]
        vt_ref[g * VT_ROWS + HEAD_DIM:(g + 1) * VT_ROWS, :] = jnp.ones((BF16_SUBLANES, tm), BF16)

    gate = _silu(sg)
    for hd in range(N_SGU_HEADS):
        for c in range(nck):
            rows = slice(c * CHUNK, (c + 1) * CHUNK)
            sp = spcs[hd][:, c * SGU_W:(c + 1) * SGU_W] + bs_ref[:, lanes(hd)]
            osgu_ref[rows, lanes(hd)] = (su[rows, lanes(hd)] * sp
                                         * gate[rows, lanes(hd)]).astype(BF16)


def _inproj(x, kv, lw, tabs):
    B, S, _ = x.shape
    tm = TOKEN_TILE
    nt = S // tm
    tok = lambda w: pl.BlockSpec((None, tm, w), lambda b, i: (b, i, 0))
    pos = pl.BlockSpec((tm, LANES), lambda b, i: (i, 0))
    in_specs = [
        tok(D_MODEL),
        _const_spec((1, D_MODEL)),
        _const_spec((D_MODEL, IN_W)),
        _const_spec((GROUP_W, GROUP_W)),
        _const_spec((1, GROUP_W)),
        _const_spec((1, LANES)),
        pos, pos, pos,
        _const_spec((1, GROUP_W)),
        _const_spec((N_SGU_HEADS, CHUNK, CHUNK)),
        _const_spec((CHUNK, GROUP_W)),
        pl.BlockSpec((None, N_MEM, 2 * GROUP_W), lambda b, i: (b, 0, 0)),
    ]
    out_specs = [
        pl.BlockSpec((None, tm // Q_TILE, LANES, Q_ROWS), lambda b, i: (b, i, 0, 0)), tok(LANES),
        pl.BlockSpec((None, None, N_KV_HEADS * VT_ROWS, tm), lambda b, i: (b, i, 0, 0)),
        tok(GROUP_W), tok(GROUP_W), tok(GROUP_W), tok(GROUP_W), tok(GROUP_W),
    ]
    sds = jax.ShapeDtypeStruct
    four_dt = BF16 if S <= FOURIER_FUSED_MAX_SEQ else F32
    out_shape = [
        sds((B, S // Q_TILE, LANES, Q_ROWS), BF16), sds((B, S, LANES), BF16),
        sds((B, nt, N_KV_HEADS * VT_ROWS, tm), BF16),
        sds((B, S, GROUP_W), BF16), sds((B, S, GROUP_W), four_dt), sds((B, S, GROUP_W), four_dt),
        sds((B, S, GROUP_W), BF16), sds((B, S, GROUP_W), BF16),
    ]
    return pl.pallas_call(
        _inproj_kernel,
        grid=(B, nt),
        in_specs=in_specs, out_specs=out_specs, out_shape=out_shape,
        compiler_params=_cparams("parallel", "parallel"),
        name="inproj",
    )(x, lw["pre_g"], lw["w_in"], tabs["bd"], lw["qg"], lw["kg"],
      tabs["cos"], tabs["sina"], tabs["sinb"], lw["vg"], lw["ws"], lw["bs"], kv)


def _attn_kernel(qt_ref, k_ref, vt_ref, ot_ref, st_ref, mx_ref, acc_ref):
    ntiles = qt_ref.shape[0]
    nchunks, _, tk = vt_ref.shape
    neg_inf = jnp.full((1, qt_ref.shape[2]), -jnp.inf, F32)

    def scores(t, c, slot):
        start = c * tk if isinstance(c, int) else pl.multiple_of(c * tk, tk)
        st = jnp.dot(k_ref[pl.ds(start, tk), :], qt_ref[t], preferred_element_type=F32)
        st_ref[slot] = st
        mx_ref[slot] = jnp.max(st, axis=0, keepdims=True)

    def softmax_pv(t, c, slot, m_old, first=False):
        st = st_ref[slot]
        m_new = mx_ref[slot]
        if not first:
            m_new = jnp.maximum(m_old, m_new)
        p = jnp.exp2(st - m_new).astype(BF16)
        alpha = None if first else jnp.exp2(m_old - m_new)
        for g in range(N_KV_HEADS):
            cols = slice(g * Q_COLS_PER_KV, (g + 1) * Q_COLS_PER_KV)
            pv = jnp.dot(vt_ref[c, g * VT_ROWS:(g + 1) * VT_ROWS, :], p[:, cols],
                         preferred_element_type=F32)
            acc_ref[t, g] = pv if first else alpha[:, cols] * acc_ref[t, g] + pv
        return m_new

    def finish(t):
        for g in range(N_KV_HEADS):
            acc = acc_ref[t, g]
            ot_ref[t, :, g * Q_COLS_PER_KV:(g + 1) * Q_COLS_PER_KV] = (
                acc[:HEAD_DIM] / acc[HEAD_DIM:HEAD_DIM + 1]).astype(BF16)

    if nchunks <= ATTN_UNROLL:
        assert nchunks % 2 == 0

        def tile_steps(t, nxt):
            m = None
            for c in range(nchunks):
                if c + 1 < nchunks:
                    scores(t, c + 1, (c + 1) % 2)
                elif nxt is not None:
                    scores(nxt, 0, 0)
                m = softmax_pv(t, c, c % 2, m, first=(c == 0))
            finish(t)

        scores(0, 0, 0)
        if ntiles > 1:
            def tile(t, carry):
                tile_steps(t, t + 1)
                return carry
            lax.fori_loop(0, ntiles - 1, tile, 0)
        tile_steps(ntiles - 1, None)
    else:
        assert ntiles == 1
        steady = nchunks - 1
        groups = (steady - 1) // ATTN_UNROLL
        peeled = steady - groups * ATTN_UNROLL

        def step(c, slot, m, first=False):
            scores(0, c + 1, 1 - slot)
            return softmax_pv(0, c, slot, m, first)

        scores(0, 0, 0)
        m = neg_inf
        for c in range(peeled):
            m = step(c, c % 2, m, first=(c == 0))

        def group(g, m):
            for u in range(ATTN_UNROLL):
                m = step(peeled + g * ATTN_UNROLL + u, (peeled + u) % 2, m)
            return m
        m = lax.fori_loop(0, groups, group, m)
        softmax_pv(0, nchunks - 1, (nchunks - 1) % 2, m)
        finish(0)


def _attention(qt, k, vt):
    B, nq = qt.shape[0], qt.shape[1]
    S = k.shape[1]
    tk = vt.shape[3]
    ntiles = min(nq, ATTN_TILES_PER_STEP) if S // tk <= ATTN_UNROLL else 1
    qspec = pl.BlockSpec((None, ntiles, LANES, Q_ROWS), lambda b, i: (b, i, 0, 0))
    return pl.pallas_call(
        _attn_kernel,
        grid=(B, nq // ntiles),
        in_specs=[qspec,
                  pl.BlockSpec((None, S, LANES), lambda b, i: (b, 0, 0)),
                  pl.BlockSpec((None,) + vt.shape[1:], lambda b, i: (b, 0, 0, 0))],
        out_specs=pl.BlockSpec((None, ntiles, HEAD_DIM, Q_ROWS), lambda b, i: (b, i, 0, 0)),
        out_shape=jax.ShapeDtypeStruct((B, nq, HEAD_DIM, Q_ROWS), BF16),
        scratch_shapes=[pltpu.VMEM((2, tk, Q_ROWS), F32),
                        pltpu.VMEM((2, 1, Q_ROWS), F32),
                        pltpu.VMEM((ntiles, N_KV_HEADS, VT_ROWS, Q_COLS_PER_KV), F32)],
        compiler_params=_cparams("parallel", "parallel"),
        name="attn",
    )(qt, k, vt)


def _dft_tables(S):
    n2 = DFT_N2
    n1 = S // n2
    c = np.arange(FOURIER_W)
    ang = 2.0 * np.pi * np.outer(c, c) / FOURIER_W
    chan = np.concatenate([np.cos(ang), -np.sin(ang)], axis=1) / np.sqrt(FOURIER_W)
    a1 = 2.0 * np.pi * np.outer(np.arange(n1), np.arange(n1)) / n1
    eye = np.eye(SUBLANES)
    mr = np.kron(np.cos(a1), eye) / np.sqrt(n1)
    mi = np.kron(-np.sin(a1), eye) / np.sqrt(n1)
    m1 = np.block([[mr, -mi], [mi, mr]])
    k1 = np.arange(n1)[None, :, None]
    s2 = (np.arange(n2 // SUBLANES)[:, None, None] * SUBLANES + np.arange(SUBLANES)[None, None, :])
    at = 2.0 * np.pi * (k1 * s2) / S
    twr = np.cos(at).reshape(n2 // SUBLANES, n1 * SUBLANES, 1)
    twi = (-np.sin(at)).reshape(n2 // SUBLANES, n1 * SUBLANES, 1)
    a2 = 2.0 * np.pi * np.outer(np.arange(n2), np.arange(n2)) / n2
    c2 = (np.cos(a2) / np.sqrt(n2))[:, None, None, :] * eye[None, :, :, None]
    s2m = (np.sin(a2) / np.sqrt(n2))[:, None, None, :] * eye[None, :, :, None]
    m2 = np.concatenate([c2.reshape(n2 * SUBLANES, n2 * SUBLANES),
                         s2m.reshape(n2 * SUBLANES, n2 * SUBLANES)], axis=1)
    c2s2 = np.concatenate([np.cos(a2), np.sin(a2)], axis=1) / np.sqrt(n2)
    return dict(chan=jnp.asarray(chan, BF16), m1=jnp.asarray(m1, BF16),
                twr=jnp.asarray(twr, F32), twi=jnp.asarray(twi, F32), m2=jnp.asarray(m2, BF16),
                c2=jnp.asarray(c2s2, BF16))


def _four1_kernel(fa_ref, chan_ref, m1_ref, twr_ref, twi_ref, br_ref, bi_ref):
    n1, sb, _ = fa_ref.shape
    rows = n1 * SUBLANES
    for i in range(sb // SUBLANES):
        rs = slice(i * SUBLANES, (i + 1) * SUBLANES)
        x = fa_ref[:, rs, :].reshape(rows, GROUP_W).astype(BF16)
        zr, zi = [], []
        for g in range(N_FOURIER_GROUPS):
            z = jnp.dot(x[:, g * FOURIER_W:(g + 1) * FOURIER_W], chan_ref[...],
                        preferred_element_type=F32)
            zr.append(z[:, :FOURIER_W])
            zi.append(z[:, FOURIER_W:])
        zcat = jnp.concatenate([jnp.concatenate(zr, axis=1), jnp.concatenate(zi, axis=1)],
                               axis=0).astype(BF16)
        a = jnp.dot(m1_ref[...], zcat, preferred_element_type=F32)
        ar, ai = a[:rows], a[rows:]
        twr, twi = twr_ref[i], twi_ref[i]
        br_ref[:, rs, :] = (ar * twr - ai * twi).reshape(n1, SUBLANES, GROUP_W)
        bi_ref[:, rs, :] = (ar * twi + ai * twr).reshape(n1, SUBLANES, GROUP_W)


def _four2_kernel(br_ref, bi_ref, m2_ref, wf_ref, fg_ref, o_ref):
    _, n2, _ = br_ref.shape
    rows = SUBLANES * n2
    bcat = jnp.concatenate([br_ref[...].reshape(rows, GROUP_W), bi_ref[...].reshape(rows, GROUP_W)],
                           axis=0).astype(BF16)
    f = jnp.dot(m2_ref[...], bcat, preferred_element_type=F32).astype(BF16)
    gate = _silu(fg_ref[...].reshape(rows, GROUP_W))
    ys = [jnp.dot(f[:, g * FOURIER_W:(g + 1) * FOURIER_W], wf_ref[g], preferred_element_type=F32)
          for g in range(N_FOURIER_GROUPS)]
    y = jnp.concatenate(ys, axis=1) * gate
    o_ref[...] = y.reshape(n2, SUBLANES, GROUP_W)


def _four_fused_kernel(fa_ref, fg_ref, chan_ref, m1_ref, twr_ref, twi_ref, c2_ref, wf_ref, o_ref,
                       zr_ref, zi_ref, br_ref, bi_ref, f_ref):
    S = fa_ref.shape[0]
    n2 = DFT_N2
    n1 = S // n2
    lanes = lambda g: slice(g * FOURIER_W, (g + 1) * FOURIER_W)

    x = fa_ref[...].astype(BF16)
    for g in range(N_FOURIER_GROUPS):
        z = jnp.dot(x[:, lanes(g)], chan_ref[...], preferred_element_type=F32)
        zr_ref[:, lanes(g)] = z[:, :FOURIER_W]
        zi_ref[:, lanes(g)] = z[:, FOURIER_W:]

    def stage1(j, carry):
        base = pl.multiple_of(j * SUBLANES, SUBLANES)
        tiles = ([zr_ref[pl.ds(s1 * n2 + base, SUBLANES), :] for s1 in range(n1)]
                 + [zi_ref[pl.ds(s1 * n2 + base, SUBLANES), :] for s1 in range(n1)])
        a = jnp.dot(m1_ref[...], jnp.concatenate(tiles, axis=0).astype(BF16),
                    preferred_element_type=F32)
        ar, ai = a[:n1 * SUBLANES], a[n1 * SUBLANES:]
        twr, twi = twr_ref[j], twi_ref[j]
        br = ar * twr - ai * twi
        bi = ar * twi + ai * twr
        for k1 in range(n1):
            rows = slice(k1 * SUBLANES, (k1 + 1) * SUBLANES)
            br_ref[pl.ds(k1 * n2 + base, SUBLANES), :] = br[rows]
            bi_ref[pl.ds(k1 * n2 + base, SUBLANES), :] = bi[rows]
        return carry

    lax.fori_loop(0, n2 // SUBLANES, stage1, 0, unroll=FOURIER_UNROLL)

    def stage2(k1, carry):
        start = pl.multiple_of(k1 * n2, n2)
        bcat = jnp.concatenate([br_ref[pl.ds(start, n2), :], bi_ref[pl.ds(start, n2), :]],
                               axis=0).astype(BF16)
        f = jnp.dot(c2_ref[...], bcat, preferred_element_type=F32)
        for g in range(N_FOURIER_GROUPS):
            f_ref[g, pl.ds(k1, n2, stride=n1), :] = f[:, lanes(g)]
        return carry

    lax.fori_loop(0, n1, stage2, 0, unroll=FOURIER_UNROLL)

    gate = _silu(fg_ref[...].astype(F32))
    for g in range(N_FOURIER_GROUPS):
        y = jnp.dot(f_ref[g].astype(BF16), wf_ref[g], preferred_element_type=F32)
        o_ref[:, lanes(g)] = (y * gate[:, lanes(g)]).astype(o_ref.dtype)


def _fourier_fused(fa, fg, wf, ft):
    B, S, _ = fa.shape
    n1 = S // DFT_N2
    tok = pl.BlockSpec((None, S, GROUP_W), lambda b: (b, 0, 0))
    return pl.pallas_call(
        _four_fused_kernel,
        grid=(B,),
        in_specs=[tok, tok, _const_spec((FOURIER_W, 2 * FOURIER_W)),
                  _const_spec((2 * n1 * SUBLANES, 2 * n1 * SUBLANES)),
                  _const_spec(ft["twr"].shape), _const_spec(ft["twi"].shape),
                  _const_spec((DFT_N2, 2 * DFT_N2)),
                  _const_spec((N_FOURIER_GROUPS, FOURIER_W, FOURIER_W))],
        out_specs=tok,
        out_shape=jax.ShapeDtypeStruct((B, S, GROUP_W), BF16),
        scratch_shapes=[pltpu.VMEM((S, GROUP_W), F32)] * 4
                       + [pltpu.VMEM((N_FOURIER_GROUPS, S, FOURIER_W), F32)],
        compiler_params=_cparams("parallel"),
        name="four",
    )(fa, fg, ft["chan"], ft["m1"], ft["twr"], ft["twi"], ft["c2"], wf)


def _fourier(fa, fg, wf, ft):
    B, S, _ = fa.shape
    if S <= FOURIER_FUSED_MAX_SEQ:
        return _fourier_fused(fa, fg, wf, ft)
    n2 = DFT_N2
    n1 = S // n2
    sb = 32 if n1 <= 16 else SUBLANES
    nsub = sb // SUBLANES
    fa4 = fa.reshape(B, n1, n2, GROUP_W)
    blk1 = pl.BlockSpec((None, n1, sb, GROUP_W), lambda b, j: (b, 0, j, 0))
    tw = pl.BlockSpec((nsub, n1 * SUBLANES, 1), lambda b, j: (j, 0, 0))
    br, bi = pl.pallas_call(
        _four1_kernel,
        grid=(B, n2 // sb),
        in_specs=[blk1, _const_spec((FOURIER_W, 2 * FOURIER_W)),
                  _const_spec((2 * n1 * SUBLANES, 2 * n1 * SUBLANES)), tw, tw],
        out_specs=[blk1, blk1],
        out_shape=[jax.ShapeDtypeStruct((B, n1, n2, GROUP_W), F32)] * 2,
        compiler_params=_cparams("parallel", "parallel"),
        name="four1",
    )(fa4, ft["chan"], ft["m1"], ft["twr"], ft["twi"])
    blk_in = pl.BlockSpec((None, SUBLANES, n2, GROUP_W), lambda b, j: (b, j, 0, 0))
    blk_out = pl.BlockSpec((None, n2, SUBLANES, GROUP_W), lambda b, j: (b, 0, j, 0))
    o = pl.pallas_call(
        _four2_kernel,
        grid=(B, n1 // SUBLANES),
        in_specs=[blk_in, blk_in, _const_spec((SUBLANES * n2, 2 * SUBLANES * n2)),
                  _const_spec((N_FOURIER_GROUPS, FOURIER_W, FOURIER_W)), blk_out],
        out_specs=blk_out,
        out_shape=jax.ShapeDtypeStruct((B, n2, n1, GROUP_W), F32),
        compiler_params=_cparams("parallel", "parallel"),
        name="four2",
    )(br, bi, ft["m2"], wf, fg.reshape(B, n2, n1, GROUP_W))
    return o.reshape(B, S, GROUP_W)


def _outproj_kernel(ot_ref, ag_ref, of_ref, os_ref, om_ref, x_ref, w_ref, g_ref, y_ref):
    for s in range(ot_ref.shape[0] // OUT_SUB_TILES):
        rows = slice(s * OUT_SUB_TILES * Q_TILE, (s + 1) * OUT_SUB_TILES * Q_TILE)
        tiles = []
        for t in range(s * OUT_SUB_TILES, (s + 1) * OUT_SUB_TILES):
            ot = ot_ref[t].astype(F32)
            slabs = []
            for j in range(GROUP_W // LANES):
                pair = jnp.concatenate([ot[:, (2 * j) * Q_TILE:(2 * j + 1) * Q_TILE],
                                        ot[:, (2 * j + 1) * Q_TILE:(2 * j + 2) * Q_TILE]], axis=0)
                slabs.append(pair.T)
            tiles.append(jnp.concatenate(slabs, axis=1))
        oa = (jnp.concatenate(tiles, axis=0) * _silu(ag_ref[rows, :].astype(F32))).astype(BF16)
        rest = jnp.concatenate([of_ref[rows, :].astype(BF16), os_ref[rows, :], om_ref[rows, :]],
                               axis=1)
        y = jnp.dot(rest, w_ref[GROUP_W:, :], preferred_element_type=F32)
        y = y + jnp.dot(oa, w_ref[:GROUP_W, :], preferred_element_type=F32)
        y_ref[rows, :] = x_ref[rows, :] + _rms(y, g_ref[...])


def _outproj(ot, ag, of, osg, om, x, w, g):
    B, S, _ = x.shape
    tm = TOKEN_TILE
    tok = lambda wd: pl.BlockSpec((None, tm, wd), lambda b, i: (b, i, 0))
    return pl.pallas_call(
        _outproj_kernel,
        grid=(B, S // tm),
        in_specs=[pl.BlockSpec((None, tm // Q_TILE, HEAD_DIM, Q_ROWS), lambda b, i: (b, i, 0, 0)),
                  tok(GROUP_W), tok(GROUP_W), tok(GROUP_W), tok(GROUP_W), tok(D_MODEL),
                  _const_spec((4 * GROUP_W, D_MODEL)), _const_spec((1, D_MODEL))],
        out_specs=tok(D_MODEL),
        out_shape=jax.ShapeDtypeStruct((B, S, D_MODEL), F32),
        compiler_params=_cparams("parallel", "parallel"),
        name="outproj",
    )(ot, ag, of, osg, om, x, w, g)


def _rope_tables(S):
    rows = S // GRID_W
    row = jnp.broadcast_to(jnp.arange(rows, dtype=F32)[:, None], (rows, GRID_W)).reshape(S)
    col = jnp.broadcast_to(jnp.arange(GRID_W, dtype=F32)[None, :], (rows, GRID_W)).reshape(S)
    inv = ROPE_THETA ** (-jnp.arange(ROPE_PAIRS, dtype=F32) / ROPE_PAIRS)
    ang = jnp.stack([row[:, None] * inv, col[:, None] * inv], axis=1)
    cos, sin = jnp.cos(ang), jnp.sin(ang)
    zero = jnp.zeros((S, ROPE_PAIRS), F32)
    two = lambda parts: jnp.tile(jnp.concatenate(parts, axis=-1), (1, LANES // HEAD_DIM))
    return dict(cos=two([cos[:, 0], cos[:, 0], cos[:, 1], cos[:, 1]]),
                sina=two([zero, sin[:, 0], zero, sin[:, 1]]),
                sinb=two([-sin[:, 0], zero, -sin[:, 1], zero]))


def _layer_weights(l, pre_norm_g, w_in, q_norm_g, k_norm_g, w_fourier, sgu_norm_g, w_spatial,
                   b_spatial, mem_norm_g, w_mem_kv, w_out, post_norm_g):
    return dict(
        pre_g=pre_norm_g[l][None, :],
        w_in=w_in[l].astype(BF16),
        qg=jnp.tile(q_norm_g[l], N_HEADS)[None, :],
        kg=jnp.tile(k_norm_g[l], N_KV_HEADS)[None, :],
        wf=w_fourier[l].astype(BF16),
        vg=sgu_norm_g[l].reshape(1, GROUP_W),
        ws=w_spatial[l].astype(BF16),
        bs=jnp.repeat(b_spatial[l].T, SGU_W, axis=1),
        mem_g=mem_norm_g[l][None, :],
        w_mem_kv=w_mem_kv[l].astype(BF16),
        w_out=w_out[l].astype(BF16),
        post_g=post_norm_g[l][None, :],
    )


def _trunk(x, mem, layers):
    S = x.shape[1]
    tabs = _rope_tables(S)
    ids = np.arange(GROUP_W) // HEAD_DIM
    tabs["bd"] = jnp.asarray(ids[:, None] == ids[None, :], BF16)
    ft = _dft_tables(S)
    for lw in layers:
        kv = _memkv(mem, lw["mem_g"], lw["w_mem_kv"])
        qt, k, vt, ag, fa, fg, osgu, omem = _inproj(x, kv, lw, tabs)
        ot = _attention(qt, k, vt)
        ofour = _fourier(fa, fg, lw["wf"], ft)
        x = _outproj(ot, ag, ofour, osgu, omem, x, lw["w_out"], lw["post_g"])
    return x


def kernel(x_prompt, x_sample, mem_prompt, mem_sample, pre_norm_g, w_in, q_norm_g, k_norm_g,
           w_fourier, sgu_norm_g, w_spatial, b_spatial, mem_norm_g, w_mem_kv, w_out, post_norm_g):
    layers = [_layer_weights(l, pre_norm_g, w_in, q_norm_g, k_norm_g, w_fourier, sgu_norm_g,
                             w_spatial, b_spatial, mem_norm_g, w_mem_kv, w_out, post_norm_g)
              for l in range(DEPTH)]
    return (_trunk(x_prompt, mem_prompt, layers), _trunk(x_sample, mem_sample, layers))
```

```python
import functools

import numpy as np
import jax
import jax.numpy as jnp
from jax import lax
from jax.experimental import pallas as pl
from jax.experimental.pallas import tpu as pltpu

F32 = jnp.float32
BF16 = jnp.bfloat16

D_MODEL = 1024
DEPTH = 2
GRID_W = 64
N_MEM = 256
GROUP_W = 512
HEAD_DIM = 64
N_HEADS = 8
N_KV_HEADS = 2
Q_PER_KV = N_HEADS // N_KV_HEADS
ROPE_PAIRS = HEAD_DIM // 4
ROPE_THETA = 10000.0
N_FOURIER_GROUPS = 4
FOURIER_W = 128
N_SGU_HEADS = 4
SGU_W = 128
CHUNK = 128
N_MEM_HEADS = 4
MEM_HEAD_DIM = 128
EPS = 1e-6
IN_W = 4864
C_AQ, C_AK, C_AV, C_AG, C_FA, C_FG, C_SU, C_SV, C_SG, C_MQ, C_MG = (
    0, 512, 640, 768, 1280, 1792, 2304, 2816, 3328, 3840, 4352)

LANES = 128
SUBLANES = 8
VMEM_LIMIT_BYTES = 56 * 1024 * 1024

TOKEN_TILE = 512
KV_CHUNK = 512
Q_TILE = 128
Q_ROWS = N_HEADS * Q_TILE
DFT_N2 = 128
FOURIER_FUSED_MAX_SEQ = 2048
FOURIER_UNROLL = 4
ATTN_UNROLL = 6
OUT_SUB_TILES = 2
ATTN_UNSHIFTED_MAX_LOG2 = 64.0
ATTN_TILES_PER_STEP = 8
BF16_SUBLANES = 16
VT_ROWS = HEAD_DIM + BF16_SUBLANES
Q_COLS_PER_KV = Q_PER_KV * Q_TILE
Q_SCALE = float(HEAD_DIM ** -0.5 * np.log2(np.e))


def _cparams(*sem):
    return pltpu.CompilerParams(dimension_semantics=sem, vmem_limit_bytes=VMEM_LIMIT_BYTES)


def _const_spec(shape):
    nd = len(shape)
    return pl.BlockSpec(shape, lambda *_: (0,) * nd)


def _silu(g):
    return g / (1.0 + jnp.exp(-g))


def _rms(x, g):
    ms = jnp.mean(x * x, axis=-1, keepdims=True)
    return x * lax.rsqrt(ms + EPS) * g


def _memkv_kernel(mem_ref, g_ref, w_ref, kv_ref):
    h = _rms(mem_ref[...], g_ref[...]).astype(BF16)
    kv_ref[...] = jnp.dot(h, w_ref[...], preferred_element_type=F32).astype(BF16)


def _memkv(mem, g, w):
    B = mem.shape[0]
    return pl.pallas_call(
        _memkv_kernel,
        grid=(B,),
        in_specs=[pl.BlockSpec((None, N_MEM, D_MODEL), lambda b: (b, 0, 0)),
                  _const_spec((1, D_MODEL)),
                  _const_spec((D_MODEL, 2 * GROUP_W))],
        out_specs=pl.BlockSpec((None, N_MEM, 2 * GROUP_W), lambda b: (b, 0, 0)),
        out_shape=jax.ShapeDtypeStruct((B, N_MEM, 2 * GROUP_W), BF16),
        compiler_params=_cparams("parallel"),
        name="memkv",
    )(mem, g, w)


def _head_ssq(x, bd):
    return jnp.dot((x * x).astype(BF16), bd, preferred_element_type=F32)


def _rope(x, cos, sina, sinb):
    return x * cos + pltpu.roll(x, 16, 1) * sina + pltpu.roll(x, LANES - 16, 1) * sinb


def _inproj_kernel(x_ref, pre_g_ref, w_ref, bd_ref, qg_ref, kg_ref, cos_ref, sina_ref, sinb_ref,
                   vg_ref, ws_ref, bs_ref, kv_ref,
                   qt_ref, k_ref, vt_ref, ag_ref, fa_ref, fg_ref, osgu_ref, omem_ref):
    tm = x_ref.shape[0]
    h = _rms(x_ref[...], pre_g_ref[...]).astype(BF16)

    def proj(lo, width):
        return jnp.dot(h, w_ref[:, lo:lo + width], preferred_element_type=F32)

    cos, sina, sinb = cos_ref[...], sina_ref[...], sinb_ref[...]
    nck = tm // CHUNK
    lanes = lambda i: slice(i * LANES, (i + 1) * LANES)

    aq = proj(C_AQ, GROUP_W)
    ak = proj(C_AK, LANES)
    av = proj(C_AV, LANES)
    mq = proj(C_MQ, GROUP_W)
    sv = proj(C_SV, GROUP_W)
    ssq = _head_ssq(aq, bd_ref[...])
    ssk = _head_ssq(ak, bd_ref[0:LANES, 0:LANES])

    mq16 = mq.astype(BF16)
    scores = [lax.dot_general(mq16[:, lanes(hd)], kv_ref[:, lanes(hd)], (((1,), (1,)), ((), ())),
                              preferred_element_type=F32) * (MEM_HEAD_DIM ** -0.5)
              for hd in range(N_MEM_HEADS)]

    su = proj(C_SU, GROUP_W)
    sg = proj(C_SG, GROUP_W)
    mg = proj(C_MG, GROUP_W)

    vhs = [_rms(sv[:, lanes(hd)], vg_ref[:, lanes(hd)]).astype(BF16) for hd in range(N_SGU_HEADS)]
    spcs = [jnp.dot(ws_ref[hd],
                    jnp.concatenate([vhs[hd][c * CHUNK:(c + 1) * CHUNK, :] for c in range(nck)],
                                    axis=1), preferred_element_type=F32)
            for hd in range(N_SGU_HEADS)]

    ag_ref[...] = proj(C_AG, GROUP_W).astype(BF16)
    fa_ref[...] = proj(C_FA, GROUP_W).astype(fa_ref.dtype)
    fg_ref[...] = proj(C_FG, GROUP_W).astype(fg_ref.dtype)

    es = [jnp.exp(s - jnp.max(s, axis=-1, keepdims=True)) for s in scores]
    mgate = _silu(mg)
    for hd in range(N_MEM_HEADS):
        mv = kv_ref[:, GROUP_W + hd * MEM_HEAD_DIM:GROUP_W + (hd + 1) * MEM_HEAD_DIM]
        o = jnp.dot(es[hd].astype(BF16), mv, preferred_element_type=F32)
        o = o / jnp.sum(es[hd], axis=-1, keepdims=True)
        omem_ref[:, lanes(hd)] = (o * mgate[:, lanes(hd)]).astype(BF16)

    qn = aq * lax.rsqrt(ssq * (1.0 / HEAD_DIM) + EPS) * qg_ref[...]
    slabs = [_rope(qn[:, lanes(j)], cos, sina, sinb) * Q_SCALE for j in range(GROUP_W // LANES)]
    low = lax.broadcasted_iota(jnp.int32, (Q_TILE, LANES), 1) < HEAD_DIM
    for t in range(tm // Q_TILE):
        parts = []
        for j, slab in enumerate(slabs):
            sj = slab[t * Q_TILE:(t + 1) * Q_TILE, :]
            rolled = pltpu.roll(sj, HEAD_DIM, 1)
            if j < Q_PER_KV // 2:
                parts += [jnp.where(low, sj, 0.0), jnp.where(low, rolled, 0.0)]
            else:
                parts += [jnp.where(low, 0.0, rolled), jnp.where(low, 0.0, sj)]
        qt_ref[t] = jnp.concatenate(parts, axis=0).T.astype(BF16)

    kn = ak * lax.rsqrt(ssk * (1.0 / HEAD_DIM) + EPS) * kg_ref[...]
    k_ref[...] = _rope(kn, cos, sina, sinb).astype(BF16)
    avt = av.T.astype(BF16)
    for g in range(N_KV_HEADS):
        vt_ref[g * VT_ROWS:g * VT_ROWS + HEAD_DIM, :] = avt[g * HEAD_DIM:(g + 1) * HEAD_DIM, :]
        vt_ref[g * VT_ROWS + HEAD_DIM:(g + 1) * VT_ROWS, :] = jnp.ones((BF16_SUBLANES, tm), BF16)

    gate = _silu(sg)
    for hd in range(N_SGU_HEADS):
        for c in range(nck):
            rows = slice(c * CHUNK, (c + 1) * CHUNK)
            sp = spcs[hd][:, c * SGU_W:(c + 1) * SGU_W] + bs_ref[:, lanes(hd)]
            osgu_ref[rows, lanes(hd)] = (su[rows, lanes(hd)] * sp
                                         * gate[rows, lanes(hd)]).astype(BF16)


def _inproj(x, kv, lw, tabs):
    B, S, _ = x.shape
    tm = TOKEN_TILE
    nt = S // tm
    tok = lambda w: pl.BlockSpec((None, tm, w), lambda b, i: (b, i, 0))
    pos = pl.BlockSpec((tm, LANES), lambda b, i: (i, 0))
    in_specs = [
        tok(D_MODEL),
        _const_spec((1, D_MODEL)),
        _const_spec((D_MODEL, IN_W)),
        _const_spec((GROUP_W, GROUP_W)),
        _const_spec((1, GROUP_W)),
        _const_spec((1, LANES)),
        pos, pos, pos,
        _const_spec((1, GROUP_W)),
        _const_spec((N_SGU_HEADS, CHUNK, CHUNK)),
        _const_spec((CHUNK, GROUP_W)),
        pl.BlockSpec((None, N_MEM, 2 * GROUP_W), lambda b, i: (b, 0, 0)),
    ]
    out_specs = [
        pl.BlockSpec((None, tm // Q_TILE, LANES, Q_ROWS), lambda b, i: (b, i, 0, 0)), tok(LANES),
        pl.BlockSpec((None, None, N_KV_HEADS * VT_ROWS, tm), lambda b, i: (b, i, 0, 0)),
        tok(GROUP_W), tok(GROUP_W), tok(GROUP_W), tok(GROUP_W), tok(GROUP_W),
    ]
    sds = jax.ShapeDtypeStruct
    four_dt = BF16 if S <= FOURIER_FUSED_MAX_SEQ else F32
    out_shape = [
        sds((B, S // Q_TILE, LANES, Q_ROWS), BF16), sds((B, S, LANES), BF16),
        sds((B, nt, N_KV_HEADS * VT_ROWS, tm), BF16),
        sds((B, S, GROUP_W), BF16), sds((B, S, GROUP_W), four_dt), sds((B, S, GROUP_W), four_dt),
        sds((B, S, GROUP_W), BF16), sds((B, S, GROUP_W), BF16),
    ]
    return pl.pallas_call(
        _inproj_kernel,
        grid=(B, nt),
        in_specs=in_specs, out_specs=out_specs, out_shape=out_shape,
        compiler_params=_cparams("parallel", "parallel"),
        name="inproj",
    )(x, lw["pre_g"], lw["w_in"], tabs["bd"], lw["qg"], lw["kg"],
      tabs["cos"], tabs["sina"], tabs["sinb"], lw["vg"], lw["ws"], lw["bs"], kv)


def _attn_kernel(qt_ref, k_ref, vt_ref, ot_ref, *scratch, bounded):
    ntiles = qt_ref.shape[0]
    nchunks, _, tk = vt_ref.shape
    neg_inf = jnp.full((1, qt_ref.shape[2]), -jnp.inf, F32)

    if bounded:
        p_ref, acc_ref = scratch
    else:
        st_ref, mx_ref, acc_ref = scratch

    def scores(t, c, slot):
        start = c * tk if isinstance(c, int) else pl.multiple_of(c * tk, tk)
        st = jnp.dot(k_ref[pl.ds(start, tk), :], qt_ref[t], preferred_element_type=F32)
        if bounded:
            p_ref[slot] = jnp.exp2(st).astype(BF16)
        else:
            st_ref[slot] = st
            mx_ref[slot] = jnp.max(st, axis=0, keepdims=True)

    def softmax_pv(t, c, slot, m_old, first=False):
        if bounded:
            p, m_new, alpha = p_ref[slot], m_old, None
        else:
            m_new = mx_ref[slot]
            if not first:
                m_new = jnp.maximum(m_old, m_new)
            p = jnp.exp2(st_ref[slot] - m_new).astype(BF16)
            alpha = None if first else jnp.exp2(m_old - m_new)
        for g in range(N_KV_HEADS):
            cols = slice(g * Q_COLS_PER_KV, (g + 1) * Q_COLS_PER_KV)
            pv = jnp.dot(vt_ref[c, g * VT_ROWS:(g + 1) * VT_ROWS, :], p[:, cols],
                         preferred_element_type=F32)
            if first:
                acc_ref[t, g] = pv
            elif bounded:
                acc_ref[t, g] += pv
            else:
                acc_ref[t, g] = alpha[:, cols] * acc_ref[t, g] + pv
        return m_new

    def finish(t):
        for g in range(N_KV_HEADS):
            acc = acc_ref[t, g]
            ot_ref[t, :, g * Q_COLS_PER_KV:(g + 1) * Q_COLS_PER_KV] = (
                acc[:HEAD_DIM] / acc[HEAD_DIM:HEAD_DIM + 1]).astype(BF16)

    if nchunks <= ATTN_UNROLL:
        assert nchunks % 2 == 0

        def tile_steps(t, nxt):
            m = None
            for c in range(nchunks):
                if c + 1 < nchunks:
                    scores(t, c + 1, (c + 1) % 2)
                elif nxt is not None:
                    scores(nxt, 0, 0)
                m = softmax_pv(t, c, c % 2, m, first=(c == 0))
            finish(t)

        scores(0, 0, 0)
        if ntiles > 1:
            def tile(t, carry):
                tile_steps(t, t + 1)
                return carry
            lax.fori_loop(0, ntiles - 1, tile, 0)
        tile_steps(ntiles - 1, None)
    else:
        assert ntiles == 1
        steady = nchunks - 1
        groups = (steady - 1) // ATTN_UNROLL
        peeled = steady - groups * ATTN_UNROLL

        def step(c, slot, m, first=False):
            scores(0, c + 1, 1 - slot)
            return softmax_pv(0, c, slot, m, first)

        scores(0, 0, 0)
        m = neg_inf
        for c in range(peeled):
            m = step(c, c % 2, m, first=(c == 0))

        def group(g, m):
            for u in range(ATTN_UNROLL):
                m = step(peeled + g * ATTN_UNROLL + u, (peeled + u) % 2, m)
            return m
        m = lax.fori_loop(0, groups, group, m)
        softmax_pv(0, nchunks - 1, (nchunks - 1) % 2, m)
        finish(0)


def _attention(qt, k, vt, bounded):
    B, nq = qt.shape[0], qt.shape[1]
    S = k.shape[1]
    tk = vt.shape[3]
    ntiles = min(nq, ATTN_TILES_PER_STEP) if S // tk <= ATTN_UNROLL else 1
    qspec = pl.BlockSpec((None, ntiles, LANES, Q_ROWS), lambda b, i: (b, i, 0, 0))
    if bounded:
        buffers = [pltpu.VMEM((2, tk, Q_ROWS), BF16)]
    else:
        buffers = [pltpu.VMEM((2, tk, Q_ROWS), F32), pltpu.VMEM((2, 1, Q_ROWS), F32)]
    return pl.pallas_call(
        functools.partial(_attn_kernel, bounded=bounded),
        grid=(B, nq // ntiles),
        in_specs=[qspec,
                  pl.BlockSpec((None, S, LANES), lambda b, i: (b, 0, 0)),
                  pl.BlockSpec((None,) + vt.shape[1:], lambda b, i: (b, 0, 0, 0))],
        out_specs=pl.BlockSpec((None, ntiles, HEAD_DIM, Q_ROWS), lambda b, i: (b, i, 0, 0)),
        out_shape=jax.ShapeDtypeStruct((B, nq, HEAD_DIM, Q_ROWS), BF16),
        scratch_shapes=buffers + [pltpu.VMEM((ntiles, N_KV_HEADS, VT_ROWS, Q_COLS_PER_KV), F32)],
        compiler_params=_cparams("parallel", "parallel"),
        name="attn_bounded" if bounded else "attn",
    )(qt, k, vt)


def _dft_tables(S):
    n2 = DFT_N2
    n1 = S // n2
    c = np.arange(FOURIER_W)
    ang = 2.0 * np.pi * np.outer(c, c) / FOURIER_W
    chan = np.concatenate([np.cos(ang), -np.sin(ang)], axis=1) / np.sqrt(FOURIER_W)
    a1 = 2.0 * np.pi * np.outer(np.arange(n1), np.arange(n1)) / n1
    eye = np.eye(SUBLANES)
    mr = np.kron(np.cos(a1), eye) / np.sqrt(n1)
    mi = np.kron(-np.sin(a1), eye) / np.sqrt(n1)
    m1 = np.block([[mr, -mi], [mi, mr]])
    k1 = np.arange(n1)[None, :, None]
    s2 = (np.arange(n2 // SUBLANES)[:, None, None] * SUBLANES + np.arange(SUBLANES)[None, None, :])
    at = 2.0 * np.pi * (k1 * s2) / S
    twr = np.cos(at).reshape(n2 // SUBLANES, n1 * SUBLANES, 1)
    twi = (-np.sin(at)).reshape(n2 // SUBLANES, n1 * SUBLANES, 1)
    a2 = 2.0 * np.pi * np.outer(np.arange(n2), np.arange(n2)) / n2
    c2 = (np.cos(a2) / np.sqrt(n2))[:, None, None, :] * eye[None, :, :, None]
    s2m = (np.sin(a2) / np.sqrt(n2))[:, None, None, :] * eye[None, :, :, None]
    m2 = np.concatenate([c2.reshape(n2 * SUBLANES, n2 * SUBLANES),
                         s2m.reshape(n2 * SUBLANES, n2 * SUBLANES)], axis=1)
    c2s2 = np.concatenate([np.cos(a2), np.sin(a2)], axis=1) / np.sqrt(n2)
    return dict(chan=jnp.asarray(chan, BF16), m1=jnp.asarray(m1, BF16),
                twr=jnp.asarray(twr, F32), twi=jnp.asarray(twi, F32), m2=jnp.asarray(m2, BF16),
                c2=jnp.asarray(c2s2, BF16))


def _four1_kernel(fa_ref, chan_ref, m1_ref, twr_ref, twi_ref, br_ref, bi_ref):
    n1, sb, _ = fa_ref.shape
    rows = n1 * SUBLANES
    for i in range(sb // SUBLANES):
        rs = slice(i * SUBLANES, (i + 1) * SUBLANES)
        x = fa_ref[:, rs, :].reshape(rows, GROUP_W).astype(BF16)
        zr, zi = [], []
        for g in range(N_FOURIER_GROUPS):
            z = jnp.dot(x[:, g * FOURIER_W:(g + 1) * FOURIER_W], chan_ref[...],
                        preferred_element_type=F32)
            zr.append(z[:, :FOURIER_W])
            zi.append(z[:, FOURIER_W:])
        zcat = jnp.concatenate([jnp.concatenate(zr, axis=1), jnp.concatenate(zi, axis=1)],
                               axis=0).astype(BF16)
        a = jnp.dot(m1_ref[...], zcat, preferred_element_type=F32)
        ar, ai = a[:rows], a[rows:]
        twr, twi = twr_ref[i], twi_ref[i]
        br_ref[:, rs, :] = (ar * twr - ai * twi).reshape(n1, SUBLANES, GROUP_W)
        bi_ref[:, rs, :] = (ar * twi + ai * twr).reshape(n1, SUBLANES, GROUP_W)


def _four2_kernel(br_ref, bi_ref, m2_ref, wf_ref, fg_ref, o_ref):
    _, n2, _ = br_ref.shape
    rows = SUBLANES * n2
    bcat = jnp.concatenate([br_ref[...].reshape(rows, GROUP_W), bi_ref[...].reshape(rows, GROUP_W)],
                           axis=0).astype(BF16)
    f = jnp.dot(m2_ref[...], bcat, preferred_element_type=F32).astype(BF16)
    gate = _silu(fg_ref[...].reshape(rows, GROUP_W))
    ys = [jnp.dot(f[:, g * FOURIER_W:(g + 1) * FOURIER_W], wf_ref[g], preferred_element_type=F32)
          for g in range(N_FOURIER_GROUPS)]
    y = jnp.concatenate(ys, axis=1) * gate
    o_ref[...] = y.reshape(n2, SUBLANES, GROUP_W)


def _four_fused_kernel(fa_ref, fg_ref, chan_ref, m1_ref, twr_ref, twi_ref, c2_ref, wf_ref, o_ref,
                       zr_ref, zi_ref, br_ref, bi_ref, f_ref):
    S = fa_ref.shape[0]
    n2 = DFT_N2
    n1 = S // n2
    lanes = lambda g: slice(g * FOURIER_W, (g + 1) * FOURIER_W)

    x = fa_ref[...].astype(BF16)
    for g in range(N_FOURIER_GROUPS):
        z = jnp.dot(x[:, lanes(g)], chan_ref[...], preferred_element_type=F32)
        zr_ref[:, lanes(g)] = z[:, :FOURIER_W]
        zi_ref[:, lanes(g)] = z[:, FOURIER_W:]

    def stage1(j, carry):
        base = pl.multiple_of(j * SUBLANES, SUBLANES)
        tiles = ([zr_ref[pl.ds(s1 * n2 + base, SUBLANES), :] for s1 in range(n1)]
                 + [zi_ref[pl.ds(s1 * n2 + base, SUBLANES), :] for s1 in range(n1)])
        a = jnp.dot(m1_ref[...], jnp.concatenate(tiles, axis=0).astype(BF16),
                    preferred_element_type=F32)
        ar, ai = a[:n1 * SUBLANES], a[n1 * SUBLANES:]
        twr, twi = twr_ref[j], twi_ref[j]
        br = ar * twr - ai * twi
        bi = ar * twi + ai * twr
        for k1 in range(n1):
            rows = slice(k1 * SUBLANES, (k1 + 1) * SUBLANES)
            br_ref[pl.ds(k1 * n2 + base, SUBLANES), :] = br[rows]
            bi_ref[pl.ds(k1 * n2 + base, SUBLANES), :] = bi[rows]
        return carry

    lax.fori_loop(0, n2 // SUBLANES, stage1, 0, unroll=FOURIER_UNROLL)

    def stage2(k1, carry):
        start = pl.multiple_of(k1 * n2, n2)
        bcat = jnp.concatenate([br_ref[pl.ds(start, n2), :], bi_ref[pl.ds(start, n2), :]],
                               axis=0).astype(BF16)
        f = jnp.dot(c2_ref[...], bcat, preferred_element_type=F32)
        for g in range(N_FOURIER_GROUPS):
            f_ref[g, pl.ds(k1, n2, stride=n1), :] = f[:, lanes(g)]
        return carry

    lax.fori_loop(0, n1, stage2, 0, unroll=FOURIER_UNROLL)

    gate = _silu(fg_ref[...].astype(F32))
    for g in range(N_FOURIER_GROUPS):
        y = jnp.dot(f_ref[g].astype(BF16), wf_ref[g], preferred_element_type=F32)
        o_ref[:, lanes(g)] = (y * gate[:, lanes(g)]).astype(o_ref.dtype)


def _fourier_fused(fa, fg, wf, ft):
    B, S, _ = fa.shape
    n1 = S // DFT_N2
    tok = pl.BlockSpec((None, S, GROUP_W), lambda b: (b, 0, 0))
    return pl.pallas_call(
        _four_fused_kernel,
        grid=(B,),
        in_specs=[tok, tok, _const_spec((FOURIER_W, 2 * FOURIER_W)),
                  _const_spec((2 * n1 * SUBLANES, 2 * n1 * SUBLANES)),
                  _const_spec(ft["twr"].shape), _const_spec(ft["twi"].shape),
                  _const_spec((DFT_N2, 2 * DFT_N2)),
                  _const_spec((N_FOURIER_GROUPS, FOURIER_W, FOURIER_W))],
        out_specs=tok,
        out_shape=jax.ShapeDtypeStruct((B, S, GROUP_W), BF16),
        scratch_shapes=[pltpu.VMEM((S, GROUP_W), F32)] * 4
                       + [pltpu.VMEM((N_FOURIER_GROUPS, S, FOURIER_W), F32)],
        compiler_params=_cparams("parallel"),
        name="four",
    )(fa, fg, ft["chan"], ft["m1"], ft["twr"], ft["twi"], ft["c2"], wf)


def _fourier(fa, fg, wf, ft):
    B, S, _ = fa.shape
    if S <= FOURIER_FUSED_MAX_SEQ:
        return _fourier_fused(fa, fg, wf, ft)
    n2 = DFT_N2
    n1 = S // n2
    sb = 32 if n1 <= 16 else SUBLANES
    nsub = sb // SUBLANES
    fa4 = fa.reshape(B, n1, n2, GROUP_W)
    blk1 = pl.BlockSpec((None, n1, sb, GROUP_W), lambda b, j: (b, 0, j, 0))
    tw = pl.BlockSpec((nsub, n1 * SUBLANES, 1), lambda b, j: (j, 0, 0))
    br, bi = pl.pallas_call(
        _four1_kernel,
        grid=(B, n2 // sb),
        in_specs=[blk1, _const_spec((FOURIER_W, 2 * FOURIER_W)),
                  _const_spec((2 * n1 * SUBLANES, 2 * n1 * SUBLANES)), tw, tw],
        out_specs=[blk1, blk1],
        out_shape=[jax.ShapeDtypeStruct((B, n1, n2, GROUP_W), F32)] * 2,
        compiler_params=_cparams("parallel", "parallel"),
        name="four1",
    )(fa4, ft["chan"], ft["m1"], ft["twr"], ft["twi"])
    blk_in = pl.BlockSpec((None, SUBLANES, n2, GROUP_W), lambda b, j: (b, j, 0, 0))
    blk_out = pl.BlockSpec((None, n2, SUBLANES, GROUP_W), lambda b, j: (b, 0, j, 0))
    o = pl.pallas_call(
        _four2_kernel,
        grid=(B, n1 // SUBLANES),
        in_specs=[blk_in, blk_in, _const_spec((SUBLANES * n2, 2 * SUBLANES * n2)),
                  _const_spec((N_FOURIER_GROUPS, FOURIER_W, FOURIER_W)), blk_out],
        out_specs=blk_out,
        out_shape=jax.ShapeDtypeStruct((B, n2, n1, GROUP_W), F32),
        compiler_params=_cparams("parallel", "parallel"),
        name="four2",
    )(br, bi, ft["m2"], wf, fg.reshape(B, n2, n1, GROUP_W))
    return o.reshape(B, S, GROUP_W)


def _outproj_kernel(ot_ref, ag_ref, of_ref, os_ref, om_ref, x_ref, w_ref, g_ref, y_ref):
    for s in range(ot_ref.shape[0] // OUT_SUB_TILES):
        rows = slice(s * OUT_SUB_TILES * Q_TILE, (s + 1) * OUT_SUB_TILES * Q_TILE)
        tiles = []
        for t in range(s * OUT_SUB_TILES, (s + 1) * OUT_SUB_TILES):
            ot = ot_ref[t].astype(F32)
            slabs = []
            for j in range(GROUP_W // LANES):
                pair = jnp.concatenate([ot[:, (2 * j) * Q_TILE:(2 * j + 1) * Q_TILE],
                                        ot[:, (2 * j + 1) * Q_TILE:(2 * j + 2) * Q_TILE]], axis=0)
                slabs.append(pair.T)
            tiles.append(jnp.concatenate(slabs, axis=1))
        oa = (jnp.concatenate(tiles, axis=0) * _silu(ag_ref[rows, :].astype(F32))).astype(BF16)
        rest = jnp.concatenate([of_ref[rows, :].astype(BF16), os_ref[rows, :], om_ref[rows, :]],
                               axis=1)
        y = jnp.dot(rest, w_ref[GROUP_W:, :], preferred_element_type=F32)
        y = y + jnp.dot(oa, w_ref[:GROUP_W, :], preferred_element_type=F32)
        y_ref[rows, :] = x_ref[rows, :] + _rms(y, g_ref[...])


def _outproj(ot, ag, of, osg, om, x, w, g):
    B, S, _ = x.shape
    tm = TOKEN_TILE
    tok = lambda wd: pl.BlockSpec((None, tm, wd), lambda b, i: (b, i, 0))
    return pl.pallas_call(
        _outproj_kernel,
        grid=(B, S // tm),
        in_specs=[pl.BlockSpec((None, tm // Q_TILE, HEAD_DIM, Q_ROWS), lambda b, i: (b, i, 0, 0)),
                  tok(GROUP_W), tok(GROUP_W), tok(GROUP_W), tok(GROUP_W), tok(D_MODEL),
                  _const_spec((4 * GROUP_W, D_MODEL)), _const_spec((1, D_MODEL))],
        out_specs=tok(D_MODEL),
        out_shape=jax.ShapeDtypeStruct((B, S, D_MODEL), F32),
        compiler_params=_cparams("parallel", "parallel"),
        name="outproj",
    )(ot, ag, of, osg, om, x, w, g)


def _rope_tables(S):
    rows = S // GRID_W
    row = jnp.broadcast_to(jnp.arange(rows, dtype=F32)[:, None], (rows, GRID_W)).reshape(S)
    col = jnp.broadcast_to(jnp.arange(GRID_W, dtype=F32)[None, :], (rows, GRID_W)).reshape(S)
    inv = ROPE_THETA ** (-jnp.arange(ROPE_PAIRS, dtype=F32) / ROPE_PAIRS)
    ang = jnp.stack([row[:, None] * inv, col[:, None] * inv], axis=1)
    cos, sin = jnp.cos(ang), jnp.sin(ang)
    zero = jnp.zeros((S, ROPE_PAIRS), F32)
    two = lambda parts: jnp.tile(jnp.concatenate(parts, axis=-1), (1, LANES // HEAD_DIM))
    return dict(cos=two([cos[:, 0], cos[:, 0], cos[:, 1], cos[:, 1]]),
                sina=two([zero, sin[:, 0], zero, sin[:, 1]]),
                sinb=two([-sin[:, 0], zero, -sin[:, 1], zero]))


def _layer_weights(l, pre_norm_g, w_in, q_norm_g, k_norm_g, w_fourier, sgu_norm_g, w_spatial,
                   b_spatial, mem_norm_g, w_mem_kv, w_out, post_norm_g):
    return dict(
        pre_g=pre_norm_g[l][None, :],
        w_in=w_in[l].astype(BF16),
        qg=jnp.tile(q_norm_g[l], N_HEADS)[None, :],
        kg=jnp.tile(k_norm_g[l], N_KV_HEADS)[None, :],
        wf=w_fourier[l].astype(BF16),
        vg=sgu_norm_g[l].reshape(1, GROUP_W),
        ws=w_spatial[l].astype(BF16),
        bs=jnp.repeat(b_spatial[l].T, SGU_W, axis=1),
        mem_g=mem_norm_g[l][None, :],
        w_mem_kv=w_mem_kv[l].astype(BF16),
        w_out=w_out[l].astype(BF16),
        post_g=post_norm_g[l][None, :],
        score_bound=(HEAD_DIM * Q_SCALE * jnp.max(jnp.abs(q_norm_g[l]))
                     * jnp.max(jnp.abs(k_norm_g[l]))),
    )


def _trunk(x, mem, layers):
    S = x.shape[1]
    tabs = _rope_tables(S)
    ids = np.arange(GROUP_W) // HEAD_DIM
    tabs["bd"] = jnp.asarray(ids[:, None] == ids[None, :], BF16)
    ft = _dft_tables(S)
    for lw in layers:
        kv = _memkv(mem, lw["mem_g"], lw["w_mem_kv"])
        qt, k, vt, ag, fa, fg, osgu, omem = _inproj(x, kv, lw, tabs)
        ot = lax.cond(lw["score_bound"] < ATTN_UNSHIFTED_MAX_LOG2,
                      functools.partial(_attention, bounded=True),
                      functools.partial(_attention, bounded=False), qt, k, vt)
        ofour = _fourier(fa, fg, lw["wf"], ft)
        x = _outproj(ot, ag, ofour, osgu, omem, x, lw["w_out"], lw["post_g"])
    return x


def kernel(x_prompt, x_sample, mem_prompt, mem_sample, pre_norm_g, w_in, q_norm_g, k_norm_g,
           w_fourier, sgu_norm_g, w_spatial, b_spatial, mem_norm_g, w_mem_kv, w_out, post_norm_g):
    layers = [_layer_weights(l, pre_norm_g, w_in, q_norm_g, k_norm_g, w_fourier, sgu_norm_g,
                             w_spatial, b_spatial, mem_norm_g, w_mem_kv, w_out, post_norm_g)
              for l in range(DEPTH)]
    return (_trunk(x_prompt, mem_prompt, layers), _trunk(x_sample, mem_sample, layers))
```

```python
import functools

import numpy as np
import jax
import jax.numpy as jnp
from jax import lax
from jax.experimental import pallas as pl
from jax.experimental.pallas import tpu as pltpu

F32 = jnp.float32
BF16 = jnp.bfloat16

D_MODEL = 1024
DEPTH = 2
GRID_W = 64
N_MEM = 256
GROUP_W = 512
HEAD_DIM = 64
N_HEADS = 8
N_KV_HEADS = 2
Q_PER_KV = N_HEADS // N_KV_HEADS
ROPE_PAIRS = HEAD_DIM // 4
ROPE_THETA = 10000.0
N_FOURIER_GROUPS = 4
FOURIER_W = 128
N_SGU_HEADS = 4
SGU_W = 128
CHUNK = 128
N_MEM_HEADS = 4
MEM_HEAD_DIM = 128
EPS = 1e-6
IN_W = 4864
C_AQ, C_AK, C_AV, C_AG, C_FA, C_FG, C_SU, C_SV, C_SG, C_MQ, C_MG = (
    0, 512, 640, 768, 1280, 1792, 2304, 2816, 3328, 3840, 4352)

LANES = 128
SUBLANES = 8
MXU_COLS = 256
VMEM_LIMIT_BYTES = 56 * 1024 * 1024

TOKEN_TILE = 512
KV_CHUNK = 512
Q_TILE = 128
Q_ROWS = N_HEADS * Q_TILE
DFT_N2 = 128
FOURIER_FUSED_MAX_SEQ = 2048
FOURIER_UNROLL = 4
ATTN_UNROLL = 6
OUT_SUB_TILES = 2
ATTN_UNSHIFTED_MAX_LOG2 = 64.0
ATTN_TILES_PER_STEP = 8
BF16_SUBLANES = 16
VT_ROWS = HEAD_DIM + BF16_SUBLANES
Q_COLS_PER_KV = Q_PER_KV * Q_TILE
Q_SCALE = float(HEAD_DIM ** -0.5 * np.log2(np.e))


def _cparams(*sem):
    return pltpu.CompilerParams(dimension_semantics=sem, vmem_limit_bytes=VMEM_LIMIT_BYTES)


def _const_spec(shape):
    nd = len(shape)
    return pl.BlockSpec(shape, lambda *_: (0,) * nd)


def _silu(g):
    return g / (1.0 + jnp.exp(-g))


def _rms(x, g):
    ms = jnp.mean(x * x, axis=-1, keepdims=True)
    return x * lax.rsqrt(ms + EPS) * g


def _memkv_kernel(mem_ref, g_ref, w_ref, kv_ref):
    h = _rms(mem_ref[...], g_ref[...]).astype(BF16)
    kv_ref[...] = jnp.dot(h, w_ref[...], preferred_element_type=F32).astype(BF16)


def _memkv(mem, g, w):
    B = mem.shape[0]
    return pl.pallas_call(
        _memkv_kernel,
        grid=(B,),
        in_specs=[pl.BlockSpec((None, N_MEM, D_MODEL), lambda b: (b, 0, 0)),
                  _const_spec((1, D_MODEL)),
                  _const_spec((D_MODEL, 2 * GROUP_W))],
        out_specs=pl.BlockSpec((None, N_MEM, 2 * GROUP_W), lambda b: (b, 0, 0)),
        out_shape=jax.ShapeDtypeStruct((B, N_MEM, 2 * GROUP_W), BF16),
        compiler_params=_cparams("parallel"),
        name="memkv",
    )(mem, g, w)


def _head_ssq(x, bd):
    return jnp.dot((x * x).astype(BF16), bd, preferred_element_type=F32)


def _rope(x, cos, sina, sinb):
    return x * cos + pltpu.roll(x, 16, 1) * sina + pltpu.roll(x, LANES - 16, 1) * sinb


def _inproj_kernel(x_ref, pre_g_ref, w_ref, bd_ref, qg_ref, kg_ref, cos_ref, sina_ref, sinb_ref,
                   vg_ref, ws_ref, bs_ref, kv_ref,
                   qt_ref, k_ref, vt_ref, ag_ref, fa_ref, fg_ref, osgu_ref, omem_ref):
    tm = x_ref.shape[0]
    h = _rms(x_ref[...], pre_g_ref[...]).astype(BF16)

    def proj(lo, width):
        return jnp.dot(h, w_ref[:, lo:lo + width], preferred_element_type=F32)

    cos, sina, sinb = cos_ref[...], sina_ref[...], sinb_ref[...]
    nck = tm // CHUNK
    lanes = lambda i: slice(i * LANES, (i + 1) * LANES)

    aq = proj(C_AQ, GROUP_W)
    ak = proj(C_AK, LANES)
    av = proj(C_AV, LANES)
    mq = proj(C_MQ, GROUP_W)
    sv = proj(C_SV, GROUP_W)
    ssq = _head_ssq(aq, bd_ref[...])
    ssk = _head_ssq(ak, bd_ref[0:LANES, 0:LANES])

    mq16 = mq.astype(BF16)
    scores = [lax.dot_general(mq16[:, lanes(hd)], kv_ref[:, lanes(hd)], (((1,), (1,)), ((), ())),
                              preferred_element_type=F32) * (MEM_HEAD_DIM ** -0.5)
              for hd in range(N_MEM_HEADS)]

    su = proj(C_SU, GROUP_W)
    sg = proj(C_SG, GROUP_W)
    mg = proj(C_MG, GROUP_W)

    vhs = [_rms(sv[:, lanes(hd)], vg_ref[:, lanes(hd)]).astype(BF16) for hd in range(N_SGU_HEADS)]
    spcs = [jnp.dot(ws_ref[hd],
                    jnp.concatenate([vhs[hd][c * CHUNK:(c + 1) * CHUNK, :] for c in range(nck)],
                                    axis=1), preferred_element_type=F32)
            for hd in range(N_SGU_HEADS)]

    ag_ref[...] = proj(C_AG, GROUP_W).astype(BF16)
    fa_ref[...] = proj(C_FA, GROUP_W).astype(fa_ref.dtype)
    fg_ref[...] = proj(C_FG, GROUP_W).astype(fg_ref.dtype)

    es = [jnp.exp(s - jnp.max(s, axis=-1, keepdims=True)) for s in scores]
    mgate = _silu(mg)
    for hd in range(N_MEM_HEADS):
        mv = kv_ref[:, GROUP_W + hd * MEM_HEAD_DIM:GROUP_W + (hd + 1) * MEM_HEAD_DIM]
        o = jnp.dot(es[hd].astype(BF16), mv, preferred_element_type=F32)
        o = o / jnp.sum(es[hd], axis=-1, keepdims=True)
        omem_ref[:, lanes(hd)] = (o * mgate[:, lanes(hd)]).astype(BF16)

    qn = aq * lax.rsqrt(ssq * (1.0 / HEAD_DIM) + EPS) * qg_ref[...]
    slabs = [_rope(qn[:, lanes(j)], cos, sina, sinb) * Q_SCALE for j in range(GROUP_W // LANES)]
    low = lax.broadcasted_iota(jnp.int32, (Q_TILE, LANES), 1) < HEAD_DIM
    for t in range(tm // Q_TILE):
        parts = []
        for j, slab in enumerate(slabs):
            sj = slab[t * Q_TILE:(t + 1) * Q_TILE, :]
            rolled = pltpu.roll(sj, HEAD_DIM, 1)
            if j < Q_PER_KV // 2:
                parts += [jnp.where(low, sj, 0.0), jnp.where(low, rolled, 0.0)]
            else:
                parts += [jnp.where(low, 0.0, rolled), jnp.where(low, 0.0, sj)]
        qt_ref[t] = jnp.concatenate(parts, axis=0).T.astype(BF16)

    kn = ak * lax.rsqrt(ssk * (1.0 / HEAD_DIM) + EPS) * kg_ref[...]
    k_ref[...] = _rope(kn, cos, sina, sinb).astype(BF16)
    avt = av.T.astype(BF16)
    for g in range(N_KV_HEADS):
        vt_ref[g * VT_ROWS:g * VT_ROWS + HEAD_DIM, :] = avt[g * HEAD_DIM:(g + 1) * HEAD_DIM, :]
        vt_ref[g * VT_ROWS + HEAD_DIM:(g + 1) * VT_ROWS, :] = jnp.ones((BF16_SUBLANES, tm), BF16)

    gate = _silu(sg)
    for hd in range(N_SGU_HEADS):
        for c in range(nck):
            rows = slice(c * CHUNK, (c + 1) * CHUNK)
            sp = spcs[hd][:, c * SGU_W:(c + 1) * SGU_W] + bs_ref[:, lanes(hd)]
            osgu_ref[rows, lanes(hd)] = (su[rows, lanes(hd)] * sp
                                         * gate[rows, lanes(hd)]).astype(BF16)


def _inproj(x, kv, lw, tabs):
    B, S, _ = x.shape
    tm = TOKEN_TILE
    nt = S // tm
    tok = lambda w: pl.BlockSpec((None, tm, w), lambda b, i: (b, i, 0))
    pos = pl.BlockSpec((tm, LANES), lambda b, i: (i, 0))
    in_specs = [
        tok(D_MODEL),
        _const_spec((1, D_MODEL)),
        _const_spec((D_MODEL, IN_W)),
        _const_spec((GROUP_W, GROUP_W)),
        _const_spec((1, GROUP_W)),
        _const_spec((1, LANES)),
        pos, pos, pos,
        _const_spec((1, GROUP_W)),
        _const_spec((N_SGU_HEADS, CHUNK, CHUNK)),
        _const_spec((CHUNK, GROUP_W)),
        pl.BlockSpec((None, N_MEM, 2 * GROUP_W), lambda b, i: (b, 0, 0)),
    ]
    out_specs = [
        pl.BlockSpec((None, tm // Q_TILE, LANES, Q_ROWS), lambda b, i: (b, i, 0, 0)), tok(LANES),
        pl.BlockSpec((None, None, N_KV_HEADS * VT_ROWS, tm), lambda b, i: (b, i, 0, 0)),
        tok(GROUP_W), tok(GROUP_W), tok(GROUP_W), tok(GROUP_W), tok(GROUP_W),
    ]
    sds = jax.ShapeDtypeStruct
    four_dt = BF16 if S <= FOURIER_FUSED_MAX_SEQ else F32
    out_shape = [
        sds((B, S // Q_TILE, LANES, Q_ROWS), BF16), sds((B, S, LANES), BF16),
        sds((B, nt, N_KV_HEADS * VT_ROWS, tm), BF16),
        sds((B, S, GROUP_W), BF16), sds((B, S, GROUP_W), four_dt), sds((B, S, GROUP_W), four_dt),
        sds((B, S, GROUP_W), BF16), sds((B, S, GROUP_W), BF16),
    ]
    return pl.pallas_call(
        _inproj_kernel,
        grid=(B, nt),
        in_specs=in_specs, out_specs=out_specs, out_shape=out_shape,
        compiler_params=_cparams("parallel", "parallel"),
        name="inproj",
    )(x, lw["pre_g"], lw["w_in"], tabs["bd"], lw["qg"], lw["kg"],
      tabs["cos"], tabs["sina"], tabs["sinb"], lw["vg"], lw["ws"], lw["bs"], kv)


def _attn_kernel(qt_ref, k_ref, vt_ref, ot_ref, *scratch, bounded):
    ntiles = qt_ref.shape[0]
    nchunks, _, tk = vt_ref.shape
    neg_inf = jnp.full((1, qt_ref.shape[2]), -jnp.inf, F32)

    if bounded:
        p_ref, acc_ref = scratch
    else:
        st_ref, mx_ref, acc_ref = scratch

    def scores(t, c, slot):
        start = c * tk if isinstance(c, int) else pl.multiple_of(c * tk, tk)
        st = jnp.dot(k_ref[pl.ds(start, tk), :], qt_ref[t], preferred_element_type=F32)
        if bounded:
            p_ref[slot] = jnp.exp2(st).astype(BF16)
        else:
            st_ref[slot] = st
            mx_ref[slot] = jnp.max(st, axis=0, keepdims=True)

    def softmax_pv(t, c, slot, m_old, first=False):
        if bounded:
            p, m_new, alpha = p_ref[slot], m_old, None
        else:
            m_new = mx_ref[slot]
            if not first:
                m_new = jnp.maximum(m_old, m_new)
            p = jnp.exp2(st_ref[slot] - m_new).astype(BF16)
            alpha = None if first else jnp.exp2(m_old - m_new)
        for g in range(N_KV_HEADS):
            cols = slice(g * Q_COLS_PER_KV, (g + 1) * Q_COLS_PER_KV)
            pv = jnp.dot(vt_ref[c, g * VT_ROWS:(g + 1) * VT_ROWS, :], p[:, cols],
                         preferred_element_type=F32)
            if first:
                acc_ref[t, g] = pv
            elif bounded:
                acc_ref[t, g] += pv
            else:
                acc_ref[t, g] = alpha[:, cols] * acc_ref[t, g] + pv
        return m_new

    def step_pair(nxt, t, c, slot, m_old, first=False):
        if not bounded or nxt is None:
            if nxt is not None:
                scores(*nxt)
            return softmax_pv(t, c, slot, m_old, first)
        tn, cn, sn = nxt
        start = cn * tk if isinstance(cn, int) else pl.multiple_of(cn * tk, tk)
        kc = k_ref[pl.ds(start, tk), :]
        for i in range(Q_ROWS // MXU_COLS):
            cols = slice(i * MXU_COLS, (i + 1) * MXU_COLS)
            st = jnp.dot(kc, qt_ref[tn, :, cols], preferred_element_type=F32)
            p_ref[sn, :, cols] = jnp.exp2(st).astype(BF16)
            g, off = divmod(i * MXU_COLS, Q_COLS_PER_KV)
            pv = jnp.dot(vt_ref[c, g * VT_ROWS:(g + 1) * VT_ROWS, :], p_ref[slot, :, cols],
                         preferred_element_type=F32)
            if first:
                acc_ref[t, g, :, off:off + MXU_COLS] = pv
            else:
                acc_ref[t, g, :, off:off + MXU_COLS] += pv
        return m_old

    def finish(t):
        for g in range(N_KV_HEADS):
            acc = acc_ref[t, g]
            ot_ref[t, :, g * Q_COLS_PER_KV:(g + 1) * Q_COLS_PER_KV] = (
                acc[:HEAD_DIM] / acc[HEAD_DIM:HEAD_DIM + 1]).astype(BF16)

    if nchunks <= ATTN_UNROLL:
        assert nchunks % 2 == 0

        def tile_steps(t, nxt):
            m = None
            for c in range(nchunks):
                if c + 1 < nchunks:
                    ahead = (t, c + 1, (c + 1) % 2)
                else:
                    ahead = None if nxt is None else (nxt, 0, 0)
                m = step_pair(ahead, t, c, c % 2, m, first=(c == 0))
            finish(t)

        scores(0, 0, 0)
        if ntiles > 1:
            def tile(t, carry):
                tile_steps(t, t + 1)
                return carry
            lax.fori_loop(0, ntiles - 1, tile, 0)
        tile_steps(ntiles - 1, None)
    else:
        assert ntiles == 1
        steady = nchunks - 1
        groups = (steady - 1) // ATTN_UNROLL
        peeled = steady - groups * ATTN_UNROLL

        def step(c, slot, m, first=False):
            return step_pair((0, c + 1, 1 - slot), 0, c, slot, m, first)

        scores(0, 0, 0)
        m = neg_inf
        for c in range(peeled):
            m = step(c, c % 2, m, first=(c == 0))

        def group(g, m):
            for u in range(ATTN_UNROLL):
                m = step(peeled + g * ATTN_UNROLL + u, (peeled + u) % 2, m)
            return m
        m = lax.fori_loop(0, groups, group, m)
        softmax_pv(0, nchunks - 1, (nchunks - 1) % 2, m)
        finish(0)


def _attention(qt, k, vt, bounded):
    B, nq = qt.shape[0], qt.shape[1]
    S = k.shape[1]
    tk = vt.shape[3]
    ntiles = min(nq, ATTN_TILES_PER_STEP) if S // tk <= ATTN_UNROLL else 1
    qspec = pl.BlockSpec((None, ntiles, LANES, Q_ROWS), lambda b, i: (b, i, 0, 0))
    if bounded:
        buffers = [pltpu.VMEM((2, tk, Q_ROWS), BF16)]
    else:
        buffers = [pltpu.VMEM((2, tk, Q_ROWS), F32), pltpu.VMEM((2, 1, Q_ROWS), F32)]
    return pl.pallas_call(
        functools.partial(_attn_kernel, bounded=bounded),
        grid=(B, nq // ntiles),
        in_specs=[qspec,
                  pl.BlockSpec((None, S, LANES), lambda b, i: (b, 0, 0)),
                  pl.BlockSpec((None,) + vt.shape[1:], lambda b, i: (b, 0, 0, 0))],
        out_specs=pl.BlockSpec((None, ntiles, HEAD_DIM, Q_ROWS), lambda b, i: (b, i, 0, 0)),
        out_shape=jax.ShapeDtypeStruct((B, nq, HEAD_DIM, Q_ROWS), BF16),
        scratch_shapes=buffers + [pltpu.VMEM((ntiles, N_KV_HEADS, VT_ROWS, Q_COLS_PER_KV), F32)],
        compiler_params=_cparams("parallel", "parallel"),
        name="attn_bounded" if bounded else "attn",
    )(qt, k, vt)


def _dft_tables(S):
    n2 = DFT_N2
    n1 = S // n2
    c = np.arange(FOURIER_W)
    ang = 2.0 * np.pi * np.outer(c, c) / FOURIER_W
    chan = np.concatenate([np.cos(ang), -np.sin(ang)], axis=1) / np.sqrt(FOURIER_W)
    a1 = 2.0 * np.pi * np.outer(np.arange(n1), np.arange(n1)) / n1
    eye = np.eye(SUBLANES)
    mr = np.kron(np.cos(a1), eye) / np.sqrt(n1)
    mi = np.kron(-np.sin(a1), eye) / np.sqrt(n1)
    m1 = np.block([[mr, -mi], [mi, mr]])
    k1 = np.arange(n1)[None, :, None]
    s2 = (np.arange(n2 // SUBLANES)[:, None, None] * SUBLANES + np.arange(SUBLANES)[None, None, :])
    at = 2.0 * np.pi * (k1 * s2) / S
    twr = np.cos(at).reshape(n2 // SUBLANES, n1 * SUBLANES, 1)
    twi = (-np.sin(at)).reshape(n2 // SUBLANES, n1 * SUBLANES, 1)
    a2 = 2.0 * np.pi * np.outer(np.arange(n2), np.arange(n2)) / n2
    c2 = (np.cos(a2) / np.sqrt(n2))[:, None, None, :] * eye[None, :, :, None]
    s2m = (np.sin(a2) / np.sqrt(n2))[:, None, None, :] * eye[None, :, :, None]
    m2 = np.concatenate([c2.reshape(n2 * SUBLANES, n2 * SUBLANES),
                         s2m.reshape(n2 * SUBLANES, n2 * SUBLANES)], axis=1)
    c2s2 = np.concatenate([np.cos(a2), np.sin(a2)], axis=1) / np.sqrt(n2)
    return dict(chan=jnp.asarray(chan, BF16), m1=jnp.asarray(m1, BF16),
                twr=jnp.asarray(twr, F32), twi=jnp.asarray(twi, F32), m2=jnp.asarray(m2, BF16),
                c2=jnp.asarray(c2s2, BF16))


def _four1_kernel(fa_ref, chan_ref, m1_ref, twr_ref, twi_ref, br_ref, bi_ref):
    n1, sb, _ = fa_ref.shape
    rows = n1 * SUBLANES
    for i in range(sb // SUBLANES):
        rs = slice(i * SUBLANES, (i + 1) * SUBLANES)
        x = fa_ref[:, rs, :].reshape(rows, GROUP_W).astype(BF16)
        zr, zi = [], []
        for g in range(N_FOURIER_GROUPS):
            z = jnp.dot(x[:, g * FOURIER_W:(g + 1) * FOURIER_W], chan_ref[...],
                        preferred_element_type=F32)
            zr.append(z[:, :FOURIER_W])
            zi.append(z[:, FOURIER_W:])
        zcat = jnp.concatenate([jnp.concatenate(zr, axis=1), jnp.concatenate(zi, axis=1)],
                               axis=0).astype(BF16)
        a = jnp.dot(m1_ref[...], zcat, preferred_element_type=F32)
        ar, ai = a[:rows], a[rows:]
        twr, twi = twr_ref[i], twi_ref[i]
        br_ref[:, rs, :] = (ar * twr - ai * twi).reshape(n1, SUBLANES, GROUP_W)
        bi_ref[:, rs, :] = (ar * twi + ai * twr).reshape(n1, SUBLANES, GROUP_W)


def _four2_kernel(br_ref, bi_ref, m2_ref, wf_ref, fg_ref, o_ref):
    _, n2, _ = br_ref.shape
    rows = SUBLANES * n2
    bcat = jnp.concatenate([br_ref[...].reshape(rows, GROUP_W), bi_ref[...].reshape(rows, GROUP_W)],
                           axis=0).astype(BF16)
    f = jnp.dot(m2_ref[...], bcat, preferred_element_type=F32).astype(BF16)
    gate = _silu(fg_ref[...].reshape(rows, GROUP_W))
    ys = [jnp.dot(f[:, g * FOURIER_W:(g + 1) * FOURIER_W], wf_ref[g], preferred_element_type=F32)
          for g in range(N_FOURIER_GROUPS)]
    y = jnp.concatenate(ys, axis=1) * gate
    o_ref[...] = y.reshape(n2, SUBLANES, GROUP_W)


def _four_fused_kernel(fa_ref, fg_ref, chan_ref, m1_ref, twr_ref, twi_ref, c2_ref, wf_ref, o_ref,
                       zr_ref, zi_ref, br_ref, bi_ref, f_ref):
    S = fa_ref.shape[0]
    n2 = DFT_N2
    n1 = S // n2
    lanes = lambda g: slice(g * FOURIER_W, (g + 1) * FOURIER_W)

    x = fa_ref[...].astype(BF16)
    for g in range(N_FOURIER_GROUPS):
        z = jnp.dot(x[:, lanes(g)], chan_ref[...], preferred_element_type=F32)
        zr_ref[:, lanes(g)] = z[:, :FOURIER_W]
        zi_ref[:, lanes(g)] = z[:, FOURIER_W:]

    def stage1(j, carry):
        base = pl.multiple_of(j * SUBLANES, SUBLANES)
        tiles = ([zr_ref[pl.ds(s1 * n2 + base, SUBLANES), :] for s1 in range(n1)]
                 + [zi_ref[pl.ds(s1 * n2 + base, SUBLANES), :] for s1 in range(n1)])
        a = jnp.dot(m1_ref[...], jnp.concatenate(tiles, axis=0).astype(BF16),
                    preferred_element_type=F32)
        ar, ai = a[:n1 * SUBLANES], a[n1 * SUBLANES:]
        twr, twi = twr_ref[j], twi_ref[j]
        br = ar * twr - ai * twi
        bi = ar * twi + ai * twr
        for k1 in range(n1):
            rows = slice(k1 * SUBLANES, (k1 + 1) * SUBLANES)
            br_ref[pl.ds(k1 * n2 + base, SUBLANES), :] = br[rows]
            bi_ref[pl.ds(k1 * n2 + base, SUBLANES), :] = bi[rows]
        return carry

    lax.fori_loop(0, n2 // SUBLANES, stage1, 0, unroll=FOURIER_UNROLL)

    def stage2(k1, carry):
        start = pl.multiple_of(k1 * n2, n2)
        bcat = jnp.concatenate([br_ref[pl.ds(start, n2), :], bi_ref[pl.ds(start, n2), :]],
                               axis=0).astype(BF16)
        f = jnp.dot(c2_ref[...], bcat, preferred_element_type=F32)
        for g in range(N_FOURIER_GROUPS):
            f_ref[g, pl.ds(k1, n2, stride=n1), :] = f[:, lanes(g)]
        return carry

    lax.fori_loop(0, n1, stage2, 0, unroll=FOURIER_UNROLL)

    gate = _silu(fg_ref[...].astype(F32))
    for g in range(N_FOURIER_GROUPS):
        y = jnp.dot(f_ref[g].astype(BF16), wf_ref[g], preferred_element_type=F32)
        o_ref[:, lanes(g)] = (y * gate[:, lanes(g)]).astype(o_ref.dtype)


def _fourier_fused(fa, fg, wf, ft):
    B, S, _ = fa.shape
    n1 = S // DFT_N2
    tok = pl.BlockSpec((None, S, GROUP_W), lambda b: (b, 0, 0))
    return pl.pallas_call(
        _four_fused_kernel,
        grid=(B,),
        in_specs=[tok, tok, _const_spec((FOURIER_W, 2 * FOURIER_W)),
                  _const_spec((2 * n1 * SUBLANES, 2 * n1 * SUBLANES)),
                  _const_spec(ft["twr"].shape), _const_spec(ft["twi"].shape),
                  _const_spec((DFT_N2, 2 * DFT_N2)),
                  _const_spec((N_FOURIER_GROUPS, FOURIER_W, FOURIER_W))],
        out_specs=tok,
        out_shape=jax.ShapeDtypeStruct((B, S, GROUP_W), BF16),
        scratch_shapes=[pltpu.VMEM((S, GROUP_W), F32)] * 4
                       + [pltpu.VMEM((N_FOURIER_GROUPS, S, FOURIER_W), F32)],
        compiler_params=_cparams("parallel"),
        name="four",
    )(fa, fg, ft["chan"], ft["m1"], ft["twr"], ft["twi"], ft["c2"], wf)


def _fourier(fa, fg, wf, ft):
    B, S, _ = fa.shape
    if S <= FOURIER_FUSED_MAX_SEQ:
        return _fourier_fused(fa, fg, wf, ft)
    n2 = DFT_N2
    n1 = S // n2
    sb = 32 if n1 <= 16 else SUBLANES
    nsub = sb // SUBLANES
    fa4 = fa.reshape(B, n1, n2, GROUP_W)
    blk1 = pl.BlockSpec((None, n1, sb, GROUP_W), lambda b, j: (b, 0, j, 0))
    tw = pl.BlockSpec((nsub, n1 * SUBLANES, 1), lambda b, j: (j, 0, 0))
    br, bi = pl.pallas_call(
        _four1_kernel,
        grid=(B, n2 // sb),
        in_specs=[blk1, _const_spec((FOURIER_W, 2 * FOURIER_W)),
                  _const_spec((2 * n1 * SUBLANES, 2 * n1 * SUBLANES)), tw, tw],
        out_specs=[blk1, blk1],
        out_shape=[jax.ShapeDtypeStruct((B, n1, n2, GROUP_W), F32)] * 2,
        compiler_params=_cparams("parallel", "parallel"),
        name="four1",
    )(fa4, ft["chan"], ft["m1"], ft["twr"], ft["twi"])
    blk_in = pl.BlockSpec((None, SUBLANES, n2, GROUP_W), lambda b, j: (b, j, 0, 0))
    blk_out = pl.BlockSpec((None, n2, SUBLANES, GROUP_W), lambda b, j: (b, 0, j, 0))
    o = pl.pallas_call(
        _four2_kernel,
        grid=(B, n1 // SUBLANES),
        in_specs=[blk_in, blk_in, _const_spec((SUBLANES * n2, 2 * SUBLANES * n2)),
                  _const_spec((N_FOURIER_GROUPS, FOURIER_W, FOURIER_W)), blk_out],
        out_specs=blk_out,
        out_shape=jax.ShapeDtypeStruct((B, n2, n1, GROUP_W), F32),
        compiler_params=_cparams("parallel", "parallel"),
        name="four2",
    )(br, bi, ft["m2"], wf, fg.reshape(B, n2, n1, GROUP_W))
    return o.reshape(B, S, GROUP_W)


def _outproj_kernel(ot_ref, ag_ref, of_ref, os_ref, om_ref, x_ref, w_ref, g_ref, y_ref):
    for s in range(ot_ref.shape[0] // OUT_SUB_TILES):
        rows = slice(s * OUT_SUB_TILES * Q_TILE, (s + 1) * OUT_SUB_TILES * Q_TILE)
        tiles = []
        for t in range(s * OUT_SUB_TILES, (s + 1) * OUT_SUB_TILES):
            ot = ot_ref[t].astype(F32)
            slabs = []
            for j in range(GROUP_W // LANES):
                pair = jnp.concatenate([ot[:, (2 * j) * Q_TILE:(2 * j + 1) * Q_TILE],
                                        ot[:, (2 * j + 1) * Q_TILE:(2 * j + 2) * Q_TILE]], axis=0)
                slabs.append(pair.T)
            tiles.append(jnp.concatenate(slabs, axis=1))
        oa = (jnp.concatenate(tiles, axis=0) * _silu(ag_ref[rows, :].astype(F32))).astype(BF16)
        rest = jnp.concatenate([of_ref[rows, :].astype(BF16), os_ref[rows, :], om_ref[rows, :]],
                               axis=1)
        y = jnp.dot(rest, w_ref[GROUP_W:, :], preferred_element_type=F32)
        y = y + jnp.dot(oa, w_ref[:GROUP_W, :], preferred_element_type=F32)
        y_ref[rows, :] = x_ref[rows, :] + _rms(y, g_ref[...])


def _outproj(ot, ag, of, osg, om, x, w, g):
    B, S, _ = x.shape
    tm = TOKEN_TILE
    tok = lambda wd: pl.BlockSpec((None, tm, wd), lambda b, i: (b, i, 0))
    return pl.pallas_call(
        _outproj_kernel,
        grid=(B, S // tm),
        in_specs=[pl.BlockSpec((None, tm // Q_TILE, HEAD_DIM, Q_ROWS), lambda b, i: (b, i, 0, 0)),
                  tok(GROUP_W), tok(GROUP_W), tok(GROUP_W), tok(GROUP_W), tok(D_MODEL),
                  _const_spec((4 * GROUP_W, D_MODEL)), _const_spec((1, D_MODEL))],
        out_specs=tok(D_MODEL),
        out_shape=jax.ShapeDtypeStruct((B, S, D_MODEL), F32),
        compiler_params=_cparams("parallel", "parallel"),
        name="outproj",
    )(ot, ag, of, osg, om, x, w, g)


def _rope_tables(S):
    rows = S // GRID_W
    row = jnp.broadcast_to(jnp.arange(rows, dtype=F32)[:, None], (rows, GRID_W)).reshape(S)
    col = jnp.broadcast_to(jnp.arange(GRID_W, dtype=F32)[None, :], (rows, GRID_W)).reshape(S)
    inv = ROPE_THETA ** (-jnp.arange(ROPE_PAIRS, dtype=F32) / ROPE_PAIRS)
    ang = jnp.stack([row[:, None] * inv, col[:, None] * inv], axis=1)
    cos, sin = jnp.cos(ang), jnp.sin(ang)
    zero = jnp.zeros((S, ROPE_PAIRS), F32)
    two = lambda parts: jnp.tile(jnp.concatenate(parts, axis=-1), (1, LANES // HEAD_DIM))
    return dict(cos=two([cos[:, 0], cos[:, 0], cos[:, 1], cos[:, 1]]),
                sina=two([zero, sin[:, 0], zero, sin[:, 1]]),
                sinb=two([-sin[:, 0], zero, -sin[:, 1], zero]))


def _layer_weights(l, pre_norm_g, w_in, q_norm_g, k_norm_g, w_fourier, sgu_norm_g, w_spatial,
                   b_spatial, mem_norm_g, w_mem_kv, w_out, post_norm_g):
    return dict(
        pre_g=pre_norm_g[l][None, :],
        w_in=w_in[l].astype(BF16),
        qg=jnp.tile(q_norm_g[l], N_HEADS)[None, :],
        kg=jnp.tile(k_norm_g[l], N_KV_HEADS)[None, :],
        wf=w_fourier[l].astype(BF16),
        vg=sgu_norm_g[l].reshape(1, GROUP_W),
        ws=w_spatial[l].astype(BF16),
        bs=jnp.repeat(b_spatial[l].T, SGU_W, axis=1),
        mem_g=mem_norm_g[l][None, :],
        w_mem_kv=w_mem_kv[l].astype(BF16),
        w_out=w_out[l].astype(BF16),
        post_g=post_norm_g[l][None, :],
        score_bound=(HEAD_DIM * Q_SCALE * jnp.max(jnp.abs(q_norm_g[l]))
                     * jnp.max(jnp.abs(k_norm_g[l]))),
    )


def _trunk(x, mem, layers):
    S = x.shape[1]
    tabs = _rope_tables(S)
    ids = np.arange(GROUP_W) // HEAD_DIM
    tabs["bd"] = jnp.asarray(ids[:, None] == ids[None, :], BF16)
    ft = _dft_tables(S)
    for lw in layers:
        kv = _memkv(mem, lw["mem_g"], lw["w_mem_kv"])
        qt, k, vt, ag, fa, fg, osgu, omem = _inproj(x, kv, lw, tabs)
        ot = lax.cond(lw["score_bound"] < ATTN_UNSHIFTED_MAX_LOG2,
                      functools.partial(_attention, bounded=True),
                      functools.partial(_attention, bounded=False), qt, k, vt)
        ofour = _fourier(fa, fg, lw["wf"], ft)
        x = _outproj(ot, ag, ofour, osgu, omem, x, lw["w_out"], lw["post_g"])
    return x


def kernel(x_prompt, x_sample, mem_prompt, mem_sample, pre_norm_g, w_in, q_norm_g, k_norm_g,
           w_fourier, sgu_norm_g, w_spatial, b_spatial, mem_norm_g, w_mem_kv, w_out, post_norm_g):
    layers = [_layer_weights(l, pre_norm_g, w_in, q_norm_g, k_norm_g, w_fourier, sgu_norm_g,
                             w_spatial, b_spatial, mem_norm_g, w_mem_kv, w_out, post_norm_g)
              for l in range(DEPTH)]
    return (_trunk(x_prompt, mem_prompt, layers), _trunk(x_sample, mem_sample, layers))
```

```python
import functools

import numpy as np
import jax
import jax.numpy as jnp
from jax import lax
from jax.experimental import pallas as pl
from jax.experimental.pallas import tpu as pltpu

F32 = jnp.float32
BF16 = jnp.bfloat16

D_MODEL = 1024
DEPTH = 2
GRID_W = 64
N_MEM = 256
GROUP_W = 512
HEAD_DIM = 64
N_HEADS = 8
N_KV_HEADS = 2
Q_PER_KV = N_HEADS // N_KV_HEADS
ROPE_PAIRS = HEAD_DIM // 4
ROPE_THETA = 10000.0
N_FOURIER_GROUPS = 4
FOURIER_W = 128
N_SGU_HEADS = 4
SGU_W = 128
CHUNK = 128
N_MEM_HEADS = 4
MEM_HEAD_DIM = 128
EPS = 1e-6
IN_W = 4864
C_AQ, C_AK, C_AV, C_AG, C_FA, C_FG, C_SU, C_SV, C_SG, C_MQ, C_MG = (
    0, 512, 640, 768, 1280, 1792, 2304, 2816, 3328, 3840, 4352)

LANES = 128
SUBLANES = 8
MXU_COLS = 256
VMEM_LIMIT_BYTES = 56 * 1024 * 1024

TOKEN_TILE = 512
KV_CHUNK = 512
Q_TILE = 128
Q_ROWS = N_HEADS * Q_TILE
DFT_N2 = 128
FOURIER_FUSED_MAX_SEQ = 2048
FOURIER_UNROLL = 4
ATTN_UNROLL = 14
OUT_SUB_TILES = 2
ATTN_UNSHIFTED_MAX_LOG2 = 64.0
ATTN_TILES_PER_STEP = 8
BF16_SUBLANES = 16
VT_ROWS = HEAD_DIM + BF16_SUBLANES
Q_COLS_PER_KV = Q_PER_KV * Q_TILE
Q_SCALE = float(HEAD_DIM ** -0.5 * np.log2(np.e))


def _cparams(*sem):
    return pltpu.CompilerParams(dimension_semantics=sem, vmem_limit_bytes=VMEM_LIMIT_BYTES)


def _const_spec(shape):
    nd = len(shape)
    return pl.BlockSpec(shape, lambda *_: (0,) * nd)


def _silu(g):
    return g / (1.0 + jnp.exp(-g))


def _rms(x, g):
    ms = jnp.mean(x * x, axis=-1, keepdims=True)
    return x * lax.rsqrt(ms + EPS) * g


def _memkv_kernel(mem_ref, g_ref, w_ref, kv_ref):
    h = _rms(mem_ref[...], g_ref[...]).astype(BF16)
    kv_ref[...] = jnp.dot(h, w_ref[...], preferred_element_type=F32).astype(BF16)


def _memkv(mem, g, w):
    B = mem.shape[0]
    return pl.pallas_call(
        _memkv_kernel,
        grid=(B,),
        in_specs=[pl.BlockSpec((None, N_MEM, D_MODEL), lambda b: (b, 0, 0)),
                  _const_spec((1, D_MODEL)),
                  _const_spec((D_MODEL, 2 * GROUP_W))],
        out_specs=pl.BlockSpec((None, N_MEM, 2 * GROUP_W), lambda b: (b, 0, 0)),
        out_shape=jax.ShapeDtypeStruct((B, N_MEM, 2 * GROUP_W), BF16),
        compiler_params=_cparams("parallel"),
        name="memkv",
    )(mem, g, w)


def _head_ssq(x, bd):
    return jnp.dot((x * x).astype(BF16), bd, preferred_element_type=F32)


def _rope(x, cos, sina, sinb):
    return x * cos + pltpu.roll(x, 16, 1) * sina + pltpu.roll(x, LANES - 16, 1) * sinb


def _inproj_kernel(x_ref, pre_g_ref, w_ref, bd_ref, qg_ref, kg_ref, cos_ref, sina_ref, sinb_ref,
                   vg_ref, ws_ref, bs_ref, kv_ref,
                   qt_ref, k_ref, vt_ref, ag_ref, fa_ref, fg_ref, osgu_ref, omem_ref):
    tm = x_ref.shape[0]
    h = _rms(x_ref[...], pre_g_ref[...]).astype(BF16)

    def proj(lo, width):
        return jnp.dot(h, w_ref[:, lo:lo + width], preferred_element_type=F32)

    cos, sina, sinb = cos_ref[...], sina_ref[...], sinb_ref[...]
    nck = tm // CHUNK
    lanes = lambda i: slice(i * LANES, (i + 1) * LANES)

    aq = proj(C_AQ, GROUP_W)
    ak = proj(C_AK, LANES)
    av = proj(C_AV, LANES)
    mq = proj(C_MQ, GROUP_W)
    sv = proj(C_SV, GROUP_W)
    ssq = _head_ssq(aq, bd_ref[...])
    ssk = _head_ssq(ak, bd_ref[0:LANES, 0:LANES])

    mq16 = mq.astype(BF16)
    scores = [lax.dot_general(mq16[:, lanes(hd)], kv_ref[:, lanes(hd)], (((1,), (1,)), ((), ())),
                              preferred_element_type=F32) * (MEM_HEAD_DIM ** -0.5)
              for hd in range(N_MEM_HEADS)]

    su = proj(C_SU, GROUP_W)
    sg = proj(C_SG, GROUP_W)
    mg = proj(C_MG, GROUP_W)

    vhs = [_rms(sv[:, lanes(hd)], vg_ref[:, lanes(hd)]).astype(BF16) for hd in range(N_SGU_HEADS)]
    spcs = [jnp.dot(ws_ref[hd],
                    jnp.concatenate([vhs[hd][c * CHUNK:(c + 1) * CHUNK, :] for c in range(nck)],
                                    axis=1), preferred_element_type=F32)
            for hd in range(N_SGU_HEADS)]

    ag_ref[...] = proj(C_AG, GROUP_W).astype(BF16)
    fa_ref[...] = proj(C_FA, GROUP_W).astype(fa_ref.dtype)
    fg_ref[...] = proj(C_FG, GROUP_W).astype(fg_ref.dtype)

    es = [jnp.exp(s - jnp.max(s, axis=-1, keepdims=True)) for s in scores]
    mgate = _silu(mg)
    for hd in range(N_MEM_HEADS):
        mv = kv_ref[:, GROUP_W + hd * MEM_HEAD_DIM:GROUP_W + (hd + 1) * MEM_HEAD_DIM]
        o = jnp.dot(es[hd].astype(BF16), mv, preferred_element_type=F32)
        o = o / jnp.sum(es[hd], axis=-1, keepdims=True)
        omem_ref[:, lanes(hd)] = (o * mgate[:, lanes(hd)]).astype(BF16)

    qn = aq * lax.rsqrt(ssq * (1.0 / HEAD_DIM) + EPS) * qg_ref[...]
    slabs = [_rope(qn[:, lanes(j)], cos, sina, sinb) * Q_SCALE for j in range(GROUP_W // LANES)]
    low = lax.broadcasted_iota(jnp.int32, (Q_TILE, LANES), 1) < HEAD_DIM
    for t in range(tm // Q_TILE):
        parts = []
        for j, slab in enumerate(slabs):
            sj = slab[t * Q_TILE:(t + 1) * Q_TILE, :]
            rolled = pltpu.roll(sj, HEAD_DIM, 1)
            if j < Q_PER_KV // 2:
                parts += [jnp.where(low, sj, 0.0), jnp.where(low, rolled, 0.0)]
            else:
                parts += [jnp.where(low, 0.0, rolled), jnp.where(low, 0.0, sj)]
        qt_ref[t] = jnp.concatenate(parts, axis=0).T.astype(BF16)

    kn = ak * lax.rsqrt(ssk * (1.0 / HEAD_DIM) + EPS) * kg_ref[...]
    k_ref[...] = _rope(kn, cos, sina, sinb).astype(BF16)
    avt = av.T.astype(BF16)
    for g in range(N_KV_HEADS):
        vt_ref[g * VT_ROWS:g * VT_ROWS + HEAD_DIM, :] = avt[g * HEAD_DIM:(g + 1) * HEAD_DIM, :]
        vt_ref[g * VT_ROWS + HEAD_DIM:(g + 1) * VT_ROWS, :] = jnp.ones((BF16_SUBLANES, tm), BF16)

    gate = _silu(sg)
    for hd in range(N_SGU_HEADS):
        for c in range(nck):
            rows = slice(c * CHUNK, (c + 1) * CHUNK)
            sp = spcs[hd][:, c * SGU_W:(c + 1) * SGU_W] + bs_ref[:, lanes(hd)]
            osgu_ref[rows, lanes(hd)] = (su[rows, lanes(hd)] * sp
                                         * gate[rows, lanes(hd)]).astype(BF16)


def _inproj(x, kv, lw, tabs):
    B, S, _ = x.shape
    tm = TOKEN_TILE
    nt = S // tm
    tok = lambda w: pl.BlockSpec((None, tm, w), lambda b, i: (b, i, 0))
    pos = pl.BlockSpec((tm, LANES), lambda b, i: (i, 0))
    in_specs = [
        tok(D_MODEL),
        _const_spec((1, D_MODEL)),
        _const_spec((D_MODEL, IN_W)),
        _const_spec((GROUP_W, GROUP_W)),
        _const_spec((1, GROUP_W)),
        _const_spec((1, LANES)),
        pos, pos, pos,
        _const_spec((1, GROUP_W)),
        _const_spec((N_SGU_HEADS, CHUNK, CHUNK)),
        _const_spec((CHUNK, GROUP_W)),
        pl.BlockSpec((None, N_MEM, 2 * GROUP_W), lambda b, i: (b, 0, 0)),
    ]
    out_specs = [
        pl.BlockSpec((None, tm // Q_TILE, LANES, Q_ROWS), lambda b, i: (b, i, 0, 0)), tok(LANES),
        pl.BlockSpec((None, None, N_KV_HEADS * VT_ROWS, tm), lambda b, i: (b, i, 0, 0)),
        tok(GROUP_W), tok(GROUP_W), tok(GROUP_W), tok(GROUP_W), tok(GROUP_W),
    ]
    sds = jax.ShapeDtypeStruct
    four_dt = BF16 if S <= FOURIER_FUSED_MAX_SEQ else F32
    out_shape = [
        sds((B, S // Q_TILE, LANES, Q_ROWS), BF16), sds((B, S, LANES), BF16),
        sds((B, nt, N_KV_HEADS * VT_ROWS, tm), BF16),
        sds((B, S, GROUP_W), BF16), sds((B, S, GROUP_W), four_dt), sds((B, S, GROUP_W), four_dt),
        sds((B, S, GROUP_W), BF16), sds((B, S, GROUP_W), BF16),
    ]
    return pl.pallas_call(
        _inproj_kernel,
        grid=(B, nt),
        in_specs=in_specs, out_specs=out_specs, out_shape=out_shape,
        compiler_params=_cparams("parallel", "parallel"),
        name="inproj",
    )(x, lw["pre_g"], lw["w_in"], tabs["bd"], lw["qg"], lw["kg"],
      tabs["cos"], tabs["sina"], tabs["sinb"], lw["vg"], lw["ws"], lw["bs"], kv)


def _attn_kernel(qt_ref, k_ref, vt_ref, ot_ref, *scratch, bounded):
    ntiles = qt_ref.shape[0]
    nchunks, _, tk = vt_ref.shape
    neg_inf = jnp.full((1, qt_ref.shape[2]), -jnp.inf, F32)

    if bounded:
        p_ref, acc_ref = scratch
    else:
        st_ref, mx_ref, acc_ref = scratch

    def scores(t, c, slot):
        start = c * tk if isinstance(c, int) else pl.multiple_of(c * tk, tk)
        st = jnp.dot(k_ref[pl.ds(start, tk), :], qt_ref[t], preferred_element_type=F32)
        if bounded:
            p_ref[slot] = jnp.exp2(st).astype(BF16)
        else:
            st_ref[slot] = st
            mx_ref[slot] = jnp.max(st, axis=0, keepdims=True)

    def softmax_pv(t, c, slot, m_old, first=False):
        if bounded:
            p, m_new, alpha = p_ref[slot], m_old, None
        else:
            m_new = mx_ref[slot]
            if not first:
                m_new = jnp.maximum(m_old, m_new)
            p = jnp.exp2(st_ref[slot] - m_new).astype(BF16)
            alpha = None if first else jnp.exp2(m_old - m_new)
        for g in range(N_KV_HEADS):
            cols = slice(g * Q_COLS_PER_KV, (g + 1) * Q_COLS_PER_KV)
            pv = jnp.dot(vt_ref[c, g * VT_ROWS:(g + 1) * VT_ROWS, :], p[:, cols],
                         preferred_element_type=F32)
            if first:
                acc_ref[t, g] = pv
            elif bounded:
                acc_ref[t, g] += pv
            else:
                acc_ref[t, g] = alpha[:, cols] * acc_ref[t, g] + pv
        return m_new

    def step_pair(nxt, t, c, slot, m_old, first=False):
        if not bounded or nxt is None:
            if nxt is not None:
                scores(*nxt)
            return softmax_pv(t, c, slot, m_old, first)
        tn, cn, sn = nxt
        start = cn * tk if isinstance(cn, int) else pl.multiple_of(cn * tk, tk)
        kc = k_ref[pl.ds(start, tk), :]
        for i in range(Q_ROWS // MXU_COLS):
            cols = slice(i * MXU_COLS, (i + 1) * MXU_COLS)
            st = jnp.dot(kc, qt_ref[tn, :, cols], preferred_element_type=F32)
            p_ref[sn, :, cols] = jnp.exp2(st).astype(BF16)
            g, off = divmod(i * MXU_COLS, Q_COLS_PER_KV)
            pv = jnp.dot(vt_ref[c, g * VT_ROWS:(g + 1) * VT_ROWS, :], p_ref[slot, :, cols],
                         preferred_element_type=F32)
            if first:
                acc_ref[t, g, :, off:off + MXU_COLS] = pv
            else:
                acc_ref[t, g, :, off:off + MXU_COLS] += pv
        return m_old

    def finish(t):
        for g in range(N_KV_HEADS):
            acc = acc_ref[t, g]
            ot_ref[t, :, g * Q_COLS_PER_KV:(g + 1) * Q_COLS_PER_KV] = (
                acc[:HEAD_DIM] / acc[HEAD_DIM:HEAD_DIM + 1]).astype(BF16)

    if nchunks <= ATTN_UNROLL:
        assert nchunks % 2 == 0

        def tile_steps(t, nxt):
            m = None
            for c in range(nchunks):
                if c + 1 < nchunks:
                    ahead = (t, c + 1, (c + 1) % 2)
                else:
                    ahead = None if nxt is None else (nxt, 0, 0)
                m = step_pair(ahead, t, c, c % 2, m, first=(c == 0))
            finish(t)

        scores(0, 0, 0)
        if ntiles > 1:
            def tile(t, carry):
                tile_steps(t, t + 1)
                return carry
            lax.fori_loop(0, ntiles - 1, tile, 0, unroll=True)
        tile_steps(ntiles - 1, None)
    else:
        assert ntiles == 1
        steady = nchunks - 1
        groups = (steady - 1) // ATTN_UNROLL
        peeled = steady - groups * ATTN_UNROLL

        def step(c, slot, m, first=False):
            return step_pair((0, c + 1, 1 - slot), 0, c, slot, m, first)

        scores(0, 0, 0)
        m = neg_inf
        for c in range(peeled):
            m = step(c, c % 2, m, first=(c == 0))

        def group(g, m):
            for u in range(ATTN_UNROLL):
                m = step(peeled + g * ATTN_UNROLL + u, (peeled + u) % 2, m)
            return m
        m = lax.fori_loop(0, groups, group, m)
        softmax_pv(0, nchunks - 1, (nchunks - 1) % 2, m)
        finish(0)


def _attention(qt, k, vt, bounded):
    B, nq = qt.shape[0], qt.shape[1]
    S = k.shape[1]
    tk = vt.shape[3]
    ntiles = min(nq, ATTN_TILES_PER_STEP) if S // tk <= ATTN_UNROLL else 1
    qspec = pl.BlockSpec((None, ntiles, LANES, Q_ROWS), lambda b, i: (b, i, 0, 0))
    if bounded:
        buffers = [pltpu.VMEM((2, tk, Q_ROWS), BF16)]
    else:
        buffers = [pltpu.VMEM((2, tk, Q_ROWS), F32), pltpu.VMEM((2, 1, Q_ROWS), F32)]
    return pl.pallas_call(
        functools.partial(_attn_kernel, bounded=bounded),
        grid=(B, nq // ntiles),
        in_specs=[qspec,
                  pl.BlockSpec((None, S, LANES), lambda b, i: (b, 0, 0)),
                  pl.BlockSpec((None,) + vt.shape[1:], lambda b, i: (b, 0, 0, 0))],
        out_specs=pl.BlockSpec((None, ntiles, HEAD_DIM, Q_ROWS), lambda b, i: (b, i, 0, 0)),
        out_shape=jax.ShapeDtypeStruct((B, nq, HEAD_DIM, Q_ROWS), BF16),
        scratch_shapes=buffers + [pltpu.VMEM((ntiles, N_KV_HEADS, VT_ROWS, Q_COLS_PER_KV), F32)],
        compiler_params=_cparams("parallel", "parallel"),
        name="attn_bounded" if bounded else "attn",
    )(qt, k, vt)


def _dft_tables(S):
    n2 = DFT_N2
    n1 = S // n2
    c = np.arange(FOURIER_W)
    ang = 2.0 * np.pi * np.outer(c, c) / FOURIER_W
    chan = np.concatenate([np.cos(ang), -np.sin(ang)], axis=1) / np.sqrt(FOURIER_W)
    a1 = 2.0 * np.pi * np.outer(np.arange(n1), np.arange(n1)) / n1
    eye = np.eye(SUBLANES)
    mr = np.kron(np.cos(a1), eye) / np.sqrt(n1)
    mi = np.kron(-np.sin(a1), eye) / np.sqrt(n1)
    m1 = np.block([[mr, -mi], [mi, mr]])
    k1 = np.arange(n1)[None, :, None]
    s2 = (np.arange(n2 // SUBLANES)[:, None, None] * SUBLANES + np.arange(SUBLANES)[None, None, :])
    at = 2.0 * np.pi * (k1 * s2) / S
    twr = np.cos(at).reshape(n2 // SUBLANES, n1 * SUBLANES, 1)
    twi = (-np.sin(at)).reshape(n2 // SUBLANES, n1 * SUBLANES, 1)
    a2 = 2.0 * np.pi * np.outer(np.arange(n2), np.arange(n2)) / n2
    c2 = (np.cos(a2) / np.sqrt(n2))[:, None, None, :] * eye[None, :, :, None]
    s2m = (np.sin(a2) / np.sqrt(n2))[:, None, None, :] * eye[None, :, :, None]
    m2 = np.concatenate([c2.reshape(n2 * SUBLANES, n2 * SUBLANES),
                         s2m.reshape(n2 * SUBLANES, n2 * SUBLANES)], axis=1)
    c2s2 = np.concatenate([np.cos(a2), np.sin(a2)], axis=1) / np.sqrt(n2)
    return dict(chan=jnp.asarray(chan, BF16), m1=jnp.asarray(m1, BF16),
                twr=jnp.asarray(twr, F32), twi=jnp.asarray(twi, F32), m2=jnp.asarray(m2, BF16),
                c2=jnp.asarray(c2s2, BF16))


def _four1_kernel(fa_ref, chan_ref, m1_ref, twr_ref, twi_ref, br_ref, bi_ref):
    n1, sb, _ = fa_ref.shape
    rows = n1 * SUBLANES
    for i in range(sb // SUBLANES):
        rs = slice(i * SUBLANES, (i + 1) * SUBLANES)
        x = fa_ref[:, rs, :].reshape(rows, GROUP_W).astype(BF16)
        zr, zi = [], []
        for g in range(N_FOURIER_GROUPS):
            z = jnp.dot(x[:, g * FOURIER_W:(g + 1) * FOURIER_W], chan_ref[...],
                        preferred_element_type=F32)
            zr.append(z[:, :FOURIER_W])
            zi.append(z[:, FOURIER_W:])
        zcat = jnp.concatenate([jnp.concatenate(zr, axis=1), jnp.concatenate(zi, axis=1)],
                               axis=0).astype(BF16)
        a = jnp.dot(m1_ref[...], zcat, preferred_element_type=F32)
        ar, ai = a[:rows], a[rows:]
        twr, twi = twr_ref[i], twi_ref[i]
        br_ref[:, rs, :] = (ar * twr - ai * twi).reshape(n1, SUBLANES, GROUP_W)
        bi_ref[:, rs, :] = (ar * twi + ai * twr).reshape(n1, SUBLANES, GROUP_W)


def _four2_kernel(br_ref, bi_ref, m2_ref, wf_ref, fg_ref, o_ref):
    _, n2, _ = br_ref.shape
    rows = SUBLANES * n2
    bcat = jnp.concatenate([br_ref[...].reshape(rows, GROUP_W), bi_ref[...].reshape(rows, GROUP_W)],
                           axis=0).astype(BF16)
    f = jnp.dot(m2_ref[...], bcat, preferred_element_type=F32).astype(BF16)
    gate = _silu(fg_ref[...].reshape(rows, GROUP_W))
    ys = [jnp.dot(f[:, g * FOURIER_W:(g + 1) * FOURIER_W], wf_ref[g], preferred_element_type=F32)
          for g in range(N_FOURIER_GROUPS)]
    y = jnp.concatenate(ys, axis=1) * gate
    o_ref[...] = y.reshape(n2, SUBLANES, GROUP_W)


def _four_fused_kernel(fa_ref, fg_ref, chan_ref, m1_ref, twr_ref, twi_ref, c2_ref, wf_ref, o_ref,
                       zr_ref, zi_ref, br_ref, bi_ref, f_ref):
    S = fa_ref.shape[0]
    n2 = DFT_N2
    n1 = S // n2
    lanes = lambda g: slice(g * FOURIER_W, (g + 1) * FOURIER_W)

    x = fa_ref[...].astype(BF16)
    for g in range(N_FOURIER_GROUPS):
        z = jnp.dot(x[:, lanes(g)], chan_ref[...], preferred_element_type=F32)
        zr_ref[:, lanes(g)] = z[:, :FOURIER_W]
        zi_ref[:, lanes(g)] = z[:, FOURIER_W:]

    def stage1(j, carry):
        base = pl.multiple_of(j * SUBLANES, SUBLANES)
        tiles = ([zr_ref[pl.ds(s1 * n2 + base, SUBLANES), :] for s1 in range(n1)]
                 + [zi_ref[pl.ds(s1 * n2 + base, SUBLANES), :] for s1 in range(n1)])
        a = jnp.dot(m1_ref[...], jnp.concatenate(tiles, axis=0).astype(BF16),
                    preferred_element_type=F32)
        ar, ai = a[:n1 * SUBLANES], a[n1 * SUBLANES:]
        twr, twi = twr_ref[j], twi_ref[j]
        br = ar * twr - ai * twi
        bi = ar * twi + ai * twr
        for k1 in range(n1):
            rows = slice(k1 * SUBLANES, (k1 + 1) * SUBLANES)
            br_ref[pl.ds(k1 * n2 + base, SUBLANES), :] = br[rows]
            bi_ref[pl.ds(k1 * n2 + base, SUBLANES), :] = bi[rows]
        return carry

    lax.fori_loop(0, n2 // SUBLANES, stage1, 0, unroll=FOURIER_UNROLL)

    def stage2(k1, carry):
        start = pl.multiple_of(k1 * n2, n2)
        bcat = jnp.concatenate([br_ref[pl.ds(start, n2), :], bi_ref[pl.ds(start, n2), :]],
                               axis=0).astype(BF16)
        f = jnp.dot(c2_ref[...], bcat, preferred_element_type=F32)
        for g in range(N_FOURIER_GROUPS):
            f_ref[g, pl.ds(k1, n2, stride=n1), :] = f[:, lanes(g)]
        return carry

    lax.fori_loop(0, n1, stage2, 0, unroll=FOURIER_UNROLL)

    gate = _silu(fg_ref[...].astype(F32))
    for g in range(N_FOURIER_GROUPS):
        y = jnp.dot(f_ref[g].astype(BF16), wf_ref[g], preferred_element_type=F32)
        o_ref[:, lanes(g)] = (y * gate[:, lanes(g)]).astype(o_ref.dtype)


def _fourier_fused(fa, fg, wf, ft):
    B, S, _ = fa.shape
    n1 = S // DFT_N2
    tok = pl.BlockSpec((None, S, GROUP_W), lambda b: (b, 0, 0))
    return pl.pallas_call(
        _four_fused_kernel,
        grid=(B,),
        in_specs=[tok, tok, _const_spec((FOURIER_W, 2 * FOURIER_W)),
                  _const_spec((2 * n1 * SUBLANES, 2 * n1 * SUBLANES)),
                  _const_spec(ft["twr"].shape), _const_spec(ft["twi"].shape),
                  _const_spec((DFT_N2, 2 * DFT_N2)),
                  _const_spec((N_FOURIER_GROUPS, FOURIER_W, FOURIER_W))],
        out_specs=tok,
        out_shape=jax.ShapeDtypeStruct((B, S, GROUP_W), BF16),
        scratch_shapes=[pltpu.VMEM((S, GROUP_W), F32)] * 4
                       + [pltpu.VMEM((N_FOURIER_GROUPS, S, FOURIER_W), F32)],
        compiler_params=_cparams("parallel"),
        name="four",
    )(fa, fg, ft["chan"], ft["m1"], ft["twr"], ft["twi"], ft["c2"], wf)


def _fourier(fa, fg, wf, ft):
    B, S, _ = fa.shape
    if S <= FOURIER_FUSED_MAX_SEQ:
        return _fourier_fused(fa, fg, wf, ft)
    n2 = DFT_N2
    n1 = S // n2
    sb = 32 if n1 <= 16 else SUBLANES
    nsub = sb // SUBLANES
    fa4 = fa.reshape(B, n1, n2, GROUP_W)
    blk1 = pl.BlockSpec((None, n1, sb, GROUP_W), lambda b, j: (b, 0, j, 0))
    tw = pl.BlockSpec((nsub, n1 * SUBLANES, 1), lambda b, j: (j, 0, 0))
    br, bi = pl.pallas_call(
        _four1_kernel,
        grid=(B, n2 // sb),
        in_specs=[blk1, _const_spec((FOURIER_W, 2 * FOURIER_W)),
                  _const_spec((2 * n1 * SUBLANES, 2 * n1 * SUBLANES)), tw, tw],
        out_specs=[blk1, blk1],
        out_shape=[jax.ShapeDtypeStruct((B, n1, n2, GROUP_W), F32)] * 2,
        compiler_params=_cparams("parallel", "parallel"),
        name="four1",
    )(fa4, ft["chan"], ft["m1"], ft["twr"], ft["twi"])
    blk_in = pl.BlockSpec((None, SUBLANES, n2, GROUP_W), lambda b, j: (b, j, 0, 0))
    blk_out = pl.BlockSpec((None, n2, SUBLANES, GROUP_W), lambda b, j: (b, 0, j, 0))
    o = pl.pallas_call(
        _four2_kernel,
        grid=(B, n1 // SUBLANES),
        in_specs=[blk_in, blk_in, _const_spec((SUBLANES * n2, 2 * SUBLANES * n2)),
                  _const_spec((N_FOURIER_GROUPS, FOURIER_W, FOURIER_W)), blk_out],
        out_specs=blk_out,
        out_shape=jax.ShapeDtypeStruct((B, n2, n1, GROUP_W), F32),
        compiler_params=_cparams("parallel", "parallel"),
        name="four2",
    )(br, bi, ft["m2"], wf, fg.reshape(B, n2, n1, GROUP_W))
    return o.reshape(B, S, GROUP_W)


def _outproj_kernel(ot_ref, ag_ref, of_ref, os_ref, om_ref, x_ref, w_ref, g_ref, y_ref):
    for s in range(ot_ref.shape[0] // OUT_SUB_TILES):
        rows = slice(s * OUT_SUB_TILES * Q_TILE, (s + 1) * OUT_SUB_TILES * Q_TILE)
        tiles = []
        for t in range(s * OUT_SUB_TILES, (s + 1) * OUT_SUB_TILES):
            ot = ot_ref[t].astype(F32)
            slabs = []
            for j in range(GROUP_W // LANES):
                pair = jnp.concatenate([ot[:, (2 * j) * Q_TILE:(2 * j + 1) * Q_TILE],
                                        ot[:, (2 * j + 1) * Q_TILE:(2 * j + 2) * Q_TILE]], axis=0)
                slabs.append(pair.T)
            tiles.append(jnp.concatenate(slabs, axis=1))
        oa = (jnp.concatenate(tiles, axis=0) * _silu(ag_ref[rows, :].astype(F32))).astype(BF16)
        rest = jnp.concatenate([of_ref[rows, :].astype(BF16), os_ref[rows, :], om_ref[rows, :]],
                               axis=1)
        y = jnp.dot(rest, w_ref[GROUP_W:, :], preferred_element_type=F32)
        y = y + jnp.dot(oa, w_ref[:GROUP_W, :], preferred_element_type=F32)
        y_ref[rows, :] = x_ref[rows, :] + _rms(y, g_ref[...])


def _outproj(ot, ag, of, osg, om, x, w, g):
    B, S, _ = x.shape
    tm = TOKEN_TILE
    tok = lambda wd: pl.BlockSpec((None, tm, wd), lambda b, i: (b, i, 0))
    return pl.pallas_call(
        _outproj_kernel,
        grid=(B, S // tm),
        in_specs=[pl.BlockSpec((None, tm // Q_TILE, HEAD_DIM, Q_ROWS), lambda b, i: (b, i, 0, 0)),
                  tok(GROUP_W), tok(GROUP_W), tok(GROUP_W), tok(GROUP_W), tok(D_MODEL),
                  _const_spec((4 * GROUP_W, D_MODEL)), _const_spec((1, D_MODEL))],
        out_specs=tok(D_MODEL),
        out_shape=jax.ShapeDtypeStruct((B, S, D_MODEL), F32),
        compiler_params=_cparams("parallel", "parallel"),
        name="outproj",
    )(ot, ag, of, osg, om, x, w, g)


def _rope_tables(S):
    rows = S // GRID_W
    row = jnp.broadcast_to(jnp.arange(rows, dtype=F32)[:, None], (rows, GRID_W)).reshape(S)
    col = jnp.broadcast_to(jnp.arange(GRID_W, dtype=F32)[None, :], (rows, GRID_W)).reshape(S)
    inv = ROPE_THETA ** (-jnp.arange(ROPE_PAIRS, dtype=F32) / ROPE_PAIRS)
    ang = jnp.stack([row[:, None] * inv, col[:, None] * inv], axis=1)
    cos, sin = jnp.cos(ang), jnp.sin(ang)
    zero = jnp.zeros((S, ROPE_PAIRS), F32)
    two = lambda parts: jnp.tile(jnp.concatenate(parts, axis=-1), (1, LANES // HEAD_DIM))
    return dict(cos=two([cos[:, 0], cos[:, 0], cos[:, 1], cos[:, 1]]),
                sina=two([zero, sin[:, 0], zero, sin[:, 1]]),
                sinb=two([-sin[:, 0], zero, -sin[:, 1], zero]))


def _layer_weights(l, pre_norm_g, w_in, q_norm_g, k_norm_g, w_fourier, sgu_norm_g, w_spatial,
                   b_spatial, mem_norm_g, w_mem_kv, w_out, post_norm_g):
    return dict(
        pre_g=pre_norm_g[l][None, :],
        w_in=w_in[l].astype(BF16),
        qg=jnp.tile(q_norm_g[l], N_HEADS)[None, :],
        kg=jnp.tile(k_norm_g[l], N_KV_HEADS)[None, :],
        wf=w_fourier[l].astype(BF16),
        vg=sgu_norm_g[l].reshape(1, GROUP_W),
        ws=w_spatial[l].astype(BF16),
        bs=jnp.repeat(b_spatial[l].T, SGU_W, axis=1),
        mem_g=mem_norm_g[l][None, :],
        w_mem_kv=w_mem_kv[l].astype(BF16),
        w_out=w_out[l].astype(BF16),
        post_g=post_norm_g[l][None, :],
        score_bound=(HEAD_DIM * Q_SCALE * jnp.max(jnp.abs(q_norm_g[l]))
                     * jnp.max(jnp.abs(k_norm_g[l]))),
    )


def _trunk(x, mem, layers):
    S = x.shape[1]
    tabs = _rope_tables(S)
    ids = np.arange(GROUP_W) // HEAD_DIM
    tabs["bd"] = jnp.asarray(ids[:, None] == ids[None, :], BF16)
    ft = _dft_tables(S)
    for lw in layers:
        kv = _memkv(mem, lw["mem_g"], lw["w_mem_kv"])
        qt, k, vt, ag, fa, fg, osgu, omem = _inproj(x, kv, lw, tabs)
        ot = lax.cond(lw["score_bound"] < ATTN_UNSHIFTED_MAX_LOG2,
                      functools.partial(_attention, bounded=True),
                      functools.partial(_attention, bounded=False), qt, k, vt)
        ofour = _fourier(fa, fg, lw["wf"], ft)
        x = _outproj(ot, ag, ofour, osgu, omem, x, lw["w_out"], lw["post_g"])
    return x


def kernel(x_prompt, x_sample, mem_prompt, mem_sample, pre_norm_g, w_in, q_norm_g, k_norm_g,
           w_fourier, sgu_norm_g, w_spatial, b_spatial, mem_norm_g, w_mem_kv, w_out, post_norm_g):
    layers = [_layer_weights(l, pre_norm_g, w_in, q_norm_g, k_norm_g, w_fourier, sgu_norm_g,
                             w_spatial, b_spatial, mem_norm_g, w_mem_kv, w_out, post_norm_g)
              for l in range(DEPTH)]
    return (_trunk(x_prompt, mem_prompt, layers), _trunk(x_sample, mem_sample, layers))
```

```python
import functools

import numpy as np
import jax
import jax.numpy as jnp
from jax import lax
from jax.experimental import pallas as pl
from jax.experimental.pallas import tpu as pltpu

F32 = jnp.float32
BF16 = jnp.bfloat16

D_MODEL = 1024
DEPTH = 2
GRID_W = 64
N_MEM = 256
GROUP_W = 512
HEAD_DIM = 64
N_HEADS = 8
N_KV_HEADS = 2
Q_PER_KV = N_HEADS // N_KV_HEADS
ROPE_PAIRS = HEAD_DIM // 4
ROPE_THETA = 10000.0
N_FOURIER_GROUPS = 4
FOURIER_W = 128
N_SGU_HEADS = 4
SGU_W = 128
CHUNK = 128
N_MEM_HEADS = 4
MEM_HEAD_DIM = 128
EPS = 1e-6
IN_W = 4864
C_AQ, C_AK, C_AV, C_AG, C_FA, C_FG, C_SU, C_SV, C_SG, C_MQ, C_MG = (
    0, 512, 640, 768, 1280, 1792, 2304, 2816, 3328, 3840, 4352)

LANES = 128
SUBLANES = 8
MXU_COLS = 256
VMEM_LIMIT_BYTES = 56 * 1024 * 1024

TOKEN_TILE = 512
KV_CHUNK = 512
Q_TILE = 128
Q_ROWS = N_HEADS * Q_TILE
DFT_N2 = 128
FOURIER_FUSED_MAX_SEQ = 2048
FOURIER_UNROLL = 8
ATTN_UNROLL = 14
OUT_TILE = 1024
OUT_SUB_TILES = 2
ATTN_UNSHIFTED_MAX_LOG2 = 64.0
ATTN_TILES_PER_STEP = 8
BF16_SUBLANES = 16
VT_ROWS = HEAD_DIM + BF16_SUBLANES
Q_COLS_PER_KV = Q_PER_KV * Q_TILE
Q_SCALE = float(HEAD_DIM ** -0.5 * np.log2(np.e))


def _cparams(*sem):
    return pltpu.CompilerParams(dimension_semantics=sem, vmem_limit_bytes=VMEM_LIMIT_BYTES)


def _const_spec(shape):
    nd = len(shape)
    return pl.BlockSpec(shape, lambda *_: (0,) * nd)


def _silu(g):
    return g / (1.0 + jnp.exp(-g))


def _rms(x, g):
    ms = jnp.mean(x * x, axis=-1, keepdims=True)
    return x * lax.rsqrt(ms + EPS) * g


def _memkv_kernel(mem_ref, g_ref, w_ref, kv_ref):
    h = _rms(mem_ref[...], g_ref[...]).astype(BF16)
    kv_ref[...] = jnp.dot(h, w_ref[...], preferred_element_type=F32).astype(BF16)


def _memkv(mem, g, w):
    B = mem.shape[0]
    return pl.pallas_call(
        _memkv_kernel,
        grid=(B,),
        in_specs=[pl.BlockSpec((None, N_MEM, D_MODEL), lambda b: (b, 0, 0)),
                  _const_spec((1, D_MODEL)),
                  _const_spec((D_MODEL, 2 * GROUP_W))],
        out_specs=pl.BlockSpec((None, N_MEM, 2 * GROUP_W), lambda b: (b, 0, 0)),
        out_shape=jax.ShapeDtypeStruct((B, N_MEM, 2 * GROUP_W), BF16),
        compiler_params=_cparams("parallel"),
        name="memkv",
    )(mem, g, w)


def _head_ssq(x, bd):
    return jnp.dot((x * x).astype(BF16), bd, preferred_element_type=F32)


def _rope(x, cos, sina, sinb):
    return x * cos + pltpu.roll(x, 16, 1) * sina + pltpu.roll(x, LANES - 16, 1) * sinb


def _inproj_kernel(x_ref, pre_g_ref, w_ref, bd_ref, qg_ref, kg_ref, cos_ref, sina_ref, sinb_ref,
                   vg_ref, ws_ref, bs_ref, kv_ref,
                   qt_ref, k_ref, vt_ref, ag_ref, fa_ref, fg_ref, osgu_ref, omem_ref):
    tm = x_ref.shape[0]
    h = _rms(x_ref[...], pre_g_ref[...]).astype(BF16)

    def proj(lo, width):
        return jnp.dot(h, w_ref[:, lo:lo + width], preferred_element_type=F32)

    cos, sina, sinb = cos_ref[...], sina_ref[...], sinb_ref[...]
    nck = tm // CHUNK
    lanes = lambda i: slice(i * LANES, (i + 1) * LANES)

    aq = proj(C_AQ, GROUP_W)
    ak = proj(C_AK, LANES)
    av = proj(C_AV, LANES)
    mq = proj(C_MQ, GROUP_W)
    sv = proj(C_SV, GROUP_W)
    ssq = _head_ssq(aq, bd_ref[...])
    ssk = _head_ssq(ak, bd_ref[0:LANES, 0:LANES])

    mq16 = mq.astype(BF16)
    scores = [lax.dot_general(mq16[:, lanes(hd)], kv_ref[:, lanes(hd)], (((1,), (1,)), ((), ())),
                              preferred_element_type=F32) * (MEM_HEAD_DIM ** -0.5)
              for hd in range(N_MEM_HEADS)]

    su = proj(C_SU, GROUP_W)
    sg = proj(C_SG, GROUP_W)
    mg = proj(C_MG, GROUP_W)

    vhs = [_rms(sv[:, lanes(hd)], vg_ref[:, lanes(hd)]).astype(BF16) for hd in range(N_SGU_HEADS)]
    spcs = [jnp.dot(ws_ref[hd],
                    jnp.concatenate([vhs[hd][c * CHUNK:(c + 1) * CHUNK, :] for c in range(nck)],
                                    axis=1), preferred_element_type=F32)
            for hd in range(N_SGU_HEADS)]

    ag_ref[...] = proj(C_AG, GROUP_W).astype(BF16)
    fa_ref[...] = proj(C_FA, GROUP_W).astype(fa_ref.dtype)
    fg_ref[...] = proj(C_FG, GROUP_W).astype(fg_ref.dtype)

    es = [jnp.exp(s - jnp.max(s, axis=-1, keepdims=True)) for s in scores]
    mgate = _silu(mg)
    for hd in range(N_MEM_HEADS):
        mv = kv_ref[:, GROUP_W + hd * MEM_HEAD_DIM:GROUP_W + (hd + 1) * MEM_HEAD_DIM]
        o = jnp.dot(es[hd].astype(BF16), mv, preferred_element_type=F32)
        o = o / jnp.sum(es[hd], axis=-1, keepdims=True)
        omem_ref[:, lanes(hd)] = (o * mgate[:, lanes(hd)]).astype(BF16)

    qn = aq * lax.rsqrt(ssq * (1.0 / HEAD_DIM) + EPS) * qg_ref[...]
    slabs = [_rope(qn[:, lanes(j)], cos, sina, sinb) * Q_SCALE for j in range(GROUP_W // LANES)]
    low = lax.broadcasted_iota(jnp.int32, (Q_TILE, LANES), 1) < HEAD_DIM
    for t in range(tm // Q_TILE):
        parts = []
        for j, slab in enumerate(slabs):
            sj = slab[t * Q_TILE:(t + 1) * Q_TILE, :]
            rolled = pltpu.roll(sj, HEAD_DIM, 1)
            if j < Q_PER_KV // 2:
                parts += [jnp.where(low, sj, 0.0), jnp.where(low, rolled, 0.0)]
            else:
                parts += [jnp.where(low, 0.0, rolled), jnp.where(low, 0.0, sj)]
        qt_ref[t] = jnp.concatenate(parts, axis=0).T.astype(BF16)

    kn = ak * lax.rsqrt(ssk * (1.0 / HEAD_DIM) + EPS) * kg_ref[...]
    k_ref[...] = _rope(kn, cos, sina, sinb).astype(BF16)
    avt = av.T.astype(BF16)
    for g in range(N_KV_HEADS):
        vt_ref[g * VT_ROWS:g * VT_ROWS + HEAD_DIM, :] = avt[g * HEAD_DIM:(g + 1) * HEAD_DIM, :]
        vt_ref[g * VT_ROWS + HEAD_DIM:(g + 1) * VT_ROWS, :] = jnp.ones((BF16_SUBLANES, tm), BF16)

    gate = _silu(sg)
    for hd in range(N_SGU_HEADS):
        for c in range(nck):
            rows = slice(c * CHUNK, (c + 1) * CHUNK)
            sp = spcs[hd][:, c * SGU_W:(c + 1) * SGU_W] + bs_ref[:, lanes(hd)]
            osgu_ref[rows, lanes(hd)] = (su[rows, lanes(hd)] * sp
                                         * gate[rows, lanes(hd)]).astype(BF16)


def _inproj(x, kv, lw, tabs):
    B, S, _ = x.shape
    tm = TOKEN_TILE
    nt = S // tm
    tok = lambda w: pl.BlockSpec((None, tm, w), lambda b, i: (b, i, 0))
    pos = pl.BlockSpec((tm, LANES), lambda b, i: (i, 0))
    in_specs = [
        tok(D_MODEL),
        _const_spec((1, D_MODEL)),
        _const_spec((D_MODEL, IN_W)),
        _const_spec((GROUP_W, GROUP_W)),
        _const_spec((1, GROUP_W)),
        _const_spec((1, LANES)),
        pos, pos, pos,
        _const_spec((1, GROUP_W)),
        _const_spec((N_SGU_HEADS, CHUNK, CHUNK)),
        _const_spec((CHUNK, GROUP_W)),
        pl.BlockSpec((None, N_MEM, 2 * GROUP_W), lambda b, i: (b, 0, 0)),
    ]
    out_specs = [
        pl.BlockSpec((None, tm // Q_TILE, LANES, Q_ROWS), lambda b, i: (b, i, 0, 0)), tok(LANES),
        pl.BlockSpec((None, None, N_KV_HEADS * VT_ROWS, tm), lambda b, i: (b, i, 0, 0)),
        tok(GROUP_W), tok(GROUP_W), tok(GROUP_W), tok(GROUP_W), tok(GROUP_W),
    ]
    sds = jax.ShapeDtypeStruct
    four_dt = BF16 if S <= FOURIER_FUSED_MAX_SEQ else F32
    out_shape = [
        sds((B, S // Q_TILE, LANES, Q_ROWS), BF16), sds((B, S, LANES), BF16),
        sds((B, nt, N_KV_HEADS * VT_ROWS, tm), BF16),
        sds((B, S, GROUP_W), BF16), sds((B, S, GROUP_W), four_dt), sds((B, S, GROUP_W), four_dt),
        sds((B, S, GROUP_W), BF16), sds((B, S, GROUP_W), BF16),
    ]
    return pl.pallas_call(
        _inproj_kernel,
        grid=(B, nt),
        in_specs=in_specs, out_specs=out_specs, out_shape=out_shape,
        compiler_params=_cparams("parallel", "parallel"),
        name="inproj",
    )(x, lw["pre_g"], lw["w_in"], tabs["bd"], lw["qg"], lw["kg"],
      tabs["cos"], tabs["sina"], tabs["sinb"], lw["vg"], lw["ws"], lw["bs"], kv)


def _attn_kernel(qt_ref, k_ref, vt_ref, ot_ref, *scratch, bounded):
    ntiles = qt_ref.shape[0]
    nchunks, _, tk = vt_ref.shape
    neg_inf = jnp.full((1, qt_ref.shape[2]), -jnp.inf, F32)

    if bounded:
        p_ref, acc_ref = scratch
    else:
        st_ref, mx_ref, acc_ref = scratch

    def scores(t, c, slot):
        start = c * tk if isinstance(c, int) else pl.multiple_of(c * tk, tk)
        st = jnp.dot(k_ref[pl.ds(start, tk), :], qt_ref[t], preferred_element_type=F32)
        if bounded:
            p_ref[slot] = jnp.exp2(st).astype(BF16)
        else:
            st_ref[slot] = st
            mx_ref[slot] = jnp.max(st, axis=0, keepdims=True)

    def softmax_pv(t, c, slot, m_old, first=False):
        if bounded:
            p, m_new, alpha = p_ref[slot], m_old, None
        else:
            m_new = mx_ref[slot]
            if not first:
                m_new = jnp.maximum(m_old, m_new)
            p = jnp.exp2(st_ref[slot] - m_new).astype(BF16)
            alpha = None if first else jnp.exp2(m_old - m_new)
        for g in range(N_KV_HEADS):
            cols = slice(g * Q_COLS_PER_KV, (g + 1) * Q_COLS_PER_KV)
            pv = jnp.dot(vt_ref[c, g * VT_ROWS:(g + 1) * VT_ROWS, :], p[:, cols],
                         preferred_element_type=F32)
            if first:
                acc_ref[t, g] = pv
            elif bounded:
                acc_ref[t, g] += pv
            else:
                acc_ref[t, g] = alpha[:, cols] * acc_ref[t, g] + pv
        return m_new

    def step_pair(nxt, t, c, slot, m_old, first=False):
        if not bounded or nxt is None:
            if nxt is not None:
                scores(*nxt)
            return softmax_pv(t, c, slot, m_old, first)
        tn, cn, sn = nxt
        start = cn * tk if isinstance(cn, int) else pl.multiple_of(cn * tk, tk)
        kc = k_ref[pl.ds(start, tk), :]
        for i in range(Q_ROWS // MXU_COLS):
            cols = slice(i * MXU_COLS, (i + 1) * MXU_COLS)
            st = jnp.dot(kc, qt_ref[tn, :, cols], preferred_element_type=F32)
            p_ref[sn, :, cols] = jnp.exp2(st).astype(BF16)
            g, off = divmod(i * MXU_COLS, Q_COLS_PER_KV)
            pv = jnp.dot(vt_ref[c, g * VT_ROWS:(g + 1) * VT_ROWS, :], p_ref[slot, :, cols],
                         preferred_element_type=F32)
            if first:
                acc_ref[t, g, :, off:off + MXU_COLS] = pv
            else:
                acc_ref[t, g, :, off:off + MXU_COLS] += pv
        return m_old

    def finish(t):
        for g in range(N_KV_HEADS):
            acc = acc_ref[t, g]
            ot_ref[t, :, g * Q_COLS_PER_KV:(g + 1) * Q_COLS_PER_KV] = (
                acc[:HEAD_DIM] / acc[HEAD_DIM:HEAD_DIM + 1]).astype(BF16)

    if nchunks <= ATTN_UNROLL:
        assert nchunks % 2 == 0

        def tile_steps(t, nxt):
            m = None
            for c in range(nchunks):
                if c + 1 < nchunks:
                    ahead = (t, c + 1, (c + 1) % 2)
                else:
                    ahead = None if nxt is None else (nxt, 0, 0)
                m = step_pair(ahead, t, c, c % 2, m, first=(c == 0))
            finish(t)

        scores(0, 0, 0)
        if ntiles > 1:
            def tile(t, carry):
                tile_steps(t, t + 1)
                return carry
            lax.fori_loop(0, ntiles - 1, tile, 0, unroll=True)
        tile_steps(ntiles - 1, None)
    else:
        assert ntiles == 1
        steady = nchunks - 1
        groups = (steady - 1) // ATTN_UNROLL
        peeled = steady - groups * ATTN_UNROLL

        def step(c, slot, m, first=False):
            return step_pair((0, c + 1, 1 - slot), 0, c, slot, m, first)

        scores(0, 0, 0)
        m = neg_inf
        for c in range(peeled):
            m = step(c, c % 2, m, first=(c == 0))

        def group(g, m):
            for u in range(ATTN_UNROLL):
                m = step(peeled + g * ATTN_UNROLL + u, (peeled + u) % 2, m)
            return m
        m = lax.fori_loop(0, groups, group, m)
        softmax_pv(0, nchunks - 1, (nchunks - 1) % 2, m)
        finish(0)


def _attention(qt, k, vt, bounded):
    B, nq = qt.shape[0], qt.shape[1]
    S = k.shape[1]
    tk = vt.shape[3]
    ntiles = min(nq, ATTN_TILES_PER_STEP) if S // tk <= ATTN_UNROLL else 1
    qspec = pl.BlockSpec((None, ntiles, LANES, Q_ROWS), lambda b, i: (b, i, 0, 0))
    if bounded:
        buffers = [pltpu.VMEM((2, tk, Q_ROWS), BF16)]
    else:
        buffers = [pltpu.VMEM((2, tk, Q_ROWS), F32), pltpu.VMEM((2, 1, Q_ROWS), F32)]
    return pl.pallas_call(
        functools.partial(_attn_kernel, bounded=bounded),
        grid=(B, nq // ntiles),
        in_specs=[qspec,
                  pl.BlockSpec((None, S, LANES), lambda b, i: (b, 0, 0)),
                  pl.BlockSpec((None,) + vt.shape[1:], lambda b, i: (b, 0, 0, 0))],
        out_specs=pl.BlockSpec((None, ntiles, HEAD_DIM, Q_ROWS), lambda b, i: (b, i, 0, 0)),
        out_shape=jax.ShapeDtypeStruct((B, nq, HEAD_DIM, Q_ROWS), BF16),
        scratch_shapes=buffers + [pltpu.VMEM((ntiles, N_KV_HEADS, VT_ROWS, Q_COLS_PER_KV), F32)],
        compiler_params=_cparams("parallel", "parallel"),
        name="attn_bounded" if bounded else "attn",
    )(qt, k, vt)


def _dft_tables(S):
    n2 = DFT_N2
    n1 = S // n2
    c = np.arange(FOURIER_W)
    ang = 2.0 * np.pi * np.outer(c, c) / FOURIER_W
    chan = np.concatenate([np.cos(ang), -np.sin(ang)], axis=1) / np.sqrt(FOURIER_W)
    a1 = 2.0 * np.pi * np.outer(np.arange(n1), np.arange(n1)) / n1
    eye = np.eye(SUBLANES)
    mr = np.kron(np.cos(a1), eye) / np.sqrt(n1)
    mi = np.kron(-np.sin(a1), eye) / np.sqrt(n1)
    m1 = np.block([[mr, -mi], [mi, mr]])
    k1 = np.arange(n1)[None, :, None]
    s2 = (np.arange(n2 // SUBLANES)[:, None, None] * SUBLANES + np.arange(SUBLANES)[None, None, :])
    at = 2.0 * np.pi * (k1 * s2) / S
    twr = np.cos(at).reshape(n2 // SUBLANES, n1 * SUBLANES, 1)
    twi = (-np.sin(at)).reshape(n2 // SUBLANES, n1 * SUBLANES, 1)
    a2 = 2.0 * np.pi * np.outer(np.arange(n2), np.arange(n2)) / n2
    c2 = (np.cos(a2) / np.sqrt(n2))[:, None, None, :] * eye[None, :, :, None]
    s2m = (np.sin(a2) / np.sqrt(n2))[:, None, None, :] * eye[None, :, :, None]
    m2 = np.concatenate([c2.reshape(n2 * SUBLANES, n2 * SUBLANES),
                         s2m.reshape(n2 * SUBLANES, n2 * SUBLANES)], axis=1)
    c2s2 = np.concatenate([np.cos(a2), np.sin(a2)], axis=1) / np.sqrt(n2)
    return dict(chan=jnp.asarray(chan, BF16), m1=jnp.asarray(m1, BF16),
                twr=jnp.asarray(twr, F32), twi=jnp.asarray(twi, F32), m2=jnp.asarray(m2, BF16),
                c2=jnp.asarray(c2s2, BF16))


def _four1_kernel(fa_ref, chan_ref, m1_ref, twr_ref, twi_ref, br_ref, bi_ref):
    n1, sb, _ = fa_ref.shape
    rows = n1 * SUBLANES
    for i in range(sb // SUBLANES):
        rs = slice(i * SUBLANES, (i + 1) * SUBLANES)
        x = fa_ref[:, rs, :].reshape(rows, GROUP_W).astype(BF16)
        zr, zi = [], []
        for g in range(N_FOURIER_GROUPS):
            z = jnp.dot(x[:, g * FOURIER_W:(g + 1) * FOURIER_W], chan_ref[...],
                        preferred_element_type=F32)
            zr.append(z[:, :FOURIER_W])
            zi.append(z[:, FOURIER_W:])
        zcat = jnp.concatenate([jnp.concatenate(zr, axis=1), jnp.concatenate(zi, axis=1)],
                               axis=0).astype(BF16)
        a = jnp.dot(m1_ref[...], zcat, preferred_element_type=F32)
        ar, ai = a[:rows], a[rows:]
        twr, twi = twr_ref[i], twi_ref[i]
        br_ref[:, rs, :] = (ar * twr - ai * twi).reshape(n1, SUBLANES, GROUP_W)
        bi_ref[:, rs, :] = (ar * twi + ai * twr).reshape(n1, SUBLANES, GROUP_W)


def _four2_kernel(br_ref, bi_ref, m2_ref, wf_ref, fg_ref, o_ref):
    _, n2, _ = br_ref.shape
    rows = SUBLANES * n2
    bcat = jnp.concatenate([br_ref[...].reshape(rows, GROUP_W), bi_ref[...].reshape(rows, GROUP_W)],
                           axis=0).astype(BF16)
    f = jnp.dot(m2_ref[...], bcat, preferred_element_type=F32).astype(BF16)
    gate = _silu(fg_ref[...].reshape(rows, GROUP_W))
    ys = [jnp.dot(f[:, g * FOURIER_W:(g + 1) * FOURIER_W], wf_ref[g], preferred_element_type=F32)
          for g in range(N_FOURIER_GROUPS)]
    y = jnp.concatenate(ys, axis=1) * gate
    o_ref[...] = y.reshape(n2, SUBLANES, GROUP_W)


def _four_fused_kernel(fa_ref, fg_ref, chan_ref, m1_ref, twr_ref, twi_ref, c2_ref, wf_ref, o_ref,
                       zr_ref, zi_ref, br_ref, bi_ref, f_ref):
    S = fa_ref.shape[0]
    n2 = DFT_N2
    n1 = S // n2
    lanes = lambda g: slice(g * FOURIER_W, (g + 1) * FOURIER_W)

    x = fa_ref[...].astype(BF16)
    for g in range(N_FOURIER_GROUPS):
        z = jnp.dot(x[:, lanes(g)], chan_ref[...], preferred_element_type=F32)
        zr_ref[:, lanes(g)] = z[:, :FOURIER_W]
        zi_ref[:, lanes(g)] = z[:, FOURIER_W:]

    def stage1(j, carry):
        base = pl.multiple_of(j * SUBLANES, SUBLANES)
        tiles = ([zr_ref[pl.ds(s1 * n2 + base, SUBLANES), :] for s1 in range(n1)]
                 + [zi_ref[pl.ds(s1 * n2 + base, SUBLANES), :] for s1 in range(n1)])
        a = jnp.dot(m1_ref[...], jnp.concatenate(tiles, axis=0).astype(BF16),
                    preferred_element_type=F32)
        ar, ai = a[:n1 * SUBLANES], a[n1 * SUBLANES:]
        twr, twi = twr_ref[j], twi_ref[j]
        br = ar * twr - ai * twi
        bi = ar * twi + ai * twr
        for k1 in range(n1):
            rows = slice(k1 * SUBLANES, (k1 + 1) * SUBLANES)
            br_ref[pl.ds(k1 * n2 + base, SUBLANES), :] = br[rows]
            bi_ref[pl.ds(k1 * n2 + base, SUBLANES), :] = bi[rows]
        return carry

    lax.fori_loop(0, n2 // SUBLANES, stage1, 0, unroll=FOURIER_UNROLL)

    def stage2(k1, carry):
        start = pl.multiple_of(k1 * n2, n2)
        bcat = jnp.concatenate([br_ref[pl.ds(start, n2), :], bi_ref[pl.ds(start, n2), :]],
                               axis=0).astype(BF16)
        f = jnp.dot(c2_ref[...], bcat, preferred_element_type=F32)
        for g in range(N_FOURIER_GROUPS):
            f_ref[g, pl.ds(k1, n2, stride=n1), :] = f[:, lanes(g)]
        return carry

    lax.fori_loop(0, n1, stage2, 0, unroll=FOURIER_UNROLL)

    gate = _silu(fg_ref[...].astype(F32))
    for g in range(N_FOURIER_GROUPS):
        y = jnp.dot(f_ref[g].astype(BF16), wf_ref[g], preferred_element_type=F32)
        o_ref[:, lanes(g)] = (y * gate[:, lanes(g)]).astype(o_ref.dtype)


def _fourier_fused(fa, fg, wf, ft):
    B, S, _ = fa.shape
    n1 = S // DFT_N2
    tok = pl.BlockSpec((None, S, GROUP_W), lambda b: (b, 0, 0))
    return pl.pallas_call(
        _four_fused_kernel,
        grid=(B,),
        in_specs=[tok, tok, _const_spec((FOURIER_W, 2 * FOURIER_W)),
                  _const_spec((2 * n1 * SUBLANES, 2 * n1 * SUBLANES)),
                  _const_spec(ft["twr"].shape), _const_spec(ft["twi"].shape),
                  _const_spec((DFT_N2, 2 * DFT_N2)),
                  _const_spec((N_FOURIER_GROUPS, FOURIER_W, FOURIER_W))],
        out_specs=tok,
        out_shape=jax.ShapeDtypeStruct((B, S, GROUP_W), BF16),
        scratch_shapes=[pltpu.VMEM((S, GROUP_W), F32)] * 4
                       + [pltpu.VMEM((N_FOURIER_GROUPS, S, FOURIER_W), F32)],
        compiler_params=_cparams("parallel"),
        name="four",
    )(fa, fg, ft["chan"], ft["m1"], ft["twr"], ft["twi"], ft["c2"], wf)


def _fourier(fa, fg, wf, ft):
    B, S, _ = fa.shape
    if S <= FOURIER_FUSED_MAX_SEQ:
        return _fourier_fused(fa, fg, wf, ft)
    n2 = DFT_N2
    n1 = S // n2
    sb = 32 if n1 <= 16 else SUBLANES
    nsub = sb // SUBLANES
    fa4 = fa.reshape(B, n1, n2, GROUP_W)
    blk1 = pl.BlockSpec((None, n1, sb, GROUP_W), lambda b, j: (b, 0, j, 0))
    tw = pl.BlockSpec((nsub, n1 * SUBLANES, 1), lambda b, j: (j, 0, 0))
    br, bi = pl.pallas_call(
        _four1_kernel,
        grid=(B, n2 // sb),
        in_specs=[blk1, _const_spec((FOURIER_W, 2 * FOURIER_W)),
                  _const_spec((2 * n1 * SUBLANES, 2 * n1 * SUBLANES)), tw, tw],
        out_specs=[blk1, blk1],
        out_shape=[jax.ShapeDtypeStruct((B, n1, n2, GROUP_W), F32)] * 2,
        compiler_params=_cparams("parallel", "parallel"),
        name="four1",
    )(fa4, ft["chan"], ft["m1"], ft["twr"], ft["twi"])
    blk_in = pl.BlockSpec((None, SUBLANES, n2, GROUP_W), lambda b, j: (b, j, 0, 0))
    blk_out = pl.BlockSpec((None, n2, SUBLANES, GROUP_W), lambda b, j: (b, 0, j, 0))
    o = pl.pallas_call(
        _four2_kernel,
        grid=(B, n1 // SUBLANES),
        in_specs=[blk_in, blk_in, _const_spec((SUBLANES * n2, 2 * SUBLANES * n2)),
                  _const_spec((N_FOURIER_GROUPS, FOURIER_W, FOURIER_W)), blk_out],
        out_specs=blk_out,
        out_shape=jax.ShapeDtypeStruct((B, n2, n1, GROUP_W), F32),
        compiler_params=_cparams("parallel", "parallel"),
        name="four2",
    )(br, bi, ft["m2"], wf, fg.reshape(B, n2, n1, GROUP_W))
    return o.reshape(B, S, GROUP_W)


def _outproj_kernel(ot_ref, ag_ref, of_ref, os_ref, om_ref, x_ref, w_ref, g_ref, y_ref):
    for s in range(ot_ref.shape[0] // OUT_SUB_TILES):
        rows = slice(s * OUT_SUB_TILES * Q_TILE, (s + 1) * OUT_SUB_TILES * Q_TILE)
        tiles = []
        for t in range(s * OUT_SUB_TILES, (s + 1) * OUT_SUB_TILES):
            ot = ot_ref[t].astype(F32)
            slabs = []
            for j in range(GROUP_W // LANES):
                pair = jnp.concatenate([ot[:, (2 * j) * Q_TILE:(2 * j + 1) * Q_TILE],
                                        ot[:, (2 * j + 1) * Q_TILE:(2 * j + 2) * Q_TILE]], axis=0)
                slabs.append(pair.T)
            tiles.append(jnp.concatenate(slabs, axis=1))
        oa = (jnp.concatenate(tiles, axis=0) * _silu(ag_ref[rows, :].astype(F32))).astype(BF16)
        rest = jnp.concatenate([of_ref[rows, :].astype(BF16), os_ref[rows, :], om_ref[rows, :]],
                               axis=1)
        y = jnp.dot(rest, w_ref[GROUP_W:, :], preferred_element_type=F32)
        y = y + jnp.dot(oa, w_ref[:GROUP_W, :], preferred_element_type=F32)
        y_ref[rows, :] = x_ref[rows, :] + _rms(y, g_ref[...])


def _outproj(ot, ag, of, osg, om, x, w, g):
    B, S, _ = x.shape
    tm = OUT_TILE
    tok = lambda wd: pl.BlockSpec((None, tm, wd), lambda b, i: (b, i, 0))
    return pl.pallas_call(
        _outproj_kernel,
        grid=(B, S // tm),
        in_specs=[pl.BlockSpec((None, tm // Q_TILE, HEAD_DIM, Q_ROWS), lambda b, i: (b, i, 0, 0)),
                  tok(GROUP_W), tok(GROUP_W), tok(GROUP_W), tok(GROUP_W), tok(D_MODEL),
                  _const_spec((4 * GROUP_W, D_MODEL)), _const_spec((1, D_MODEL))],
        out_specs=tok(D_MODEL),
        out_shape=jax.ShapeDtypeStruct((B, S, D_MODEL), F32),
        compiler_params=_cparams("parallel", "parallel"),
        name="outproj",
    )(ot, ag, of, osg, om, x, w, g)


def _rope_tables(S):
    rows = S // GRID_W
    row = jnp.broadcast_to(jnp.arange(rows, dtype=F32)[:, None], (rows, GRID_W)).reshape(S)
    col = jnp.broadcast_to(jnp.arange(GRID_W, dtype=F32)[None, :], (rows, GRID_W)).reshape(S)
    inv = ROPE_THETA ** (-jnp.arange(ROPE_PAIRS, dtype=F32) / ROPE_PAIRS)
    ang = jnp.stack([row[:, None] * inv, col[:, None] * inv], axis=1)
    cos, sin = jnp.cos(ang), jnp.sin(ang)
    zero = jnp.zeros((S, ROPE_PAIRS), F32)
    two = lambda parts: jnp.tile(jnp.concatenate(parts, axis=-1), (1, LANES // HEAD_DIM))
    return dict(cos=two([cos[:, 0], cos[:, 0], cos[:, 1], cos[:, 1]]),
                sina=two([zero, sin[:, 0], zero, sin[:, 1]]),
                sinb=two([-sin[:, 0], zero, -sin[:, 1], zero]))


def _layer_weights(l, pre_norm_g, w_in, q_norm_g, k_norm_g, w_fourier, sgu_norm_g, w_spatial,
                   b_spatial, mem_norm_g, w_mem_kv, w_out, post_norm_g):
    return dict(
        pre_g=pre_norm_g[l][None, :],
        w_in=w_in[l].astype(BF16),
        qg=jnp.tile(q_norm_g[l], N_HEADS)[None, :],
        kg=jnp.tile(k_norm_g[l], N_KV_HEADS)[None, :],
        wf=w_fourier[l].astype(BF16),
        vg=sgu_norm_g[l].reshape(1, GROUP_W),
        ws=w_spatial[l].astype(BF16),
        bs=jnp.repeat(b_spatial[l].T, SGU_W, axis=1),
        mem_g=mem_norm_g[l][None, :],
        w_mem_kv=w_mem_kv[l].astype(BF16),
        w_out=w_out[l].astype(BF16),
        post_g=post_norm_g[l][None, :],
        score_bound=(HEAD_DIM * Q_SCALE * jnp.max(jnp.abs(q_norm_g[l]))
                     * jnp.max(jnp.abs(k_norm_g[l]))),
    )


def _trunk(x, mem, layers):
    S = x.shape[1]
    tabs = _rope_tables(S)
    ids = np.arange(GROUP_W) // HEAD_DIM
    tabs["bd"] = jnp.asarray(ids[:, None] == ids[None, :], BF16)
    ft = _dft_tables(S)
    for lw in layers:
        kv = _memkv(mem, lw["mem_g"], lw["w_mem_kv"])
        qt, k, vt, ag, fa, fg, osgu, omem = _inproj(x, kv, lw, tabs)
        ot = lax.cond(lw["score_bound"] < ATTN_UNSHIFTED_MAX_LOG2,
                      functools.partial(_attention, bounded=True),
                      functools.partial(_attention, bounded=False), qt, k, vt)
        ofour = _fourier(fa, fg, lw["wf"], ft)
        x = _outproj(ot, ag, ofour, osgu, omem, x, lw["w_out"], lw["post_g"])
    return x


def kernel(x_prompt, x_sample, mem_prompt, mem_sample, pre_norm_g, w_in, q_norm_g, k_norm_g,
           w_fourier, sgu_norm_g, w_spatial, b_spatial, mem_norm_g, w_mem_kv, w_out, post_norm_g):
    layers = [_layer_weights(l, pre_norm_g, w_in, q_norm_g, k_norm_g, w_fourier, sgu_norm_g,
                             w_spatial, b_spatial, mem_norm_g, w_mem_kv, w_out, post_norm_g)
              for l in range(DEPTH)]
    return (_trunk(x_prompt, mem_prompt, layers), _trunk(x_sample, mem_sample, layers))
```

```python
import functools

import numpy as np
import jax
import jax.numpy as jnp
from jax import lax
from jax.experimental import pallas as pl
from jax.experimental.pallas import tpu as pltpu

F32 = jnp.float32
BF16 = jnp.bfloat16

D_MODEL = 1024
DEPTH = 2
GRID_W = 64
N_MEM = 256
GROUP_W = 512
HEAD_DIM = 64
N_HEADS = 8
N_KV_HEADS = 2
Q_PER_KV = N_HEADS // N_KV_HEADS
ROPE_PAIRS = HEAD_DIM // 4
ROPE_THETA = 10000.0
N_FOURIER_GROUPS = 4
FOURIER_W = 128
N_SGU_HEADS = 4
SGU_W = 128
CHUNK = 128
N_MEM_HEADS = 4
MEM_HEAD_DIM = 128
EPS = 1e-6
IN_W = 4864
C_AQ, C_AK, C_AV, C_AG, C_FA, C_FG, C_SU, C_SV, C_SG, C_MQ, C_MG = (
    0, 512, 640, 768, 1280, 1792, 2304, 2816, 3328, 3840, 4352)

LANES = 128
SUBLANES = 8
MXU_COLS = 256
VMEM_LIMIT_BYTES = 56 * 1024 * 1024

TOKEN_TILE = 512
KV_CHUNK = 512
Q_TILE = 128
Q_ROWS = N_HEADS * Q_TILE
DFT_N2 = 128
FOURIER_FUSED_MAX_SEQ = 2048
ROUND_BLOCK_ROWS = 256
FOURIER_UNROLL = 8
ATTN_UNROLL = 14
OUT_TILE = 1024
OUT_SUB_TILES = 2
ATTN_UNSHIFTED_MAX_LOG2 = 64.0
ATTN_TILES_PER_STEP = 8
BF16_SUBLANES = 16
VT_ROWS = HEAD_DIM + BF16_SUBLANES
Q_COLS_PER_KV = Q_PER_KV * Q_TILE
Q_SCALE = float(HEAD_DIM ** -0.5 * np.log2(np.e))


def _cparams(*sem):
    return pltpu.CompilerParams(dimension_semantics=sem, vmem_limit_bytes=VMEM_LIMIT_BYTES)


def _const_spec(shape):
    nd = len(shape)
    return pl.BlockSpec(shape, lambda *_: (0,) * nd)


def _silu(g):
    return g / (1.0 + jnp.exp(-g))


def _rms(x, g):
    ms = jnp.mean(x * x, axis=-1, keepdims=True)
    return x * lax.rsqrt(ms + EPS) * g


def _memkv_kernel(mem_ref, g_ref, w_ref, kv_ref):
    h = _rms(mem_ref[...], g_ref[...]).astype(BF16)
    kv_ref[...] = jnp.dot(h, w_ref[...], preferred_element_type=F32).astype(BF16)


def _memkv(mem, g, w):
    B = mem.shape[0]
    return pl.pallas_call(
        _memkv_kernel,
        grid=(B,),
        in_specs=[pl.BlockSpec((None, N_MEM, D_MODEL), lambda b: (b, 0, 0)),
                  _const_spec((1, D_MODEL)),
                  _const_spec((D_MODEL, 2 * GROUP_W))],
        out_specs=pl.BlockSpec((None, N_MEM, 2 * GROUP_W), lambda b: (b, 0, 0)),
        out_shape=jax.ShapeDtypeStruct((B, N_MEM, 2 * GROUP_W), BF16),
        compiler_params=_cparams("parallel"),
        name="memkv",
    )(mem, g, w)


def _head_ssq(x, bd):
    sq = (x * x).astype(BF16)
    w = bd.shape[0]
    if x.shape[1] <= w:
        return jnp.dot(sq, bd[:x.shape[1], :x.shape[1]], preferred_element_type=F32)
    return jnp.concatenate([jnp.dot(sq[:, i:i + w], bd, preferred_element_type=F32)
                            for i in range(0, x.shape[1], w)], axis=1)


def _rope(x, cos, sina, sinb):
    return x * cos + pltpu.roll(x, 16, 1) * sina + pltpu.roll(x, LANES - 16, 1) * sinb


def _inproj_kernel(x_ref, pre_g_ref, w_ref, bd_ref, qg_ref, kg_ref, cos_ref, sina_ref, sinb_ref,
                   vg_ref, ws_ref, bs_ref, kv_ref,
                   qt_ref, k_ref, vt_ref, ag_ref, fa_ref, fg_ref, osgu_ref, omem_ref):
    tm = x_ref.shape[0]
    h = _rms(x_ref[...], pre_g_ref[...]).astype(BF16)

    def proj(lo, width):
        return jnp.dot(h, w_ref[:, lo:lo + width], preferred_element_type=F32)

    cos, sina, sinb = cos_ref[...], sina_ref[...], sinb_ref[...]
    nck = tm // CHUNK
    lanes = lambda i: slice(i * LANES, (i + 1) * LANES)

    aq = proj(C_AQ, GROUP_W)
    ak = proj(C_AK, LANES)
    av = proj(C_AV, LANES)
    mq = proj(C_MQ, GROUP_W)
    sv = proj(C_SV, GROUP_W)
    ssq = _head_ssq(aq, bd_ref[...])
    ssk = _head_ssq(ak, bd_ref[...])

    mq16 = mq.astype(BF16)
    scores = [lax.dot_general(mq16[:, lanes(hd)], kv_ref[:, lanes(hd)], (((1,), (1,)), ((), ())),
                              preferred_element_type=F32) * (MEM_HEAD_DIM ** -0.5)
              for hd in range(N_MEM_HEADS)]

    su = proj(C_SU, GROUP_W)
    sg = proj(C_SG, GROUP_W)
    mg = proj(C_MG, GROUP_W)

    vhs = [_rms(sv[:, lanes(hd)], vg_ref[:, lanes(hd)]).astype(BF16) for hd in range(N_SGU_HEADS)]
    spcs = [jnp.dot(ws_ref[hd],
                    jnp.concatenate([vhs[hd][c * CHUNK:(c + 1) * CHUNK, :] for c in range(nck)],
                                    axis=1), preferred_element_type=F32)
            for hd in range(N_SGU_HEADS)]

    ag_ref[...] = proj(C_AG, GROUP_W).astype(BF16)
    fa_ref[...] = proj(C_FA, GROUP_W).astype(fa_ref.dtype)
    fg_ref[...] = proj(C_FG, GROUP_W).astype(fg_ref.dtype)

    es = [jnp.exp(s - jnp.max(s, axis=-1, keepdims=True)) for s in scores]
    mgate = _silu(mg)
    for hd in range(N_MEM_HEADS):
        mv = kv_ref[:, GROUP_W + hd * MEM_HEAD_DIM:GROUP_W + (hd + 1) * MEM_HEAD_DIM]
        o = jnp.dot(es[hd].astype(BF16), mv, preferred_element_type=F32)
        o = o / jnp.sum(es[hd], axis=-1, keepdims=True)
        omem_ref[:, lanes(hd)] = (o * mgate[:, lanes(hd)]).astype(BF16)

    qn = aq * lax.rsqrt(ssq * (1.0 / HEAD_DIM) + EPS) * qg_ref[...]
    slabs = [_rope(qn[:, lanes(j)], cos, sina, sinb) * Q_SCALE for j in range(GROUP_W // LANES)]
    low = lax.broadcasted_iota(jnp.int32, (Q_TILE, LANES), 1) < HEAD_DIM
    for t in range(tm // Q_TILE):
        parts = []
        for j, slab in enumerate(slabs):
            sj = slab[t * Q_TILE:(t + 1) * Q_TILE, :]
            rolled = pltpu.roll(sj, HEAD_DIM, 1)
            if j < Q_PER_KV // 2:
                parts += [jnp.where(low, sj, 0.0), jnp.where(low, rolled, 0.0)]
            else:
                parts += [jnp.where(low, 0.0, rolled), jnp.where(low, 0.0, sj)]
        qt_ref[t] = jnp.concatenate(parts, axis=0).T.astype(BF16)

    kn = ak * lax.rsqrt(ssk * (1.0 / HEAD_DIM) + EPS) * kg_ref[...]
    k_ref[...] = _rope(kn, cos, sina, sinb).astype(BF16)
    avt = av.T.astype(BF16)
    for g in range(N_KV_HEADS):
        vt_ref[g * VT_ROWS:g * VT_ROWS + HEAD_DIM, :] = avt[g * HEAD_DIM:(g + 1) * HEAD_DIM, :]
        vt_ref[g * VT_ROWS + HEAD_DIM:(g + 1) * VT_ROWS, :] = jnp.ones((BF16_SUBLANES, tm), BF16)

    gate = _silu(sg)
    for hd in range(N_SGU_HEADS):
        for c in range(nck):
            rows = slice(c * CHUNK, (c + 1) * CHUNK)
            sp = spcs[hd][:, c * SGU_W:(c + 1) * SGU_W] + bs_ref[:, lanes(hd)]
            osgu_ref[rows, lanes(hd)] = (su[rows, lanes(hd)] * sp
                                         * gate[rows, lanes(hd)]).astype(BF16)


def _inproj(x, kv, lw, tabs):
    B, S, _ = x.shape
    tm = TOKEN_TILE
    nt = S // tm
    tok = lambda w: pl.BlockSpec((None, tm, w), lambda b, i: (b, i, 0))
    pos = pl.BlockSpec((tm, LANES), lambda b, i: (i, 0))
    in_specs = [
        tok(D_MODEL),
        _const_spec((1, D_MODEL)),
        _const_spec((D_MODEL, IN_W)),
        _const_spec((MXU_COLS, MXU_COLS)),
        _const_spec((1, GROUP_W)),
        _const_spec((1, LANES)),
        pos, pos, pos,
        _const_spec((1, GROUP_W)),
        _const_spec((N_SGU_HEADS, CHUNK, CHUNK)),
        _const_spec((CHUNK, GROUP_W)),
        pl.BlockSpec((None, N_MEM, 2 * GROUP_W), lambda b, i: (b, 0, 0)),
    ]
    out_specs = [
        pl.BlockSpec((None, tm // Q_TILE, LANES, Q_ROWS), lambda b, i: (b, i, 0, 0)), tok(LANES),
        pl.BlockSpec((None, None, N_KV_HEADS * VT_ROWS, tm), lambda b, i: (b, i, 0, 0)),
        tok(GROUP_W), tok(GROUP_W), tok(GROUP_W), tok(GROUP_W), tok(GROUP_W),
    ]
    sds = jax.ShapeDtypeStruct
    four_dt = BF16 if S <= FOURIER_FUSED_MAX_SEQ else F32
    out_shape = [
        sds((B, S // Q_TILE, LANES, Q_ROWS), BF16), sds((B, S, LANES), BF16),
        sds((B, nt, N_KV_HEADS * VT_ROWS, tm), BF16),
        sds((B, S, GROUP_W), BF16), sds((B, S, GROUP_W), four_dt), sds((B, S, GROUP_W), four_dt),
        sds((B, S, GROUP_W), BF16), sds((B, S, GROUP_W), BF16),
    ]
    return pl.pallas_call(
        _inproj_kernel,
        grid=(B, nt),
        in_specs=in_specs, out_specs=out_specs, out_shape=out_shape,
        compiler_params=_cparams("parallel", "parallel"),
        name="inproj",
    )(x, lw["pre_g"], lw["w_in"], tabs["bd"], lw["qg"], lw["kg"],
      tabs["cos"], tabs["sina"], tabs["sinb"], lw["vg"], lw["ws"], lw["bs"], kv)


def _attn_kernel(qt_ref, k_ref, vt_ref, ot_ref, *scratch, bounded):
    ntiles = qt_ref.shape[0]
    nchunks, _, tk = vt_ref.shape
    neg_inf = jnp.full((1, qt_ref.shape[2]), -jnp.inf, F32)

    if bounded:
        p_ref, acc_ref = scratch
    else:
        st_ref, mx_ref, acc_ref = scratch

    def scores(t, c, slot):
        start = c * tk if isinstance(c, int) else pl.multiple_of(c * tk, tk)
        st = jnp.dot(k_ref[pl.ds(start, tk), :], qt_ref[t], preferred_element_type=F32)
        if bounded:
            p_ref[slot] = jnp.exp2(st).astype(BF16)
        else:
            st_ref[slot] = st
            mx_ref[slot] = jnp.max(st, axis=0, keepdims=True)

    def softmax_pv(t, c, slot, m_old, first=False):
        if bounded:
            p, m_new, alpha = p_ref[slot], m_old, None
        else:
            m_new = mx_ref[slot]
            if not first:
                m_new = jnp.maximum(m_old, m_new)
            p = jnp.exp2(st_ref[slot] - m_new).astype(BF16)
            alpha = None if first else jnp.exp2(m_old - m_new)
        for g in range(N_KV_HEADS):
            cols = slice(g * Q_COLS_PER_KV, (g + 1) * Q_COLS_PER_KV)
            pv = jnp.dot(vt_ref[c, g * VT_ROWS:(g + 1) * VT_ROWS, :], p[:, cols],
                         preferred_element_type=F32)
            if first:
                acc_ref[t, g] = pv
            elif bounded:
                acc_ref[t, g] += pv
            else:
                acc_ref[t, g] = alpha[:, cols] * acc_ref[t, g] + pv
        return m_new

    def step_pair(nxt, t, c, slot, m_old, first=False):
        if not bounded or nxt is None:
            if nxt is not None:
                scores(*nxt)
            return softmax_pv(t, c, slot, m_old, first)
        tn, cn, sn = nxt
        start = cn * tk if isinstance(cn, int) else pl.multiple_of(cn * tk, tk)
        kc = k_ref[pl.ds(start, tk), :]
        for i in range(Q_ROWS // MXU_COLS):
            cols = slice(i * MXU_COLS, (i + 1) * MXU_COLS)
            st = jnp.dot(kc, qt_ref[tn, :, cols], preferred_element_type=F32)
            p_ref[sn, :, cols] = jnp.exp2(st).astype(BF16)
            g, off = divmod(i * MXU_COLS, Q_COLS_PER_KV)
            pv = jnp.dot(vt_ref[c, g * VT_ROWS:(g + 1) * VT_ROWS, :], p_ref[slot, :, cols],
                         preferred_element_type=F32)
            if first:
                acc_ref[t, g, :, off:off + MXU_COLS] = pv
            else:
                acc_ref[t, g, :, off:off + MXU_COLS] += pv
        return m_old

    def finish(t):
        for g in range(N_KV_HEADS):
            acc = acc_ref[t, g]
            ot_ref[t, :, g * Q_COLS_PER_KV:(g + 1) * Q_COLS_PER_KV] = (
                acc[:HEAD_DIM] / acc[HEAD_DIM:HEAD_DIM + 1]).astype(BF16)

    if nchunks <= ATTN_UNROLL:
        assert nchunks % 2 == 0

        def tile_steps(t, nxt):
            m = None
            for c in range(nchunks):
                if c + 1 < nchunks:
                    ahead = (t, c + 1, (c + 1) % 2)
                else:
                    ahead = None if nxt is None else (nxt, 0, 0)
                m = step_pair(ahead, t, c, c % 2, m, first=(c == 0))
            finish(t)

        scores(0, 0, 0)
        if ntiles > 1:
            def tile(t, carry):
                tile_steps(t, t + 1)
                return carry
            lax.fori_loop(0, ntiles - 1, tile, 0, unroll=True)
        tile_steps(ntiles - 1, None)
    else:
        assert ntiles == 1
        steady = nchunks - 1
        groups = (steady - 1) // ATTN_UNROLL
        peeled = steady - groups * ATTN_UNROLL

        def step(c, slot, m, first=False):
            return step_pair((0, c + 1, 1 - slot), 0, c, slot, m, first)

        scores(0, 0, 0)
        m = neg_inf
        for c in range(peeled):
            m = step(c, c % 2, m, first=(c == 0))

        def group(g, m):
            for u in range(ATTN_UNROLL):
                m = step(peeled + g * ATTN_UNROLL + u, (peeled + u) % 2, m)
            return m
        m = lax.fori_loop(0, groups, group, m)
        softmax_pv(0, nchunks - 1, (nchunks - 1) % 2, m)
        finish(0)


def _attention(qt, k, vt, bounded):
    B, nq = qt.shape[0], qt.shape[1]
    S = k.shape[1]
    tk = vt.shape[3]
    ntiles = min(nq, ATTN_TILES_PER_STEP) if S // tk <= ATTN_UNROLL else 1
    qspec = pl.BlockSpec((None, ntiles, LANES, Q_ROWS), lambda b, i: (b, i, 0, 0))
    if bounded:
        buffers = [pltpu.VMEM((2, tk, Q_ROWS), BF16)]
    else:
        buffers = [pltpu.VMEM((2, tk, Q_ROWS), F32), pltpu.VMEM((2, 1, Q_ROWS), F32)]
    return pl.pallas_call(
        functools.partial(_attn_kernel, bounded=bounded),
        grid=(B, nq // ntiles),
        in_specs=[qspec,
                  pl.BlockSpec((None, S, LANES), lambda b, i: (b, 0, 0)),
                  pl.BlockSpec((None,) + vt.shape[1:], lambda b, i: (b, 0, 0, 0))],
        out_specs=pl.BlockSpec((None, ntiles, HEAD_DIM, Q_ROWS), lambda b, i: (b, i, 0, 0)),
        out_shape=jax.ShapeDtypeStruct((B, nq, HEAD_DIM, Q_ROWS), BF16),
        scratch_shapes=buffers + [pltpu.VMEM((ntiles, N_KV_HEADS, VT_ROWS, Q_COLS_PER_KV), F32)],
        compiler_params=_cparams("parallel", "parallel"),
        name="attn_bounded" if bounded else "attn",
    )(qt, k, vt)


def _dft_tables(S):
    n2 = DFT_N2
    n1 = S // n2
    c = np.arange(FOURIER_W)
    ang = 2.0 * np.pi * np.outer(c, c) / FOURIER_W
    chan = np.concatenate([np.cos(ang), -np.sin(ang)], axis=1) / np.sqrt(FOURIER_W)
    a1 = 2.0 * np.pi * np.outer(np.arange(n1), np.arange(n1)) / n1
    eye = np.eye(SUBLANES)
    mr = np.kron(np.cos(a1), eye) / np.sqrt(n1)
    mi = np.kron(-np.sin(a1), eye) / np.sqrt(n1)
    m1 = np.block([[mr, -mi], [mi, mr]])
    k1 = np.arange(n1)[None, :, None]
    s2 = (np.arange(n2 // SUBLANES)[:, None, None] * SUBLANES + np.arange(SUBLANES)[None, None, :])
    at = 2.0 * np.pi * (k1 * s2) / S
    twr = np.cos(at).reshape(n2 // SUBLANES, n1 * SUBLANES, 1)
    twi = (-np.sin(at)).reshape(n2 // SUBLANES, n1 * SUBLANES, 1)
    a2 = 2.0 * np.pi * np.outer(np.arange(n2), np.arange(n2)) / n2
    c2 = (np.cos(a2) / np.sqrt(n2))[:, None, None, :] * eye[None, :, :, None]
    s2m = (np.sin(a2) / np.sqrt(n2))[:, None, None, :] * eye[None, :, :, None]
    m2 = np.concatenate([c2.reshape(n2 * SUBLANES, n2 * SUBLANES),
                         s2m.reshape(n2 * SUBLANES, n2 * SUBLANES)], axis=1)
    c2s2 = np.concatenate([np.cos(a2), np.sin(a2)], axis=1) / np.sqrt(n2)
    tabs = dict(twr=jnp.asarray(twr, F32), twi=jnp.asarray(twi, F32))
    used = dict(chan=chan, m1=m1, c2=c2s2) if S <= FOURIER_FUSED_MAX_SEQ else dict(chan=chan, m1=m1, m2=m2)
    for name, table in used.items():
        tabs[name] = _round_table(jnp.asarray(table, F32))
    return tabs


def _round_kernel(x_ref, o_ref):
    o_ref[...] = x_ref[...].astype(o_ref.dtype)


def _round_table(x):
    rows, cols = x.shape
    blk = min(rows, ROUND_BLOCK_ROWS)
    spec = pl.BlockSpec((blk, cols), lambda i: (i, 0))
    return pl.pallas_call(
        _round_kernel, grid=(rows // blk,), in_specs=[spec], out_specs=spec,
        out_shape=jax.ShapeDtypeStruct(x.shape, BF16),
        compiler_params=_cparams("parallel"), name="round_table",
    )(x)


def _four1_kernel(fa_ref, chan_ref, m1_ref, twr_ref, twi_ref, br_ref, bi_ref):
    n1, sb, _ = fa_ref.shape
    rows = n1 * SUBLANES
    for i in range(sb // SUBLANES):
        rs = slice(i * SUBLANES, (i + 1) * SUBLANES)
        x = fa_ref[:, rs, :].reshape(rows, GROUP_W).astype(BF16)
        zr, zi = [], []
        for g in range(N_FOURIER_GROUPS):
            z = jnp.dot(x[:, g * FOURIER_W:(g + 1) * FOURIER_W], chan_ref[...],
                        preferred_element_type=F32)
            zr.append(z[:, :FOURIER_W])
            zi.append(z[:, FOURIER_W:])
        zcat = jnp.concatenate([jnp.concatenate(zr, axis=1), jnp.concatenate(zi, axis=1)],
                               axis=0).astype(BF16)
        a = jnp.dot(m1_ref[...], zcat, preferred_element_type=F32)
        ar, ai = a[:rows], a[rows:]
        twr, twi = twr_ref[i], twi_ref[i]
        br_ref[:, rs, :] = (ar * twr - ai * twi).reshape(n1, SUBLANES, GROUP_W)
        bi_ref[:, rs, :] = (ar * twi + ai * twr).reshape(n1, SUBLANES, GROUP_W)


def _four2_kernel(br_ref, bi_ref, m2_ref, wf_ref, fg_ref, o_ref):
    _, n2, _ = br_ref.shape
    rows = SUBLANES * n2
    bcat = jnp.concatenate([br_ref[...].reshape(rows, GROUP_W), bi_ref[...].reshape(rows, GROUP_W)],
                           axis=0).astype(BF16)
    f = jnp.dot(m2_ref[...], bcat, preferred_element_type=F32).astype(BF16)
    gate = _silu(fg_ref[...].reshape(rows, GROUP_W))
    ys = [jnp.dot(f[:, g * FOURIER_W:(g + 1) * FOURIER_W], wf_ref[g], preferred_element_type=F32)
          for g in range(N_FOURIER_GROUPS)]
    y = jnp.concatenate(ys, axis=1) * gate
    o_ref[...] = y.reshape(n2, SUBLANES, GROUP_W)


def _four_fused_kernel(fa_ref, fg_ref, chan_ref, m1_ref, twr_ref, twi_ref, c2_ref, wf_ref, o_ref,
                       zr_ref, zi_ref, br_ref, bi_ref, f_ref):
    S = fa_ref.shape[0]
    n2 = DFT_N2
    n1 = S // n2
    lanes = lambda g: slice(g * FOURIER_W, (g + 1) * FOURIER_W)

    x = fa_ref[...].astype(BF16)
    for g in range(N_FOURIER_GROUPS):
        z = jnp.dot(x[:, lanes(g)], chan_ref[...], preferred_element_type=F32)
        zr_ref[:, lanes(g)] = z[:, :FOURIER_W]
        zi_ref[:, lanes(g)] = z[:, FOURIER_W:]

    def stage1(j, carry):
        base = pl.multiple_of(j * SUBLANES, SUBLANES)
        tiles = ([zr_ref[pl.ds(s1 * n2 + base, SUBLANES), :] for s1 in range(n1)]
                 + [zi_ref[pl.ds(s1 * n2 + base, SUBLANES), :] for s1 in range(n1)])
        a = jnp.dot(m1_ref[...], jnp.concatenate(tiles, axis=0).astype(BF16),
                    preferred_element_type=F32)
        ar, ai = a[:n1 * SUBLANES], a[n1 * SUBLANES:]
        twr, twi = twr_ref[j], twi_ref[j]
        br = ar * twr - ai * twi
        bi = ar * twi + ai * twr
        for k1 in range(n1):
            rows = slice(k1 * SUBLANES, (k1 + 1) * SUBLANES)
            br_ref[pl.ds(k1 * n2 + base, SUBLANES), :] = br[rows]
            bi_ref[pl.ds(k1 * n2 + base, SUBLANES), :] = bi[rows]
        return carry

    lax.fori_loop(0, n2 // SUBLANES, stage1, 0, unroll=FOURIER_UNROLL)

    def stage2(k1, carry):
        start = pl.multiple_of(k1 * n2, n2)
        bcat = jnp.concatenate([br_ref[pl.ds(start, n2), :], bi_ref[pl.ds(start, n2), :]],
                               axis=0).astype(BF16)
        f = jnp.dot(c2_ref[...], bcat, preferred_element_type=F32)
        for g in range(N_FOURIER_GROUPS):
            f_ref[g, pl.ds(k1, n2, stride=n1), :] = f[:, lanes(g)]
        return carry

    lax.fori_loop(0, n1, stage2, 0, unroll=FOURIER_UNROLL)

    gate = _silu(fg_ref[...].astype(F32))
    for g in range(N_FOURIER_GROUPS):
        y = jnp.dot(f_ref[g].astype(BF16), wf_ref[g], preferred_element_type=F32)
        o_ref[:, lanes(g)] = (y * gate[:, lanes(g)]).astype(o_ref.dtype)


def _fourier_fused(fa, fg, wf, ft):
    B, S, _ = fa.shape
    n1 = S // DFT_N2
    tok = pl.BlockSpec((None, S, GROUP_W), lambda b: (b, 0, 0))
    return pl.pallas_call(
        _four_fused_kernel,
        grid=(B,),
        in_specs=[tok, tok, _const_spec((FOURIER_W, 2 * FOURIER_W)),
                  _const_spec((2 * n1 * SUBLANES, 2 * n1 * SUBLANES)),
                  _const_spec(ft["twr"].shape), _const_spec(ft["twi"].shape),
                  _const_spec((DFT_N2, 2 * DFT_N2)),
                  _const_spec((N_FOURIER_GROUPS, FOURIER_W, FOURIER_W))],
        out_specs=tok,
        out_shape=jax.ShapeDtypeStruct((B, S, GROUP_W), BF16),
        scratch_shapes=[pltpu.VMEM((S, GROUP_W), F32)] * 4
                       + [pltpu.VMEM((N_FOURIER_GROUPS, S, FOURIER_W), F32)],
        compiler_params=_cparams("parallel"),
        name="four",
    )(fa, fg, ft["chan"], ft["m1"], ft["twr"], ft["twi"], ft["c2"], wf)


def _fourier(fa, fg, wf, ft):
    B, S, _ = fa.shape
    if S <= FOURIER_FUSED_MAX_SEQ:
        return _fourier_fused(fa, fg, wf, ft)
    n2 = DFT_N2
    n1 = S // n2
    sb = 32 if n1 <= 16 else SUBLANES
    nsub = sb // SUBLANES
    fa4 = fa.reshape(B, n1, n2, GROUP_W)
    blk1 = pl.BlockSpec((None, n1, sb, GROUP_W), lambda b, j: (b, 0, j, 0))
    tw = pl.BlockSpec((nsub, n1 * SUBLANES, 1), lambda b, j: (j, 0, 0))
    br, bi = pl.pallas_call(
        _four1_kernel,
        grid=(B, n2 // sb),
        in_specs=[blk1, _const_spec((FOURIER_W, 2 * FOURIER_W)),
                  _const_spec((2 * n1 * SUBLANES, 2 * n1 * SUBLANES)), tw, tw],
        out_specs=[blk1, blk1],
        out_shape=[jax.ShapeDtypeStruct((B, n1, n2, GROUP_W), F32)] * 2,
        compiler_params=_cparams("parallel", "parallel"),
        name="four1",
    )(fa4, ft["chan"], ft["m1"], ft["twr"], ft["twi"])
    blk_in = pl.BlockSpec((None, SUBLANES, n2, GROUP_W), lambda b, j: (b, j, 0, 0))
    blk_out = pl.BlockSpec((None, n2, SUBLANES, GROUP_W), lambda b, j: (b, 0, j, 0))
    o = pl.pallas_call(
        _four2_kernel,
        grid=(B, n1 // SUBLANES),
        in_specs=[blk_in, blk_in, _const_spec((SUBLANES * n2, 2 * SUBLANES * n2)),
                  _const_spec((N_FOURIER_GROUPS, FOURIER_W, FOURIER_W)), blk_out],
        out_specs=blk_out,
        out_shape=jax.ShapeDtypeStruct((B, n2, n1, GROUP_W), F32),
        compiler_params=_cparams("parallel", "parallel"),
        name="four2",
    )(br, bi, ft["m2"], wf, fg.reshape(B, n2, n1, GROUP_W))
    return o.reshape(B, S, GROUP_W)


def _outproj_kernel(ot_ref, ag_ref, of_ref, os_ref, om_ref, x_ref, w_ref, g_ref, y_ref):
    for s in range(ot_ref.shape[0] // OUT_SUB_TILES):
        rows = slice(s * OUT_SUB_TILES * Q_TILE, (s + 1) * OUT_SUB_TILES * Q_TILE)
        tiles = []
        for t in range(s * OUT_SUB_TILES, (s + 1) * OUT_SUB_TILES):
            ot = ot_ref[t].astype(F32)
            slabs = []
            for j in range(GROUP_W // LANES):
                pair = jnp.concatenate([ot[:, (2 * j) * Q_TILE:(2 * j + 1) * Q_TILE],
                                        ot[:, (2 * j + 1) * Q_TILE:(2 * j + 2) * Q_TILE]], axis=0)
                slabs.append(pair.T)
            tiles.append(jnp.concatenate(slabs, axis=1))
        oa = (jnp.concatenate(tiles, axis=0) * _silu(ag_ref[rows, :].astype(F32))).astype(BF16)
        rest = jnp.concatenate([of_ref[rows, :].astype(BF16), os_ref[rows, :], om_ref[rows, :]],
                               axis=1)
        y = jnp.dot(rest, w_ref[GROUP_W:, :], preferred_element_type=F32)
        y = y + jnp.dot(oa, w_ref[:GROUP_W, :], preferred_element_type=F32)
        y_ref[rows, :] = x_ref[rows, :] + _rms(y, g_ref[...])


def _outproj(ot, ag, of, osg, om, x, w, g):
    B, S, _ = x.shape
    tm = OUT_TILE
    tok = lambda wd: pl.BlockSpec((None, tm, wd), lambda b, i: (b, i, 0))
    return pl.pallas_call(
        _outproj_kernel,
        grid=(B, S // tm),
        in_specs=[pl.BlockSpec((None, tm // Q_TILE, HEAD_DIM, Q_ROWS), lambda b, i: (b, i, 0, 0)),
                  tok(GROUP_W), tok(GROUP_W), tok(GROUP_W), tok(GROUP_W), tok(D_MODEL),
                  _const_spec((4 * GROUP_W, D_MODEL)), _const_spec((1, D_MODEL))],
        out_specs=tok(D_MODEL),
        out_shape=jax.ShapeDtypeStruct((B, S, D_MODEL), F32),
        compiler_params=_cparams("parallel", "parallel"),
        name="outproj",
    )(ot, ag, of, osg, om, x, w, g)


def _rope_tables(S):
    rows = S // GRID_W
    row = jnp.broadcast_to(jnp.arange(rows, dtype=F32)[:, None], (rows, GRID_W)).reshape(S)
    col = jnp.broadcast_to(jnp.arange(GRID_W, dtype=F32)[None, :], (rows, GRID_W)).reshape(S)
    inv = ROPE_THETA ** (-jnp.arange(ROPE_PAIRS, dtype=F32) / ROPE_PAIRS)
    ang = jnp.stack([row[:, None] * inv, col[:, None] * inv], axis=1)
    cos, sin = jnp.cos(ang), jnp.sin(ang)
    zero = jnp.zeros((S, ROPE_PAIRS), F32)
    two = lambda parts: jnp.tile(jnp.concatenate(parts, axis=-1), (1, LANES // HEAD_DIM))
    return dict(cos=two([cos[:, 0], cos[:, 0], cos[:, 1], cos[:, 1]]),
                sina=two([zero, sin[:, 0], zero, sin[:, 1]]),
                sinb=two([-sin[:, 0], zero, -sin[:, 1], zero]))


def _layer_weights(l, pre_norm_g, w_in, q_norm_g, k_norm_g, w_fourier, sgu_norm_g, w_spatial,
                   b_spatial, mem_norm_g, w_mem_kv, w_out, post_norm_g):
    return dict(
        pre_g=pre_norm_g[l][None, :],
        w_in=w_in[l].astype(BF16),
        qg=jnp.tile(q_norm_g[l], N_HEADS)[None, :],
        kg=jnp.tile(k_norm_g[l], N_KV_HEADS)[None, :],
        wf=w_fourier[l].astype(BF16),
        vg=sgu_norm_g[l].reshape(1, GROUP_W),
        ws=w_spatial[l].astype(BF16),
        bs=jnp.repeat(b_spatial[l].T, SGU_W, axis=1),
        mem_g=mem_norm_g[l][None, :],
        w_mem_kv=w_mem_kv[l].astype(BF16),
        w_out=w_out[l].astype(BF16),
        post_g=post_norm_g[l][None, :],
        score_bound=(HEAD_DIM * Q_SCALE * jnp.max(jnp.abs(q_norm_g[l]))
                     * jnp.max(jnp.abs(k_norm_g[l]))),
    )


def _trunk(x, mem, layers):
    S = x.shape[1]
    tabs = _rope_tables(S)
    ids = np.arange(MXU_COLS) // HEAD_DIM
    tabs["bd"] = jnp.asarray(ids[:, None] == ids[None, :], BF16)
    ft = _dft_tables(S)
    for lw in layers:
        kv = _memkv(mem, lw["mem_g"], lw["w_mem_kv"])
        qt, k, vt, ag, fa, fg, osgu, omem = _inproj(x, kv, lw, tabs)
        ot = lax.cond(lw["score_bound"] < ATTN_UNSHIFTED_MAX_LOG2,
                      functools.partial(_attention, bounded=True),
                      functools.partial(_attention, bounded=False), qt, k, vt)
        ofour = _fourier(fa, fg, lw["wf"], ft)
        x = _outproj(ot, ag, ofour, osgu, omem, x, lw["w_out"], lw["post_g"])
    return x


def kernel(x_prompt, x_sample, mem_prompt, mem_sample, pre_norm_g, w_in, q_norm_g, k_norm_g,
           w_fourier, sgu_norm_g, w_spatial, b_spatial, mem_norm_g, w_mem_kv, w_out, post_norm_g):
    layers = [_layer_weights(l, pre_norm_g, w_in, q_norm_g, k_norm_g, w_fourier, sgu_norm_g,
                             w_spatial, b_spatial, mem_norm_g, w_mem_kv, w_out, post_norm_g)
              for l in range(DEPTH)]
    return (_trunk(x_prompt, mem_prompt, layers), _trunk(x_sample, mem_sample, layers))
```

```python
import functools

import numpy as np
import jax
import jax.numpy as jnp
from jax import lax
from jax.experimental import pallas as pl
from jax.experimental.pallas import tpu as pltpu

F32 = jnp.float32
BF16 = jnp.bfloat16

D_MODEL = 1024
DEPTH = 2
GRID_W = 64
N_MEM = 256
GROUP_W = 512
HEAD_DIM = 64
N_HEADS = 8
N_KV_HEADS = 2
Q_PER_KV = N_HEADS // N_KV_HEADS
ROPE_PAIRS = HEAD_DIM // 4
ROPE_THETA = 10000.0
N_FOURIER_GROUPS = 4
FOURIER_W = 128
N_SGU_HEADS = 4
SGU_W = 128
CHUNK = 128
N_MEM_HEADS = 4
MEM_HEAD_DIM = 128
EPS = 1e-6
IN_W = 4864
C_AQ, C_AK, C_AV, C_AG, C_FA, C_FG, C_SU, C_SV, C_SG, C_MQ, C_MG = (
    0, 512, 640, 768, 1280, 1792, 2304, 2816, 3328, 3840, 4352)

LANES = 128
SUBLANES = 8
MXU_COLS = 256
VMEM_LIMIT_BYTES = 56 * 1024 * 1024

TOKEN_TILE = 512
Q_TILE = 128
Q_ROWS = N_HEADS * Q_TILE
DFT_N2 = 128
FOURIER_FUSED_BYTES_PER_TOKEN = GROUP_W * (3 * 2 * 2 + 4 * 4 + 4)
FOURIER_FUSED_MAX_SEQ = VMEM_LIMIT_BYTES // FOURIER_FUSED_BYTES_PER_TOKEN
ROUND_BLOCK_ROWS = 256
FOURIER_UNROLL = 8
ATTN_UNROLL = 14
OUT_TILE = 1024
OUT_SUB_TILES = 2
ATTN_UNSHIFTED_MAX_LOG2 = 64.0
ATTN_TILES_PER_STEP = 8
BF16_SUBLANES = 16
VT_ROWS = HEAD_DIM + BF16_SUBLANES
Q_COLS_PER_KV = Q_PER_KV * Q_TILE
Q_SCALE = float(HEAD_DIM ** -0.5 * np.log2(np.e))


def _cparams(*sem):
    return pltpu.CompilerParams(dimension_semantics=sem, vmem_limit_bytes=VMEM_LIMIT_BYTES)


def _const_spec(shape):
    nd = len(shape)
    return pl.BlockSpec(shape, lambda *_: (0,) * nd)


def _silu(g):
    return g / (1.0 + jnp.exp(-g))


def _rms(x, g):
    ms = jnp.mean(x * x, axis=-1, keepdims=True)
    return x * lax.rsqrt(ms + EPS) * g


def _memkv_kernel(mem_ref, g_ref, w_ref, kv_ref):
    h = _rms(mem_ref[...], g_ref[...]).astype(BF16)
    kv_ref[...] = jnp.dot(h, w_ref[...], preferred_element_type=F32).astype(BF16)


def _memkv(mem, g, w):
    B = mem.shape[0]
    return pl.pallas_call(
        _memkv_kernel,
        grid=(B,),
        in_specs=[pl.BlockSpec((None, N_MEM, D_MODEL), lambda b: (b, 0, 0)),
                  _const_spec((1, D_MODEL)),
                  _const_spec((D_MODEL, 2 * GROUP_W))],
        out_specs=pl.BlockSpec((None, N_MEM, 2 * GROUP_W), lambda b: (b, 0, 0)),
        out_shape=jax.ShapeDtypeStruct((B, N_MEM, 2 * GROUP_W), BF16),
        compiler_params=_cparams("parallel"),
        name="memkv",
    )(mem, g, w)


def _head_ssq(x, bd):
    sq = (x * x).astype(BF16)
    w = bd.shape[0]
    if x.shape[1] <= w:
        return jnp.dot(sq, bd[:x.shape[1], :x.shape[1]], preferred_element_type=F32)
    return jnp.concatenate([jnp.dot(sq[:, i:i + w], bd, preferred_element_type=F32)
                            for i in range(0, x.shape[1], w)], axis=1)


def _rope(x, cos, sina, sinb):
    return x * cos + pltpu.roll(x, 16, 1) * sina + pltpu.roll(x, LANES - 16, 1) * sinb


def _inproj_kernel(x_ref, pre_g_ref, w_ref, bd_ref, qg_ref, kg_ref, cos_ref, sina_ref, sinb_ref,
                   vg_ref, ws_ref, bs_ref, kv_ref,
                   qt_ref, k_ref, vt_ref, ag_ref, fa_ref, fg_ref, osgu_ref, omem_ref):
    tm = x_ref.shape[0]
    h = _rms(x_ref[...], pre_g_ref[...]).astype(BF16)

    def proj(lo, width):
        return jnp.dot(h, w_ref[:, lo:lo + width], preferred_element_type=F32)

    cos, sina, sinb = cos_ref[...], sina_ref[...], sinb_ref[...]
    nck = tm // CHUNK
    lanes = lambda i: slice(i * LANES, (i + 1) * LANES)

    aq = proj(C_AQ, GROUP_W)
    ak = proj(C_AK, LANES)
    av = proj(C_AV, LANES)
    mq = proj(C_MQ, GROUP_W)
    sv = proj(C_SV, GROUP_W)
    ssq = _head_ssq(aq, bd_ref[...])
    ssk = _head_ssq(ak, bd_ref[...])

    mq16 = mq.astype(BF16)
    scores = [lax.dot_general(mq16[:, lanes(hd)], kv_ref[:, lanes(hd)], (((1,), (1,)), ((), ())),
                              preferred_element_type=F32) * (MEM_HEAD_DIM ** -0.5)
              for hd in range(N_MEM_HEADS)]

    su = proj(C_SU, GROUP_W)
    sg = proj(C_SG, GROUP_W)
    mg = proj(C_MG, GROUP_W)

    vhs = [_rms(sv[:, lanes(hd)], vg_ref[:, lanes(hd)]).astype(BF16) for hd in range(N_SGU_HEADS)]
    spcs = [jnp.dot(ws_ref[hd],
                    jnp.concatenate([vhs[hd][c * CHUNK:(c + 1) * CHUNK, :] for c in range(nck)],
                                    axis=1), preferred_element_type=F32)
            for hd in range(N_SGU_HEADS)]

    ag_ref[...] = proj(C_AG, GROUP_W).astype(BF16)
    fa_ref[...] = proj(C_FA, GROUP_W).astype(fa_ref.dtype)
    fg_ref[...] = proj(C_FG, GROUP_W).astype(fg_ref.dtype)

    es = [jnp.exp(s - jnp.max(s, axis=-1, keepdims=True)) for s in scores]
    mgate = _silu(mg)
    for hd in range(N_MEM_HEADS):
        mv = kv_ref[:, GROUP_W + hd * MEM_HEAD_DIM:GROUP_W + (hd + 1) * MEM_HEAD_DIM]
        o = jnp.dot(es[hd].astype(BF16), mv, preferred_element_type=F32)
        o = o / jnp.sum(es[hd], axis=-1, keepdims=True)
        omem_ref[:, lanes(hd)] = (o * mgate[:, lanes(hd)]).astype(BF16)

    qn = aq * lax.rsqrt(ssq * (1.0 / HEAD_DIM) + EPS) * qg_ref[...]
    slabs = [_rope(qn[:, lanes(j)], cos, sina, sinb) * Q_SCALE for j in range(GROUP_W // LANES)]
    low = lax.broadcasted_iota(jnp.int32, (Q_TILE, LANES), 1) < HEAD_DIM
    for t in range(tm // Q_TILE):
        parts = []
        for j, slab in enumerate(slabs):
            sj = slab[t * Q_TILE:(t + 1) * Q_TILE, :]
            rolled = pltpu.roll(sj, HEAD_DIM, 1)
            if j < Q_PER_KV // 2:
                parts += [jnp.where(low, sj, 0.0), jnp.where(low, rolled, 0.0)]
            else:
                parts += [jnp.where(low, 0.0, rolled), jnp.where(low, 0.0, sj)]
        qt_ref[t] = jnp.concatenate(parts, axis=0).T.astype(BF16)

    kn = ak * lax.rsqrt(ssk * (1.0 / HEAD_DIM) + EPS) * kg_ref[...]
    k_ref[...] = _rope(kn, cos, sina, sinb).astype(BF16)
    avt = av.T.astype(BF16)
    for g in range(N_KV_HEADS):
        vt_ref[g * VT_ROWS:g * VT_ROWS + HEAD_DIM, :] = avt[g * HEAD_DIM:(g + 1) * HEAD_DIM, :]
        vt_ref[g * VT_ROWS + HEAD_DIM:(g + 1) * VT_ROWS, :] = jnp.ones((BF16_SUBLANES, tm), BF16)

    gate = _silu(sg)
    for hd in range(N_SGU_HEADS):
        for c in range(nck):
            rows = slice(c * CHUNK, (c + 1) * CHUNK)
            sp = spcs[hd][:, c * SGU_W:(c + 1) * SGU_W] + bs_ref[:, lanes(hd)]
            osgu_ref[rows, lanes(hd)] = (su[rows, lanes(hd)] * sp
                                         * gate[rows, lanes(hd)]).astype(BF16)


def _inproj(x, kv, lw, tabs):
    B, S, _ = x.shape
    tm = TOKEN_TILE
    nt = S // tm
    tok = lambda w: pl.BlockSpec((None, tm, w), lambda b, i: (b, i, 0))
    pos = pl.BlockSpec((tm, LANES), lambda b, i: (i, 0))
    in_specs = [
        tok(D_MODEL),
        _const_spec((1, D_MODEL)),
        _const_spec((D_MODEL, IN_W)),
        _const_spec((MXU_COLS, MXU_COLS)),
        _const_spec((1, GROUP_W)),
        _const_spec((1, LANES)),
        pos, pos, pos,
        _const_spec((1, GROUP_W)),
        _const_spec((N_SGU_HEADS, CHUNK, CHUNK)),
        _const_spec((CHUNK, GROUP_W)),
        pl.BlockSpec((None, N_MEM, 2 * GROUP_W), lambda b, i: (b, 0, 0)),
    ]
    out_specs = [
        pl.BlockSpec((None, tm // Q_TILE, LANES, Q_ROWS), lambda b, i: (b, i, 0, 0)), tok(LANES),
        pl.BlockSpec((None, None, N_KV_HEADS * VT_ROWS, tm), lambda b, i: (b, i, 0, 0)),
        tok(GROUP_W), tok(GROUP_W), tok(GROUP_W), tok(GROUP_W), tok(GROUP_W),
    ]
    sds = jax.ShapeDtypeStruct
    four_dt = BF16 if S <= FOURIER_FUSED_MAX_SEQ else F32
    out_shape = [
        sds((B, S // Q_TILE, LANES, Q_ROWS), BF16), sds((B, S, LANES), BF16),
        sds((B, nt, N_KV_HEADS * VT_ROWS, tm), BF16),
        sds((B, S, GROUP_W), BF16), sds((B, S, GROUP_W), four_dt), sds((B, S, GROUP_W), four_dt),
        sds((B, S, GROUP_W), BF16), sds((B, S, GROUP_W), BF16),
    ]
    return pl.pallas_call(
        _inproj_kernel,
        grid=(B, nt),
        in_specs=in_specs, out_specs=out_specs, out_shape=out_shape,
        compiler_params=_cparams("parallel", "parallel"),
        name="inproj",
    )(x, lw["pre_g"], lw["w_in"], tabs["bd"], lw["qg"], lw["kg"],
      tabs["cos"], tabs["sina"], tabs["sinb"], lw["vg"], lw["ws"], lw["bs"], kv)


def _attn_kernel(qt_ref, k_ref, vt_ref, ot_ref, *scratch, bounded):
    ntiles = qt_ref.shape[0]
    nchunks, _, tk = vt_ref.shape
    neg_inf = jnp.full((1, qt_ref.shape[2]), -jnp.inf, F32)

    if bounded:
        p_ref, acc_ref = scratch
    else:
        st_ref, mx_ref, acc_ref = scratch

    def scores(t, c, slot):
        start = c * tk if isinstance(c, int) else pl.multiple_of(c * tk, tk)
        st = jnp.dot(k_ref[pl.ds(start, tk), :], qt_ref[t], preferred_element_type=F32)
        if bounded:
            p_ref[slot] = jnp.exp2(st).astype(BF16)
        else:
            st_ref[slot] = st
            mx_ref[slot] = jnp.max(st, axis=0, keepdims=True)

    def softmax_pv(t, c, slot, m_old, first=False):
        if bounded:
            p, m_new, alpha = p_ref[slot], m_old, None
        else:
            m_new = mx_ref[slot]
            if not first:
                m_new = jnp.maximum(m_old, m_new)
            p = jnp.exp2(st_ref[slot] - m_new).astype(BF16)
            alpha = None if first else jnp.exp2(m_old - m_new)
        for g in range(N_KV_HEADS):
            cols = slice(g * Q_COLS_PER_KV, (g + 1) * Q_COLS_PER_KV)
            pv = jnp.dot(vt_ref[c, g * VT_ROWS:(g + 1) * VT_ROWS, :], p[:, cols],
                         preferred_element_type=F32)
            if first:
                acc_ref[t, g] = pv
            elif bounded:
                acc_ref[t, g] += pv
            else:
                acc_ref[t, g] = alpha[:, cols] * acc_ref[t, g] + pv
        return m_new

    def step_pair(nxt, t, c, slot, m_old, first=False):
        if not bounded or nxt is None:
            if nxt is not None:
                scores(*nxt)
            return softmax_pv(t, c, slot, m_old, first)
        tn, cn, sn = nxt
        start = cn * tk if isinstance(cn, int) else pl.multiple_of(cn * tk, tk)
        kc = k_ref[pl.ds(start, tk), :]
        for i in range(Q_ROWS // MXU_COLS):
            cols = slice(i * MXU_COLS, (i + 1) * MXU_COLS)
            st = jnp.dot(kc, qt_ref[tn, :, cols], preferred_element_type=F32)
            p_ref[sn, :, cols] = jnp.exp2(st).astype(BF16)
            g, off = divmod(i * MXU_COLS, Q_COLS_PER_KV)
            pv = jnp.dot(vt_ref[c, g * VT_ROWS:(g + 1) * VT_ROWS, :], p_ref[slot, :, cols],
                         preferred_element_type=F32)
            if first:
                acc_ref[t, g, :, off:off + MXU_COLS] = pv
            else:
                acc_ref[t, g, :, off:off + MXU_COLS] += pv
        return m_old

    def finish(t):
        for g in range(N_KV_HEADS):
            acc = acc_ref[t, g]
            ot_ref[t, :, g * Q_COLS_PER_KV:(g + 1) * Q_COLS_PER_KV] = (
                acc[:HEAD_DIM] / acc[HEAD_DIM:HEAD_DIM + 1]).astype(BF16)

    if nchunks <= ATTN_UNROLL:
        assert nchunks % 2 == 0

        def tile_steps(t, nxt):
            m = None
            for c in range(nchunks):
                if c + 1 < nchunks:
                    ahead = (t, c + 1, (c + 1) % 2)
                else:
                    ahead = None if nxt is None else (nxt, 0, 0)
                m = step_pair(ahead, t, c, c % 2, m, first=(c == 0))
            finish(t)

        scores(0, 0, 0)
        if ntiles > 1:
            def tile(t, carry):
                tile_steps(t, t + 1)
                return carry
            lax.fori_loop(0, ntiles - 1, tile, 0, unroll=True)
        tile_steps(ntiles - 1, None)
    else:
        assert ntiles == 1
        steady = nchunks - 1
        groups = (steady - 1) // ATTN_UNROLL
        peeled = steady - groups * ATTN_UNROLL

        def step(c, slot, m, first=False):
            return step_pair((0, c + 1, 1 - slot), 0, c, slot, m, first)

        scores(0, 0, 0)
        m = neg_inf
        for c in range(peeled):
            m = step(c, c % 2, m, first=(c == 0))

        def group(g, m):
            for u in range(ATTN_UNROLL):
                m = step(peeled + g * ATTN_UNROLL + u, (peeled + u) % 2, m)
            return m
        m = lax.fori_loop(0, groups, group, m)
        softmax_pv(0, nchunks - 1, (nchunks - 1) % 2, m)
        finish(0)


def _attention(qt, k, vt, bounded):
    B, nq = qt.shape[0], qt.shape[1]
    S = k.shape[1]
    tk = vt.shape[3]
    ntiles = min(nq, ATTN_TILES_PER_STEP) if S // tk <= ATTN_UNROLL else 1
    qspec = pl.BlockSpec((None, ntiles, LANES, Q_ROWS), lambda b, i: (b, i, 0, 0))
    if bounded:
        buffers = [pltpu.VMEM((2, tk, Q_ROWS), BF16)]
    else:
        buffers = [pltpu.VMEM((2, tk, Q_ROWS), F32), pltpu.VMEM((2, 1, Q_ROWS), F32)]
    return pl.pallas_call(
        functools.partial(_attn_kernel, bounded=bounded),
        grid=(B, nq // ntiles),
        in_specs=[qspec,
                  pl.BlockSpec((None, S, LANES), lambda b, i: (b, 0, 0)),
                  pl.BlockSpec((None,) + vt.shape[1:], lambda b, i: (b, 0, 0, 0))],
        out_specs=pl.BlockSpec((None, ntiles, HEAD_DIM, Q_ROWS), lambda b, i: (b, i, 0, 0)),
        out_shape=jax.ShapeDtypeStruct((B, nq, HEAD_DIM, Q_ROWS), BF16),
        scratch_shapes=buffers + [pltpu.VMEM((ntiles, N_KV_HEADS, VT_ROWS, Q_COLS_PER_KV), F32)],
        compiler_params=_cparams("parallel", "parallel"),
        name="attn_bounded" if bounded else "attn",
    )(qt, k, vt)


def _dft_tables(S):
    n2 = DFT_N2
    n1 = S // n2
    c = np.arange(FOURIER_W)
    ang = 2.0 * np.pi * np.outer(c, c) / FOURIER_W
    chan = np.concatenate([np.cos(ang), -np.sin(ang)], axis=1) / np.sqrt(FOURIER_W)
    a1 = 2.0 * np.pi * np.outer(np.arange(n1), np.arange(n1)) / n1
    eye = np.eye(SUBLANES)
    mr = np.kron(np.cos(a1), eye) / np.sqrt(n1)
    mi = np.kron(-np.sin(a1), eye) / np.sqrt(n1)
    m1 = np.block([[mr, -mi], [mi, mr]])
    k1 = np.arange(n1)[None, :, None]
    s2 = (np.arange(n2 // SUBLANES)[:, None, None] * SUBLANES + np.arange(SUBLANES)[None, None, :])
    at = 2.0 * np.pi * (k1 * s2) / S
    twr = np.cos(at).reshape(n2 // SUBLANES, n1 * SUBLANES, 1)
    twi = (-np.sin(at)).reshape(n2 // SUBLANES, n1 * SUBLANES, 1)
    a2 = 2.0 * np.pi * np.outer(np.arange(n2), np.arange(n2)) / n2
    c2 = (np.cos(a2) / np.sqrt(n2))[:, None, None, :] * eye[None, :, :, None]
    s2m = (np.sin(a2) / np.sqrt(n2))[:, None, None, :] * eye[None, :, :, None]
    m2 = np.concatenate([c2.reshape(n2 * SUBLANES, n2 * SUBLANES),
                         s2m.reshape(n2 * SUBLANES, n2 * SUBLANES)], axis=1)
    c2s2 = np.concatenate([np.cos(a2), np.sin(a2)], axis=1) / np.sqrt(n2)
    tabs = dict(twr=jnp.asarray(twr, F32), twi=jnp.asarray(twi, F32))
    used = dict(chan=chan, m1=m1, c2=c2s2) if S <= FOURIER_FUSED_MAX_SEQ else dict(chan=chan, m1=m1, m2=m2)
    for name, table in used.items():
        tabs[name] = _round_table(jnp.asarray(table, F32))
    return tabs


def _round_kernel(x_ref, o_ref):
    o_ref[...] = x_ref[...].astype(o_ref.dtype)


def _round_table(x):
    rows, cols = x.shape
    blk = min(rows, ROUND_BLOCK_ROWS)
    spec = pl.BlockSpec((blk, cols), lambda i: (i, 0))
    return pl.pallas_call(
        _round_kernel, grid=(rows // blk,), in_specs=[spec], out_specs=spec,
        out_shape=jax.ShapeDtypeStruct(x.shape, BF16),
        compiler_params=_cparams("parallel"), name="round_table",
    )(x)


def _four1_kernel(fa_ref, chan_ref, m1_ref, twr_ref, twi_ref, br_ref, bi_ref):
    n1, sb, _ = fa_ref.shape
    rows = n1 * SUBLANES
    for i in range(sb // SUBLANES):
        rs = slice(i * SUBLANES, (i + 1) * SUBLANES)
        x = fa_ref[:, rs, :].reshape(rows, GROUP_W).astype(BF16)
        zr, zi = [], []
        for g in range(N_FOURIER_GROUPS):
            z = jnp.dot(x[:, g * FOURIER_W:(g + 1) * FOURIER_W], chan_ref[...],
                        preferred_element_type=F32)
            zr.append(z[:, :FOURIER_W])
            zi.append(z[:, FOURIER_W:])
        zcat = jnp.concatenate([jnp.concatenate(zr, axis=1), jnp.concatenate(zi, axis=1)],
                               axis=0).astype(BF16)
        a = jnp.dot(m1_ref[...], zcat, preferred_element_type=F32)
        ar, ai = a[:rows], a[rows:]
        twr, twi = twr_ref[i], twi_ref[i]
        br_ref[:, rs, :] = (ar * twr - ai * twi).reshape(n1, SUBLANES, GROUP_W)
        bi_ref[:, rs, :] = (ar * twi + ai * twr).reshape(n1, SUBLANES, GROUP_W)


def _four2_kernel(br_ref, bi_ref, m2_ref, wf_ref, fg_ref, o_ref):
    _, n2, _ = br_ref.shape
    rows = SUBLANES * n2
    bcat = jnp.concatenate([br_ref[...].reshape(rows, GROUP_W), bi_ref[...].reshape(rows, GROUP_W)],
                           axis=0).astype(BF16)
    f = jnp.dot(m2_ref[...], bcat, preferred_element_type=F32).astype(BF16)
    gate = _silu(fg_ref[...].reshape(rows, GROUP_W))
    ys = [jnp.dot(f[:, g * FOURIER_W:(g + 1) * FOURIER_W], wf_ref[g], preferred_element_type=F32)
          for g in range(N_FOURIER_GROUPS)]
    y = jnp.concatenate(ys, axis=1) * gate
    o_ref[...] = y.reshape(n2, SUBLANES, GROUP_W)


def _four_fused_kernel(fa_ref, fg_ref, chan_ref, m1_ref, twr_ref, twi_ref, c2_ref, wf_ref, o_ref,
                       zr_ref, zi_ref, br_ref, bi_ref, f_ref):
    S = fa_ref.shape[0]
    n2 = DFT_N2
    n1 = S // n2
    lanes = lambda g: slice(g * FOURIER_W, (g + 1) * FOURIER_W)

    x = fa_ref[...].astype(BF16)
    for g in range(N_FOURIER_GROUPS):
        z = jnp.dot(x[:, lanes(g)], chan_ref[...], preferred_element_type=F32)
        zr_ref[:, lanes(g)] = z[:, :FOURIER_W]
        zi_ref[:, lanes(g)] = z[:, FOURIER_W:]

    def stage1(j, carry):
        base = pl.multiple_of(j * SUBLANES, SUBLANES)
        tiles = ([zr_ref[pl.ds(s1 * n2 + base, SUBLANES), :] for s1 in range(n1)]
                 + [zi_ref[pl.ds(s1 * n2 + base, SUBLANES), :] for s1 in range(n1)])
        a = jnp.dot(m1_ref[...], jnp.concatenate(tiles, axis=0).astype(BF16),
                    preferred_element_type=F32)
        ar, ai = a[:n1 * SUBLANES], a[n1 * SUBLANES:]
        twr, twi = twr_ref[j], twi_ref[j]
        br = ar * twr - ai * twi
        bi = ar * twi + ai * twr
        for k1 in range(n1):
            rows = slice(k1 * SUBLANES, (k1 + 1) * SUBLANES)
            br_ref[pl.ds(k1 * n2 + base, SUBLANES), :] = br[rows]
            bi_ref[pl.ds(k1 * n2 + base, SUBLANES), :] = bi[rows]
        return carry

    lax.fori_loop(0, n2 // SUBLANES, stage1, 0, unroll=FOURIER_UNROLL)

    def stage2(k1, carry):
        start = pl.multiple_of(k1 * n2, n2)
        bcat = jnp.concatenate([br_ref[pl.ds(start, n2), :], bi_ref[pl.ds(start, n2), :]],
                               axis=0).astype(BF16)
        f = jnp.dot(c2_ref[...], bcat, preferred_element_type=F32)
        for g in range(N_FOURIER_GROUPS):
            f_ref[g, pl.ds(k1, n2, stride=n1), :] = f[:, lanes(g)]
        return carry

    lax.fori_loop(0, n1, stage2, 0, unroll=FOURIER_UNROLL)

    gate = _silu(fg_ref[...].astype(F32))
    for g in range(N_FOURIER_GROUPS):
        y = jnp.dot(f_ref[g].astype(BF16), wf_ref[g], preferred_element_type=F32)
        o_ref[:, lanes(g)] = (y * gate[:, lanes(g)]).astype(o_ref.dtype)


def _fourier_fused(fa, fg, wf, ft):
    B, S, _ = fa.shape
    n1 = S // DFT_N2
    tok = pl.BlockSpec((None, S, GROUP_W), lambda b: (b, 0, 0))
    return pl.pallas_call(
        _four_fused_kernel,
        grid=(B,),
        in_specs=[tok, tok, _const_spec((FOURIER_W, 2 * FOURIER_W)),
                  _const_spec((2 * n1 * SUBLANES, 2 * n1 * SUBLANES)),
                  _const_spec(ft["twr"].shape), _const_spec(ft["twi"].shape),
                  _const_spec((DFT_N2, 2 * DFT_N2)),
                  _const_spec((N_FOURIER_GROUPS, FOURIER_W, FOURIER_W))],
        out_specs=tok,
        out_shape=jax.ShapeDtypeStruct((B, S, GROUP_W), BF16),
        scratch_shapes=[pltpu.VMEM((S, GROUP_W), F32)] * 4
                       + [pltpu.VMEM((N_FOURIER_GROUPS, S, FOURIER_W), F32)],
        compiler_params=_cparams("parallel"),
        name="four",
    )(fa, fg, ft["chan"], ft["m1"], ft["twr"], ft["twi"], ft["c2"], wf)


def _fourier(fa, fg, wf, ft):
    B, S, _ = fa.shape
    if S <= FOURIER_FUSED_MAX_SEQ:
        return _fourier_fused(fa, fg, wf, ft)
    n2 = DFT_N2
    n1 = S // n2
    sb = SUBLANES
    nsub = sb // SUBLANES
    fa4 = fa.reshape(B, n1, n2, GROUP_W)
    blk1 = pl.BlockSpec((None, n1, sb, GROUP_W), lambda b, j: (b, 0, j, 0))
    tw = pl.BlockSpec((nsub, n1 * SUBLANES, 1), lambda b, j: (j, 0, 0))
    br, bi = pl.pallas_call(
        _four1_kernel,
        grid=(B, n2 // sb),
        in_specs=[blk1, _const_spec((FOURIER_W, 2 * FOURIER_W)),
                  _const_spec((2 * n1 * SUBLANES, 2 * n1 * SUBLANES)), tw, tw],
        out_specs=[blk1, blk1],
        out_shape=[jax.ShapeDtypeStruct((B, n1, n2, GROUP_W), F32)] * 2,
        compiler_params=_cparams("parallel", "parallel"),
        name="four1",
    )(fa4, ft["chan"], ft["m1"], ft["twr"], ft["twi"])
    blk_in = pl.BlockSpec((None, SUBLANES, n2, GROUP_W), lambda b, j: (b, j, 0, 0))
    blk_out = pl.BlockSpec((None, n2, SUBLANES, GROUP_W), lambda b, j: (b, 0, j, 0))
    o = pl.pallas_call(
        _four2_kernel,
        grid=(B, n1 // SUBLANES),
        in_specs=[blk_in, blk_in, _const_spec((SUBLANES * n2, 2 * SUBLANES * n2)),
                  _const_spec((N_FOURIER_GROUPS, FOURIER_W, FOURIER_W)), blk_out],
        out_specs=blk_out,
        out_shape=jax.ShapeDtypeStruct((B, n2, n1, GROUP_W), F32),
        compiler_params=_cparams("parallel", "parallel"),
        name="four2",
    )(br, bi, ft["m2"], wf, fg.reshape(B, n2, n1, GROUP_W))
    return o.reshape(B, S, GROUP_W)


def _outproj_kernel(ot_ref, ag_ref, of_ref, os_ref, om_ref, x_ref, w_ref, g_ref, y_ref):
    for s in range(ot_ref.shape[0] // OUT_SUB_TILES):
        rows = slice(s * OUT_SUB_TILES * Q_TILE, (s + 1) * OUT_SUB_TILES * Q_TILE)
        tiles = []
        for t in range(s * OUT_SUB_TILES, (s + 1) * OUT_SUB_TILES):
            ot = ot_ref[t].astype(F32)
            slabs = []
            for j in range(GROUP_W // LANES):
                pair = jnp.concatenate([ot[:, (2 * j) * Q_TILE:(2 * j + 1) * Q_TILE],
                                        ot[:, (2 * j + 1) * Q_TILE:(2 * j + 2) * Q_TILE]], axis=0)
                slabs.append(pair.T)
            tiles.append(jnp.concatenate(slabs, axis=1))
        oa = (jnp.concatenate(tiles, axis=0) * _silu(ag_ref[rows, :].astype(F32))).astype(BF16)
        rest = jnp.concatenate([of_ref[rows, :].astype(BF16), os_ref[rows, :], om_ref[rows, :]],
                               axis=1)
        y = jnp.dot(rest, w_ref[GROUP_W:, :], preferred_element_type=F32)
        y = y + jnp.dot(oa, w_ref[:GROUP_W, :], preferred_element_type=F32)
        y_ref[rows, :] = x_ref[rows, :] + _rms(y, g_ref[...])


def _outproj(ot, ag, of, osg, om, x, w, g):
    B, S, _ = x.shape
    tm = OUT_TILE
    tok = lambda wd: pl.BlockSpec((None, tm, wd), lambda b, i: (b, i, 0))
    return pl.pallas_call(
        _outproj_kernel,
        grid=(B, S // tm),
        in_specs=[pl.BlockSpec((None, tm // Q_TILE, HEAD_DIM, Q_ROWS), lambda b, i: (b, i, 0, 0)),
                  tok(GROUP_W), tok(GROUP_W), tok(GROUP_W), tok(GROUP_W), tok(D_MODEL),
                  _const_spec((4 * GROUP_W, D_MODEL)), _const_spec((1, D_MODEL))],
        out_specs=tok(D_MODEL),
        out_shape=jax.ShapeDtypeStruct((B, S, D_MODEL), F32),
        compiler_params=_cparams("parallel", "parallel"),
        name="outproj",
    )(ot, ag, of, osg, om, x, w, g)


def _rope_tables(S):
    rows = S // GRID_W
    row = jnp.broadcast_to(jnp.arange(rows, dtype=F32)[:, None], (rows, GRID_W)).reshape(S)
    col = jnp.broadcast_to(jnp.arange(GRID_W, dtype=F32)[None, :], (rows, GRID_W)).reshape(S)
    inv = ROPE_THETA ** (-jnp.arange(ROPE_PAIRS, dtype=F32) / ROPE_PAIRS)
    ang = jnp.stack([row[:, None] * inv, col[:, None] * inv], axis=1)
    cos, sin = jnp.cos(ang), jnp.sin(ang)
    zero = jnp.zeros((S, ROPE_PAIRS), F32)
    two = lambda parts: jnp.tile(jnp.concatenate(parts, axis=-1), (1, LANES // HEAD_DIM))
    return dict(cos=two([cos[:, 0], cos[:, 0], cos[:, 1], cos[:, 1]]),
                sina=two([zero, sin[:, 0], zero, sin[:, 1]]),
                sinb=two([-sin[:, 0], zero, -sin[:, 1], zero]))


def _layer_weights(l, pre_norm_g, w_in, q_norm_g, k_norm_g, w_fourier, sgu_norm_g, w_spatial,
                   b_spatial, mem_norm_g, w_mem_kv, w_out, post_norm_g):
    return dict(
        pre_g=pre_norm_g[l][None, :],
        w_in=w_in[l].astype(BF16),
        qg=jnp.tile(q_norm_g[l], N_HEADS)[None, :],
        kg=jnp.tile(k_norm_g[l], N_KV_HEADS)[None, :],
        wf=w_fourier[l].astype(BF16),
        vg=sgu_norm_g[l].reshape(1, GROUP_W),
        ws=w_spatial[l].astype(BF16),
        bs=jnp.repeat(b_spatial[l].T, SGU_W, axis=1),
        mem_g=mem_norm_g[l][None, :],
        w_mem_kv=w_mem_kv[l].astype(BF16),
        w_out=w_out[l].astype(BF16),
        post_g=post_norm_g[l][None, :],
        score_bound=(HEAD_DIM * Q_SCALE * jnp.max(jnp.abs(q_norm_g[l]))
                     * jnp.max(jnp.abs(k_norm_g[l]))),
    )


def _trunk(x, mem, layers):
    S = x.shape[1]
    tabs = _rope_tables(S)
    ids = np.arange(MXU_COLS) // HEAD_DIM
    tabs["bd"] = jnp.asarray(ids[:, None] == ids[None, :], BF16)
    ft = _dft_tables(S)
    for lw in layers:
        kv = _memkv(mem, lw["mem_g"], lw["w_mem_kv"])
        qt, k, vt, ag, fa, fg, osgu, omem = _inproj(x, kv, lw, tabs)
        ot = lax.cond(lw["score_bound"] < ATTN_UNSHIFTED_MAX_LOG2,
                      functools.partial(_attention, bounded=True),
                      functools.partial(_attention, bounded=False), qt, k, vt)
        ofour = _fourier(fa, fg, lw["wf"], ft)
        x = _outproj(ot, ag, ofour, osgu, omem, x, lw["w_out"], lw["post_g"])
    return x


def kernel(x_prompt, x_sample, mem_prompt, mem_sample, pre_norm_g, w_in, q_norm_g, k_norm_g,
           w_fourier, sgu_norm_g, w_spatial, b_spatial, mem_norm_g, w_mem_kv, w_out, post_norm_g):
    layers = [_layer_weights(l, pre_norm_g, w_in, q_norm_g, k_norm_g, w_fourier, sgu_norm_g,
                             w_spatial, b_spatial, mem_norm_g, w_mem_kv, w_out, post_norm_g)
              for l in range(DEPTH)]
    return (_trunk(x_prompt, mem_prompt, layers), _trunk(x_sample, mem_sample, layers))
```

```python
import functools

import numpy as np
import jax
import jax.numpy as jnp
from jax import lax
from jax.experimental import pallas as pl
from jax.experimental.pallas import tpu as pltpu

F32 = jnp.float32
BF16 = jnp.bfloat16

D_MODEL = 1024
DEPTH = 2
GRID_W = 64
N_MEM = 256
GROUP_W = 512
HEAD_DIM = 64
N_HEADS = 8
N_KV_HEADS = 2
Q_PER_KV = N_HEADS // N_KV_HEADS
ROPE_PAIRS = HEAD_DIM // 4
ROPE_THETA = 10000.0
N_FOURIER_GROUPS = 4
FOURIER_W = 128
N_SGU_HEADS = 4
SGU_W = 128
CHUNK = 128
N_MEM_HEADS = 4
MEM_HEAD_DIM = 128
EPS = 1e-6
IN_W = 4864
C_AQ, C_AK, C_AV, C_AG, C_FA, C_FG, C_SU, C_SV, C_SG, C_MQ, C_MG = (
    0, 512, 640, 768, 1280, 1792, 2304, 2816, 3328, 3840, 4352)

LANES = 128
SUBLANES = 8
MXU_COLS = 256
VMEM_LIMIT_BYTES = 56 * 1024 * 1024

TOKEN_TILE = 512
Q_TILE = 128
Q_ROWS = N_HEADS * Q_TILE
DFT_N2 = 128
FOURIER_FUSED_BYTES_PER_TOKEN = GROUP_W * (3 * 2 * 2 + 4 * 4 + 4)
FOURIER_FUSED_MAX_SEQ = VMEM_LIMIT_BYTES // FOURIER_FUSED_BYTES_PER_TOKEN
ROUND_BLOCK_ROWS = 256
FOURIER_UNROLL = 8
ATTN_UNROLL = 14
OUT_TILE = 1024
OUT_SUB_TILES = 2
ATTN_UNSHIFTED_MAX_LOG2 = 64.0
ATTN_LONG_TILES_PER_STEP = 2
ATTN_TILES_PER_STEP = 8
BF16_SUBLANES = 16
VT_ROWS = HEAD_DIM + BF16_SUBLANES
Q_COLS_PER_KV = Q_PER_KV * Q_TILE
Q_SCALE = float(HEAD_DIM ** -0.5 * np.log2(np.e))


def _cparams(*sem):
    return pltpu.CompilerParams(dimension_semantics=sem, vmem_limit_bytes=VMEM_LIMIT_BYTES)


def _const_spec(shape):
    nd = len(shape)
    return pl.BlockSpec(shape, lambda *_: (0,) * nd)


def _silu(g):
    return g / (1.0 + jnp.exp(-g))


def _rms(x, g):
    ms = jnp.mean(x * x, axis=-1, keepdims=True)
    return x * lax.rsqrt(ms + EPS) * g


def _memkv_kernel(mem_ref, g_ref, w_ref, kv_ref):
    h = _rms(mem_ref[...], g_ref[...]).astype(BF16)
    kv_ref[...] = jnp.dot(h, w_ref[...], preferred_element_type=F32).astype(BF16)


def _memkv(mem, g, w):
    B = mem.shape[0]
    return pl.pallas_call(
        _memkv_kernel,
        grid=(B,),
        in_specs=[pl.BlockSpec((None, N_MEM, D_MODEL), lambda b: (b, 0, 0)),
                  _const_spec((1, D_MODEL)),
                  _const_spec((D_MODEL, 2 * GROUP_W))],
        out_specs=pl.BlockSpec((None, N_MEM, 2 * GROUP_W), lambda b: (b, 0, 0)),
        out_shape=jax.ShapeDtypeStruct((B, N_MEM, 2 * GROUP_W), BF16),
        compiler_params=_cparams("parallel"),
        name="memkv",
    )(mem, g, w)


def _head_ssq(x, bd):
    sq = (x * x).astype(BF16)
    w = bd.shape[0]
    if x.shape[1] <= w:
        return jnp.dot(sq, bd[:x.shape[1], :x.shape[1]], preferred_element_type=F32)
    return jnp.concatenate([jnp.dot(sq[:, i:i + w], bd, preferred_element_type=F32)
                            for i in range(0, x.shape[1], w)], axis=1)


def _rope(x, cos, sina, sinb):
    return x * cos + pltpu.roll(x, 16, 1) * sina + pltpu.roll(x, LANES - 16, 1) * sinb


def _inproj_kernel(x_ref, pre_g_ref, w_ref, bd_ref, qg_ref, kg_ref, cos_ref, sina_ref, sinb_ref,
                   vg_ref, ws_ref, bs_ref, kv_ref,
                   qt_ref, k_ref, vt_ref, ag_ref, fa_ref, fg_ref, osgu_ref, omem_ref):
    tm = x_ref.shape[0]
    h = _rms(x_ref[...], pre_g_ref[...]).astype(BF16)

    def proj(lo, width):
        return jnp.dot(h, w_ref[:, lo:lo + width], preferred_element_type=F32)

    cos, sina, sinb = cos_ref[...], sina_ref[...], sinb_ref[...]
    nck = tm // CHUNK
    lanes = lambda i: slice(i * LANES, (i + 1) * LANES)

    aq = proj(C_AQ, GROUP_W)
    ak = proj(C_AK, LANES)
    av = proj(C_AV, LANES)
    mq = proj(C_MQ, GROUP_W)
    sv = proj(C_SV, GROUP_W)
    ssq = _head_ssq(aq, bd_ref[...])
    ssk = _head_ssq(ak, bd_ref[...])

    mq16 = mq.astype(BF16)
    scores = [lax.dot_general(mq16[:, lanes(hd)], kv_ref[:, lanes(hd)], (((1,), (1,)), ((), ())),
                              preferred_element_type=F32) * (MEM_HEAD_DIM ** -0.5)
              for hd in range(N_MEM_HEADS)]

    su = proj(C_SU, GROUP_W)
    sg = proj(C_SG, GROUP_W)
    mg = proj(C_MG, GROUP_W)

    vhs = [_rms(sv[:, lanes(hd)], vg_ref[:, lanes(hd)]).astype(BF16) for hd in range(N_SGU_HEADS)]
    spcs = [jnp.dot(ws_ref[hd],
                    jnp.concatenate([vhs[hd][c * CHUNK:(c + 1) * CHUNK, :] for c in range(nck)],
                                    axis=1), preferred_element_type=F32)
            for hd in range(N_SGU_HEADS)]

    ag_ref[...] = proj(C_AG, GROUP_W).astype(BF16)
    fa_ref[...] = proj(C_FA, GROUP_W).astype(fa_ref.dtype)
    fg_ref[...] = proj(C_FG, GROUP_W).astype(fg_ref.dtype)

    es = [jnp.exp(s - jnp.max(s, axis=-1, keepdims=True)) for s in scores]
    mgate = _silu(mg)
    for hd in range(N_MEM_HEADS):
        mv = kv_ref[:, GROUP_W + hd * MEM_HEAD_DIM:GROUP_W + (hd + 1) * MEM_HEAD_DIM]
        o = jnp.dot(es[hd].astype(BF16), mv, preferred_element_type=F32)
        o = o / jnp.sum(es[hd], axis=-1, keepdims=True)
        omem_ref[:, lanes(hd)] = (o * mgate[:, lanes(hd)]).astype(BF16)

    qn = aq * lax.rsqrt(ssq * (1.0 / HEAD_DIM) + EPS) * qg_ref[...]
    slabs = [_rope(qn[:, lanes(j)], cos, sina, sinb) * Q_SCALE for j in range(GROUP_W // LANES)]
    low = lax.broadcasted_iota(jnp.int32, (Q_TILE, LANES), 1) < HEAD_DIM
    for t in range(tm // Q_TILE):
        parts = []
        for j, slab in enumerate(slabs):
            sj = slab[t * Q_TILE:(t + 1) * Q_TILE, :]
            rolled = pltpu.roll(sj, HEAD_DIM, 1)
            if j < Q_PER_KV // 2:
                parts += [jnp.where(low, sj, 0.0), jnp.where(low, rolled, 0.0)]
            else:
                parts += [jnp.where(low, 0.0, rolled), jnp.where(low, 0.0, sj)]
        qt_ref[t] = jnp.concatenate(parts, axis=0).T.astype(BF16)

    kn = ak * lax.rsqrt(ssk * (1.0 / HEAD_DIM) + EPS) * kg_ref[...]
    k_ref[...] = _rope(kn, cos, sina, sinb).astype(BF16)
    avt = av.T.astype(BF16)
    for g in range(N_KV_HEADS):
        vt_ref[g * VT_ROWS:g * VT_ROWS + HEAD_DIM, :] = avt[g * HEAD_DIM:(g + 1) * HEAD_DIM, :]
        vt_ref[g * VT_ROWS + HEAD_DIM:(g + 1) * VT_ROWS, :] = jnp.ones((BF16_SUBLANES, tm), BF16)

    gate = _silu(sg)
    for hd in range(N_SGU_HEADS):
        for c in range(nck):
            rows = slice(c * CHUNK, (c + 1) * CHUNK)
            sp = spcs[hd][:, c * SGU_W:(c + 1) * SGU_W] + bs_ref[:, lanes(hd)]
            osgu_ref[rows, lanes(hd)] = (su[rows, lanes(hd)] * sp
                                         * gate[rows, lanes(hd)]).astype(BF16)


def _inproj(x, kv, lw, tabs):
    B, S, _ = x.shape
    tm = TOKEN_TILE
    nt = S // tm
    tok = lambda w: pl.BlockSpec((None, tm, w), lambda b, i: (b, i, 0))
    pos = pl.BlockSpec((tm, LANES), lambda b, i: (i, 0))
    in_specs = [
        tok(D_MODEL),
        _const_spec((1, D_MODEL)),
        _const_spec((D_MODEL, IN_W)),
        _const_spec((MXU_COLS, MXU_COLS)),
        _const_spec((1, GROUP_W)),
        _const_spec((1, LANES)),
        pos, pos, pos,
        _const_spec((1, GROUP_W)),
        _const_spec((N_SGU_HEADS, CHUNK, CHUNK)),
        _const_spec((CHUNK, GROUP_W)),
        pl.BlockSpec((None, N_MEM, 2 * GROUP_W), lambda b, i: (b, 0, 0)),
    ]
    out_specs = [
        pl.BlockSpec((None, tm // Q_TILE, LANES, Q_ROWS), lambda b, i: (b, i, 0, 0)), tok(LANES),
        pl.BlockSpec((None, None, N_KV_HEADS * VT_ROWS, tm), lambda b, i: (b, i, 0, 0)),
        tok(GROUP_W), tok(GROUP_W), tok(GROUP_W), tok(GROUP_W), tok(GROUP_W),
    ]
    sds = jax.ShapeDtypeStruct
    four_dt = BF16 if S <= FOURIER_FUSED_MAX_SEQ else F32
    out_shape = [
        sds((B, S // Q_TILE, LANES, Q_ROWS), BF16), sds((B, S, LANES), BF16),
        sds((B, nt, N_KV_HEADS * VT_ROWS, tm), BF16),
        sds((B, S, GROUP_W), BF16), sds((B, S, GROUP_W), four_dt), sds((B, S, GROUP_W), four_dt),
        sds((B, S, GROUP_W), BF16), sds((B, S, GROUP_W), BF16),
    ]
    return pl.pallas_call(
        _inproj_kernel,
        grid=(B, nt),
        in_specs=in_specs, out_specs=out_specs, out_shape=out_shape,
        compiler_params=_cparams("parallel", "parallel"),
        name="inproj",
    )(x, lw["pre_g"], lw["w_in"], tabs["bd"], lw["qg"], lw["kg"],
      tabs["cos"], tabs["sina"], tabs["sinb"], lw["vg"], lw["ws"], lw["bs"], kv)


def _attn_kernel(qt_ref, k_ref, vt_ref, ot_ref, *scratch, bounded):
    ntiles = qt_ref.shape[0]
    nchunks, _, tk = vt_ref.shape
    neg_inf = jnp.full((1, qt_ref.shape[2]), -jnp.inf, F32)

    if bounded:
        p_ref, acc_ref = scratch
    else:
        st_ref, mx_ref, acc_ref = scratch

    def scores(t, c, slot):
        start = c * tk if isinstance(c, int) else pl.multiple_of(c * tk, tk)
        st = jnp.dot(k_ref[pl.ds(start, tk), :], qt_ref[t], preferred_element_type=F32)
        if bounded:
            p_ref[slot] = jnp.exp2(st).astype(BF16)
        else:
            st_ref[slot] = st
            mx_ref[slot] = jnp.max(st, axis=0, keepdims=True)

    def softmax_pv(t, c, slot, m_old, first=False):
        if bounded:
            p, m_new, alpha = p_ref[slot], m_old, None
        else:
            m_new = mx_ref[slot]
            if not first:
                m_new = jnp.maximum(m_old, m_new)
            p = jnp.exp2(st_ref[slot] - m_new).astype(BF16)
            alpha = None if first else jnp.exp2(m_old - m_new)
        for g in range(N_KV_HEADS):
            cols = slice(g * Q_COLS_PER_KV, (g + 1) * Q_COLS_PER_KV)
            pv = jnp.dot(vt_ref[c, g * VT_ROWS:(g + 1) * VT_ROWS, :], p[:, cols],
                         preferred_element_type=F32)
            if first:
                acc_ref[t, g] = pv
            elif bounded:
                acc_ref[t, g] += pv
            else:
                acc_ref[t, g] = alpha[:, cols] * acc_ref[t, g] + pv
        return m_new

    def step_pair(nxt, t, c, slot, m_old, first=False):
        if not bounded or nxt is None:
            if nxt is not None:
                scores(*nxt)
            return softmax_pv(t, c, slot, m_old, first)
        tn, cn, sn = nxt
        start = cn * tk if isinstance(cn, int) else pl.multiple_of(cn * tk, tk)
        kc = k_ref[pl.ds(start, tk), :]
        for i in range(Q_ROWS // MXU_COLS):
            cols = slice(i * MXU_COLS, (i + 1) * MXU_COLS)
            st = jnp.dot(kc, qt_ref[tn, :, cols], preferred_element_type=F32)
            p_ref[sn, :, cols] = jnp.exp2(st).astype(BF16)
            g, off = divmod(i * MXU_COLS, Q_COLS_PER_KV)
            pv = jnp.dot(vt_ref[c, g * VT_ROWS:(g + 1) * VT_ROWS, :], p_ref[slot, :, cols],
                         preferred_element_type=F32)
            if first:
                acc_ref[t, g, :, off:off + MXU_COLS] = pv
            else:
                acc_ref[t, g, :, off:off + MXU_COLS] += pv
        return m_old

    def finish(t):
        for g in range(N_KV_HEADS):
            acc = acc_ref[t, g]
            ot_ref[t, :, g * Q_COLS_PER_KV:(g + 1) * Q_COLS_PER_KV] = (
                acc[:HEAD_DIM] / acc[HEAD_DIM:HEAD_DIM + 1]).astype(BF16)

    if nchunks <= ATTN_UNROLL:
        assert nchunks % 2 == 0

        def tile_steps(t, nxt):
            m = None
            for c in range(nchunks):
                if c + 1 < nchunks:
                    ahead = (t, c + 1, (c + 1) % 2)
                else:
                    ahead = None if nxt is None else (nxt, 0, 0)
                m = step_pair(ahead, t, c, c % 2, m, first=(c == 0))
            finish(t)

        scores(0, 0, 0)
        if ntiles > 1:
            def tile(t, carry):
                tile_steps(t, t + 1)
                return carry
            lax.fori_loop(0, ntiles - 1, tile, 0, unroll=True)
        tile_steps(ntiles - 1, None)
    else:
        assert nchunks % 2 == 0
        steady = nchunks - 1
        groups = (steady - 1) // ATTN_UNROLL
        peeled = steady - groups * ATTN_UNROLL
        last = nchunks - 1

        scores(0, 0, 0)
        for t in range(ntiles):
            def step(c, slot, m, first=False, t=t):
                return step_pair((t, c + 1, 1 - slot), t, c, slot, m, first)

            m = neg_inf
            for c in range(peeled):
                m = step(c, c % 2, m, first=(c == 0))

            def group(g, m, step=step):
                for u in range(ATTN_UNROLL):
                    m = step(peeled + g * ATTN_UNROLL + u, (peeled + u) % 2, m)
                return m
            m = lax.fori_loop(0, groups, group, m)
            ahead = (t + 1, 0, 0) if t + 1 < ntiles else None
            step_pair(ahead, t, last, last % 2, m)
            finish(t)


def _attention(qt, k, vt, bounded):
    B, nq = qt.shape[0], qt.shape[1]
    S = k.shape[1]
    tk = vt.shape[3]
    ntiles = min(nq, ATTN_TILES_PER_STEP if S // tk <= ATTN_UNROLL else ATTN_LONG_TILES_PER_STEP)
    qspec = pl.BlockSpec((None, ntiles, LANES, Q_ROWS), lambda b, i: (b, i, 0, 0))
    if bounded:
        buffers = [pltpu.VMEM((2, tk, Q_ROWS), BF16)]
    else:
        buffers = [pltpu.VMEM((2, tk, Q_ROWS), F32), pltpu.VMEM((2, 1, Q_ROWS), F32)]
    return pl.pallas_call(
        functools.partial(_attn_kernel, bounded=bounded),
        grid=(B, nq // ntiles),
        in_specs=[qspec,
                  pl.BlockSpec((None, S, LANES), lambda b, i: (b, 0, 0)),
                  pl.BlockSpec((None,) + vt.shape[1:], lambda b, i: (b, 0, 0, 0))],
        out_specs=pl.BlockSpec((None, ntiles, HEAD_DIM, Q_ROWS), lambda b, i: (b, i, 0, 0)),
        out_shape=jax.ShapeDtypeStruct((B, nq, HEAD_DIM, Q_ROWS), BF16),
        scratch_shapes=buffers + [pltpu.VMEM((ntiles, N_KV_HEADS, VT_ROWS, Q_COLS_PER_KV), F32)],
        compiler_params=_cparams("parallel", "parallel"),
        name="attn_bounded" if bounded else "attn",
    )(qt, k, vt)


def _dft_tables(S):
    n2 = DFT_N2
    n1 = S // n2
    c = np.arange(FOURIER_W)
    ang = 2.0 * np.pi * np.outer(c, c) / FOURIER_W
    chan = np.concatenate([np.cos(ang), -np.sin(ang)], axis=1) / np.sqrt(FOURIER_W)
    a1 = 2.0 * np.pi * np.outer(np.arange(n1), np.arange(n1)) / n1
    eye = np.eye(SUBLANES)
    mr = np.kron(np.cos(a1), eye) / np.sqrt(n1)
    mi = np.kron(-np.sin(a1), eye) / np.sqrt(n1)
    m1 = np.block([[mr, -mi], [mi, mr]])
    k1 = np.arange(n1)[None, :, None]
    s2 = (np.arange(n2 // SUBLANES)[:, None, None] * SUBLANES + np.arange(SUBLANES)[None, None, :])
    at = 2.0 * np.pi * (k1 * s2) / S
    twr = np.cos(at).reshape(n2 // SUBLANES, n1 * SUBLANES, 1)
    twi = (-np.sin(at)).reshape(n2 // SUBLANES, n1 * SUBLANES, 1)
    a2 = 2.0 * np.pi * np.outer(np.arange(n2), np.arange(n2)) / n2
    c2 = (np.cos(a2) / np.sqrt(n2))[:, None, None, :] * eye[None, :, :, None]
    s2m = (np.sin(a2) / np.sqrt(n2))[:, None, None, :] * eye[None, :, :, None]
    m2 = np.concatenate([c2.reshape(n2 * SUBLANES, n2 * SUBLANES),
                         s2m.reshape(n2 * SUBLANES, n2 * SUBLANES)], axis=1)
    c2s2 = np.concatenate([np.cos(a2), np.sin(a2)], axis=1) / np.sqrt(n2)
    tabs = dict(twr=jnp.asarray(twr, F32), twi=jnp.asarray(twi, F32))
    used = dict(chan=chan, m1=m1, c2=c2s2) if S <= FOURIER_FUSED_MAX_SEQ else dict(chan=chan, m1=m1, m2=m2)
    for name, table in used.items():
        tabs[name] = _round_table(jnp.asarray(table, F32))
    return tabs


def _round_kernel(x_ref, o_ref):
    o_ref[...] = x_ref[...].astype(o_ref.dtype)


def _round_table(x):
    rows, cols = x.shape
    blk = min(rows, ROUND_BLOCK_ROWS)
    spec = pl.BlockSpec((blk, cols), lambda i: (i, 0))
    return pl.pallas_call(
        _round_kernel, grid=(rows // blk,), in_specs=[spec], out_specs=spec,
        out_shape=jax.ShapeDtypeStruct(x.shape, BF16),
        compiler_params=_cparams("parallel"), name="round_table",
    )(x)


def _four1_kernel(fa_ref, chan_ref, m1_ref, twr_ref, twi_ref, br_ref, bi_ref):
    n1, sb, _ = fa_ref.shape
    rows = n1 * SUBLANES
    for i in range(sb // SUBLANES):
        rs = slice(i * SUBLANES, (i + 1) * SUBLANES)
        x = fa_ref[:, rs, :].reshape(rows, GROUP_W).astype(BF16)
        zr, zi = [], []
        for g in range(N_FOURIER_GROUPS):
            z = jnp.dot(x[:, g * FOURIER_W:(g + 1) * FOURIER_W], chan_ref[...],
                        preferred_element_type=F32)
            zr.append(z[:, :FOURIER_W])
            zi.append(z[:, FOURIER_W:])
        zcat = jnp.concatenate([jnp.concatenate(zr, axis=1), jnp.concatenate(zi, axis=1)],
                               axis=0).astype(BF16)
        a = jnp.dot(m1_ref[...], zcat, preferred_element_type=F32)
        ar, ai = a[:rows], a[rows:]
        twr, twi = twr_ref[i], twi_ref[i]
        br_ref[:, rs, :] = (ar * twr - ai * twi).reshape(n1, SUBLANES, GROUP_W)
        bi_ref[:, rs, :] = (ar * twi + ai * twr).reshape(n1, SUBLANES, GROUP_W)


def _four2_kernel(br_ref, bi_ref, m2_ref, wf_ref, fg_ref, o_ref):
    _, n2, _ = br_ref.shape
    rows = SUBLANES * n2
    bcat = jnp.concatenate([br_ref[...].reshape(rows, GROUP_W), bi_ref[...].reshape(rows, GROUP_W)],
                           axis=0).astype(BF16)
    f = jnp.dot(m2_ref[...], bcat, preferred_element_type=F32).astype(BF16)
    gate = _silu(fg_ref[...].reshape(rows, GROUP_W))
    ys = [jnp.dot(f[:, g * FOURIER_W:(g + 1) * FOURIER_W], wf_ref[g], preferred_element_type=F32)
          for g in range(N_FOURIER_GROUPS)]
    y = jnp.concatenate(ys, axis=1) * gate
    o_ref[...] = y.reshape(n2, SUBLANES, GROUP_W)


def _four_fused_kernel(fa_ref, fg_ref, chan_ref, m1_ref, twr_ref, twi_ref, c2_ref, wf_ref, o_ref,
                       zr_ref, zi_ref, br_ref, bi_ref, f_ref):
    S = fa_ref.shape[0]
    n2 = DFT_N2
    n1 = S // n2
    lanes = lambda g: slice(g * FOURIER_W, (g + 1) * FOURIER_W)

    x = fa_ref[...].astype(BF16)
    for g in range(N_FOURIER_GROUPS):
        z = jnp.dot(x[:, lanes(g)], chan_ref[...], preferred_element_type=F32)
        zr_ref[:, lanes(g)] = z[:, :FOURIER_W]
        zi_ref[:, lanes(g)] = z[:, FOURIER_W:]

    def stage1(j, carry):
        base = pl.multiple_of(j * SUBLANES, SUBLANES)
        tiles = ([zr_ref[pl.ds(s1 * n2 + base, SUBLANES), :] for s1 in range(n1)]
                 + [zi_ref[pl.ds(s1 * n2 + base, SUBLANES), :] for s1 in range(n1)])
        a = jnp.dot(m1_ref[...], jnp.concatenate(tiles, axis=0).astype(BF16),
                    preferred_element_type=F32)
        ar, ai = a[:n1 * SUBLANES], a[n1 * SUBLANES:]
        twr, twi = twr_ref[j], twi_ref[j]
        br = ar * twr - ai * twi
        bi = ar * twi + ai * twr
        for k1 in range(n1):
            rows = slice(k1 * SUBLANES, (k1 + 1) * SUBLANES)
            br_ref[pl.ds(k1 * n2 + base, SUBLANES), :] = br[rows]
            bi_ref[pl.ds(k1 * n2 + base, SUBLANES), :] = bi[rows]
        return carry

    lax.fori_loop(0, n2 // SUBLANES, stage1, 0, unroll=FOURIER_UNROLL)

    def stage2(k1, carry):
        start = pl.multiple_of(k1 * n2, n2)
        bcat = jnp.concatenate([br_ref[pl.ds(start, n2), :], bi_ref[pl.ds(start, n2), :]],
                               axis=0).astype(BF16)
        f = jnp.dot(c2_ref[...], bcat, preferred_element_type=F32)
        for g in range(N_FOURIER_GROUPS):
            f_ref[g, pl.ds(k1, n2, stride=n1), :] = f[:, lanes(g)]
        return carry

    lax.fori_loop(0, n1, stage2, 0, unroll=FOURIER_UNROLL)

    gate = _silu(fg_ref[...].astype(F32))
    for g in range(N_FOURIER_GROUPS):
        y = jnp.dot(f_ref[g].astype(BF16), wf_ref[g], preferred_element_type=F32)
        o_ref[:, lanes(g)] = (y * gate[:, lanes(g)]).astype(o_ref.dtype)


def _fourier_fused(fa, fg, wf, ft):
    B, S, _ = fa.shape
    n1 = S // DFT_N2
    tok = pl.BlockSpec((None, S, GROUP_W), lambda b: (b, 0, 0))
    return pl.pallas_call(
        _four_fused_kernel,
        grid=(B,),
        in_specs=[tok, tok, _const_spec((FOURIER_W, 2 * FOURIER_W)),
                  _const_spec((2 * n1 * SUBLANES, 2 * n1 * SUBLANES)),
                  _const_spec(ft["twr"].shape), _const_spec(ft["twi"].shape),
                  _const_spec((DFT_N2, 2 * DFT_N2)),
                  _const_spec((N_FOURIER_GROUPS, FOURIER_W, FOURIER_W))],
        out_specs=tok,
        out_shape=jax.ShapeDtypeStruct((B, S, GROUP_W), BF16),
        scratch_shapes=[pltpu.VMEM((S, GROUP_W), F32)] * 4
                       + [pltpu.VMEM((N_FOURIER_GROUPS, S, FOURIER_W), F32)],
        compiler_params=_cparams("parallel"),
        name="four",
    )(fa, fg, ft["chan"], ft["m1"], ft["twr"], ft["twi"], ft["c2"], wf)


def _fourier(fa, fg, wf, ft):
    B, S, _ = fa.shape
    if S <= FOURIER_FUSED_MAX_SEQ:
        return _fourier_fused(fa, fg, wf, ft)
    n2 = DFT_N2
    n1 = S // n2
    sb = SUBLANES
    nsub = sb // SUBLANES
    fa4 = fa.reshape(B, n1, n2, GROUP_W)
    blk1 = pl.BlockSpec((None, n1, sb, GROUP_W), lambda b, j: (b, 0, j, 0))
    tw = pl.BlockSpec((nsub, n1 * SUBLANES, 1), lambda b, j: (j, 0, 0))
    br, bi = pl.pallas_call(
        _four1_kernel,
        grid=(B, n2 // sb),
        in_specs=[blk1, _const_spec((FOURIER_W, 2 * FOURIER_W)),
                  _const_spec((2 * n1 * SUBLANES, 2 * n1 * SUBLANES)), tw, tw],
        out_specs=[blk1, blk1],
        out_shape=[jax.ShapeDtypeStruct((B, n1, n2, GROUP_W), F32)] * 2,
        compiler_params=_cparams("parallel", "parallel"),
        name="four1",
    )(fa4, ft["chan"], ft["m1"], ft["twr"], ft["twi"])
    blk_in = pl.BlockSpec((None, SUBLANES, n2, GROUP_W), lambda b, j: (b, j, 0, 0))
    blk_out = pl.BlockSpec((None, n2, SUBLANES, GROUP_W), lambda b, j: (b, 0, j, 0))
    o = pl.pallas_call(
        _four2_kernel,
        grid=(B, n1 // SUBLANES),
        in_specs=[blk_in, blk_in, _const_spec((SUBLANES * n2, 2 * SUBLANES * n2)),
                  _const_spec((N_FOURIER_GROUPS, FOURIER_W, FOURIER_W)), blk_out],
        out_specs=blk_out,
        out_shape=jax.ShapeDtypeStruct((B, n2, n1, GROUP_W), F32),
        compiler_params=_cparams("parallel", "parallel"),
        name="four2",
    )(br, bi, ft["m2"], wf, fg.reshape(B, n2, n1, GROUP_W))
    return o.reshape(B, S, GROUP_W)


def _outproj_kernel(ot_ref, ag_ref, of_ref, os_ref, om_ref, x_ref, w_ref, g_ref, y_ref):
    for s in range(ot_ref.shape[0] // OUT_SUB_TILES):
        rows = slice(s * OUT_SUB_TILES * Q_TILE, (s + 1) * OUT_SUB_TILES * Q_TILE)
        tiles = []
        for t in range(s * OUT_SUB_TILES, (s + 1) * OUT_SUB_TILES):
            ot = ot_ref[t].astype(F32)
            slabs = []
            for j in range(GROUP_W // LANES):
                pair = jnp.concatenate([ot[:, (2 * j) * Q_TILE:(2 * j + 1) * Q_TILE],
                                        ot[:, (2 * j + 1) * Q_TILE:(2 * j + 2) * Q_TILE]], axis=0)
                slabs.append(pair.T)
            tiles.append(jnp.concatenate(slabs, axis=1))
        oa = (jnp.concatenate(tiles, axis=0) * _silu(ag_ref[rows, :].astype(F32))).astype(BF16)
        rest = jnp.concatenate([of_ref[rows, :].astype(BF16), os_ref[rows, :], om_ref[rows, :]],
                               axis=1)
        y = jnp.dot(rest, w_ref[GROUP_W:, :], preferred_element_type=F32)
        y = y + jnp.dot(oa, w_ref[:GROUP_W, :], preferred_element_type=F32)
        y_ref[rows, :] = x_ref[rows, :] + _rms(y, g_ref[...])


def _outproj(ot, ag, of, osg, om, x, w, g):
    B, S, _ = x.shape
    tm = OUT_TILE
    tok = lambda wd: pl.BlockSpec((None, tm, wd), lambda b, i: (b, i, 0))
    return pl.pallas_call(
        _outproj_kernel,
        grid=(B, S // tm),
        in_specs=[pl.BlockSpec((None, tm // Q_TILE, HEAD_DIM, Q_ROWS), lambda b, i: (b, i, 0, 0)),
                  tok(GROUP_W), tok(GROUP_W), tok(GROUP_W), tok(GROUP_W), tok(D_MODEL),
                  _const_spec((4 * GROUP_W, D_MODEL)), _const_spec((1, D_MODEL))],
        out_specs=tok(D_MODEL),
        out_shape=jax.ShapeDtypeStruct((B, S, D_MODEL), F32),
        compiler_params=_cparams("parallel", "parallel"),
        name="outproj",
    )(ot, ag, of, osg, om, x, w, g)


def _rope_tables(S):
    rows = S // GRID_W
    row = jnp.broadcast_to(jnp.arange(rows, dtype=F32)[:, None], (rows, GRID_W)).reshape(S)
    col = jnp.broadcast_to(jnp.arange(GRID_W, dtype=F32)[None, :], (rows, GRID_W)).reshape(S)
    inv = ROPE_THETA ** (-jnp.arange(ROPE_PAIRS, dtype=F32) / ROPE_PAIRS)
    ang = jnp.stack([row[:, None] * inv, col[:, None] * inv], axis=1)
    cos, sin = jnp.cos(ang), jnp.sin(ang)
    zero = jnp.zeros((S, ROPE_PAIRS), F32)
    two = lambda parts: jnp.tile(jnp.concatenate(parts, axis=-1), (1, LANES // HEAD_DIM))
    return dict(cos=two([cos[:, 0], cos[:, 0], cos[:, 1], cos[:, 1]]),
                sina=two([zero, sin[:, 0], zero, sin[:, 1]]),
                sinb=two([-sin[:, 0], zero, -sin[:, 1], zero]))


def _layer_weights(l, pre_norm_g, w_in, q_norm_g, k_norm_g, w_fourier, sgu_norm_g, w_spatial,
                   b_spatial, mem_norm_g, w_mem_kv, w_out, post_norm_g):
    return dict(
        pre_g=pre_norm_g[l][None, :],
        w_in=w_in[l].astype(BF16),
        qg=jnp.tile(q_norm_g[l], N_HEADS)[None, :],
        kg=jnp.tile(k_norm_g[l], N_KV_HEADS)[None, :],
        wf=w_fourier[l].astype(BF16),
        vg=sgu_norm_g[l].reshape(1, GROUP_W),
        ws=w_spatial[l].astype(BF16),
        bs=jnp.repeat(b_spatial[l].T, SGU_W, axis=1),
        mem_g=mem_norm_g[l][None, :],
        w_mem_kv=w_mem_kv[l].astype(BF16),
        w_out=w_out[l].astype(BF16),
        post_g=post_norm_g[l][None, :],
        score_bound=(HEAD_DIM * Q_SCALE * jnp.max(jnp.abs(q_norm_g[l]))
                     * jnp.max(jnp.abs(k_norm_g[l]))),
    )


def _trunk(x, mem, layers):
    S = x.shape[1]
    tabs = _rope_tables(S)
    ids = np.arange(MXU_COLS) // HEAD_DIM
    tabs["bd"] = jnp.asarray(ids[:, None] == ids[None, :], BF16)
    ft = _dft_tables(S)
    for lw in layers:
        kv = _memkv(mem, lw["mem_g"], lw["w_mem_kv"])
        qt, k, vt, ag, fa, fg, osgu, omem = _inproj(x, kv, lw, tabs)
        ot = lax.cond(lw["score_bound"] < ATTN_UNSHIFTED_MAX_LOG2,
                      functools.partial(_attention, bounded=True),
                      functools.partial(_attention, bounded=False), qt, k, vt)
        ofour = _fourier(fa, fg, lw["wf"], ft)
        x = _outproj(ot, ag, ofour, osgu, omem, x, lw["w_out"], lw["post_g"])
    return x


def kernel(x_prompt, x_sample, mem_prompt, mem_sample, pre_norm_g, w_in, q_norm_g, k_norm_g,
           w_fourier, sgu_norm_g, w_spatial, b_spatial, mem_norm_g, w_mem_kv, w_out, post_norm_g):
    layers = [_layer_weights(l, pre_norm_g, w_in, q_norm_g, k_norm_g, w_fourier, sgu_norm_g,
                             w_spatial, b_spatial, mem_norm_g, w_mem_kv, w_out, post_norm_g)
              for l in range(DEPTH)]
    return (_trunk(x_prompt, mem_prompt, layers), _trunk(x_sample, mem_sample, layers))
```
